```python
import math
import jax
import jax.numpy as jnp
from jax import lax
import numpy as np

D_MODEL = 2048
BATCH = 4
SEQ = 2048
DEPTH = 1
DEC_BATCH = 128
DEC_SEQ = 8
PAST_LEN = 16384
PAGE_SIZE = 128

N_META = 16
EPS = 1e-6
SSD_INNER = D_MODEL
SSD_HEADDIM = 64
SSD_HEADS = SSD_INNER // SSD_HEADDIM
SSD_GROUPS = 4
SSD_STATE = 128
SSD_CONV = 4
SSD_CHUNK = 128
SSD_CONV_DIM = SSD_INNER + 2 * SSD_GROUPS * SSD_STATE
POOL_WIDTH = D_MODEL
POOL_WINDOWS = (2, 4, 8, 16)
POOL_GROUPS = 4
POOL_GROUP_WIDTH = POOL_WIDTH // POOL_GROUPS
POOL_MAX = 16
N_BRANCH = 2
OFF_Z_SSD = 0
OFF_XBC = OFF_Z_SSD + SSD_INNER
OFF_DT = OFF_XBC + SSD_CONV_DIM
OFF_Z_POOL = OFF_DT + SSD_HEADS
OFF_U_POOL = OFF_Z_POOL + POOL_WIDTH
OFF_GATES = OFF_U_POOL + POOL_WIDTH
D_IN_PROJ = OFF_GATES + N_BRANCH * D_MODEL

kernel_name = "hybrid_ssd_pool_gated_decode_step"


def rms_norm(x, w):
    xf = x.astype(jnp.float32)
    xf = xf * lax.rsqrt(jnp.mean(xf * xf, axis=-1, keepdims=True) + EPS)
    return (xf * w.astype(jnp.float32)).astype(x.dtype)


def causal_dwconv(xbc, hist, w, b):
    ext = jnp.concatenate([hist.astype(xbc.dtype), xbc], axis=1)
    l = xbc.shape[1]
    acc = b
    for k in range(SSD_CONV):
        acc = acc + ext[:, k:k + l] * w[k]
    return jax.nn.silu(acc), ext[:, ext.shape[1] - (SSD_CONV - 1):]


def ssd_segment(x, dt, a, bm, cm, h0, q):
    bsz, l, n_h, p = x.shape
    g, n = bm.shape[2], bm.shape[3]
    r = n_h // g
    nc = l // q
    xr = x.reshape(bsz, nc, q, g, r, p)
    dtr = dt.reshape(bsz, nc, q, g, r)
    br = bm.reshape(bsz, nc, q, g, n)
    cr = cm.reshape(bsz, nc, q, g, n)
    a_cum = jnp.cumsum(dtr * a.reshape(g, r), axis=2)
    xdt = xr * dtr[..., None]
    causal = jnp.tril(jnp.ones((q, q), dtype=bool))
    seg = a_cum[:, :, :, None] - a_cum[:, :, None, :]
    decay = jnp.exp(jnp.where(causal[:, :, None, None], seg, -jnp.inf))
    cb = jnp.einsum('bctgn,bcsgn->bctsg', cr, br)
    y_diag = jnp.einsum('bctsg,bctsgr,bcsgrp->bctgrp', cb, decay, xdt)
    decay_end = jnp.exp(a_cum[:, :, -1:] - a_cum)
    chunk_states = jnp.einsum('bcsgn,bcsgr,bcsgrp->bcgrpn', br, decay_end, xdt)
    chunk_decay = jnp.exp(a_cum[:, :, -1])

    def step(h, inp):
        s_c, d_c = inp
        return h * d_c[..., None, None] + s_c, h

    h_final, h_in = lax.scan(step, h0.reshape(bsz, g, r, p, n),
                             (jnp.moveaxis(chunk_states, 1, 0), jnp.moveaxis(chunk_decay, 1, 0)))
    h_in = jnp.moveaxis(h_in, 0, 1)
    y_off = jnp.einsum('bctgn,bctgr,bcgrpn->bctgrp', cr, jnp.exp(a_cum), h_in)
    y = (y_diag + y_off).reshape(bsz, l, n_h, p)
    return y, h_final.reshape(bsz, n_h, p, n)


def pool_means(ext, n_new, first_pos):
    bsz, tot, c = ext.shape
    p = tot - n_new
    s = jnp.concatenate([jnp.zeros((bsz, POOL_MAX, c), jnp.float32),
                         jnp.cumsum(ext.astype(jnp.float32), axis=1)], axis=1)
    pos = first_pos + jnp.arange(n_new) + 1
    outs = []
    for gi, w in enumerate(POOL_WINDOWS):
        cs = slice(gi * POOL_GROUP_WIDTH, (gi + 1) * POOL_GROUP_WIDTH)
        upper = s[:, p + POOL_MAX:p + POOL_MAX + n_new, cs]
        lower = s[:, p + POOL_MAX - w:p + POOL_MAX - w + n_new, cs]
        count = jnp.minimum(w, pos).astype(jnp.float32)
        outs.append((upper - lower) / count[None, :, None])
    return jnp.concatenate(outs, axis=-1).astype(ext.dtype)


def mixer_layer(h, conv_hist, ssm_h0, pool_hist, first_pos, segments, lp):
    (norm_w, w_in, conv_w, conv_b, dt_bias, a_log, d_skip, ssd_norm_w, w_proj_ssd,
     pool_mix_w, pool_mix_b, pool_scale, w_proj_pool, w_out) = lp
    f32 = jnp.float32
    bsz, l, _ = h.shape
    xn = rms_norm(h, norm_w)
    proj = jnp.einsum('bld,de->ble', xn, w_in)
    z_ssd = proj[..., OFF_Z_SSD:OFF_XBC]
    xbc = proj[..., OFF_XBC:OFF_DT]
    dt_raw = proj[..., OFF_DT:OFF_Z_POOL]
    z_pool = proj[..., OFF_Z_POOL:OFF_U_POOL]
    u = proj[..., OFF_U_POOL:OFF_GATES]
    gate_ssd = jax.nn.sigmoid(proj[..., OFF_GATES:OFF_GATES + D_MODEL])
    gate_pool = jax.nn.sigmoid(proj[..., OFF_GATES + D_MODEL:D_IN_PROJ])

    xbc_act, conv_new = causal_dwconv(xbc, conv_hist, conv_w, conv_b)
    xbc_act = xbc_act.astype(f32)
    nb = SSD_GROUPS * SSD_STATE
    xs = xbc_act[..., :SSD_INNER].reshape(bsz, l, SSD_HEADS, SSD_HEADDIM)
    b_ssm = xbc_act[..., SSD_INNER:SSD_INNER + nb].reshape(bsz, l, SSD_GROUPS, SSD_STATE)
    c_ssm = xbc_act[..., SSD_INNER + nb:].reshape(bsz, l, SSD_GROUPS, SSD_STATE)
    dt = jax.nn.softplus(dt_raw.astype(f32) + dt_bias.astype(f32))
    a = -jnp.exp(a_log.astype(f32))
    state = ssm_h0.astype(f32)
    ys = []
    start = 0
    for seg_len, q in segments:
        sl = slice(start, start + seg_len)
        y_seg, state = ssd_segment(xs[:, sl], dt[:, sl], a, b_ssm[:, sl], c_ssm[:, sl], state, q)
        ys.append(y_seg)
        start += seg_len
    y = jnp.concatenate(ys, axis=1) + xs * d_skip.astype(f32)[:, None]
    y = y.reshape(bsz, l, SSD_INNER) * jax.nn.silu(z_ssd.astype(f32))
    yg = y.reshape(bsz, l, SSD_GROUPS, SSD_INNER // SSD_GROUPS)
    yg = yg * lax.rsqrt(jnp.mean(yg * yg, axis=-1, keepdims=True) + EPS)
    y = (yg.reshape(bsz, l, SSD_INNER) * ssd_norm_w.astype(f32)).astype(h.dtype)
    branch_ssd = y @ w_proj_ssd

    pool_ext = jnp.concatenate([pool_hist.astype(u.dtype), u], axis=1)
    means = pool_means(pool_ext, l, first_pos)
    pooled = (means - u).reshape(bsz, l, POOL_GROUPS, POOL_GROUP_WIDTH)
    mixed = jnp.einsum('blgc,gcd->blgd', pooled, pool_mix_w) + pool_mix_b
    p_out = mixed.reshape(bsz, l, POOL_WIDTH) * pool_scale * jax.nn.silu(z_pool)
    branch_pool = p_out @ w_proj_pool

    merged = gate_ssd * branch_ssd + gate_pool * branch_pool
    h_new = h + merged @ w_out
    pool_new = pool_ext[:, pool_ext.shape[1] - (POOL_MAX - 1):]
    return (h_new, conv_new.astype(conv_hist.dtype), state.astype(ssm_h0.dtype),
            pool_new.astype(pool_hist.dtype))


def setup_inputs(seed: int = 0) -> dict:
    key = jax.random.key(seed)
    ks = jax.random.split(key, 24)
    f32 = jnp.float32

    def nrm(k, shape, scale):
        return scale * jax.random.normal(k, shape, f32)

    dt0 = jnp.exp(jax.random.uniform(ks[10], (DEPTH, SSD_HEADS), f32, math.log(1e-3), math.log(1e-1)))
    return {
        "x_prompt": nrm(ks[0], (BATCH, SEQ, D_MODEL), 1.0),
        "x_sample": nrm(ks[1], (DEC_BATCH, DEC_SEQ, D_MODEL), 1.0),
        "state_conv": nrm(ks[2], (DEPTH, DEC_BATCH, SSD_CONV - 1, SSD_CONV_DIM), 1.0),
        "state_ssm": nrm(ks[3], (DEPTH, DEC_BATCH, SSD_HEADS, SSD_HEADDIM, SSD_STATE), 0.1),
        "state_pool": nrm(ks[4], (DEPTH, DEC_BATCH, POOL_MAX - 1, POOL_WIDTH), 1.0),
        "meta_tokens": nrm(ks[5], (N_META, D_MODEL), 1.0),
        "norm_w": 1.0 + nrm(ks[6], (DEPTH, D_MODEL), 0.02),
        "w_in": nrm(ks[7], (DEPTH, D_MODEL, D_IN_PROJ), D_MODEL ** -0.5),
        "conv_w": nrm(ks[8], (DEPTH, SSD_CONV, SSD_CONV_DIM), SSD_CONV ** -0.5),
        "conv_b": nrm(ks[9], (DEPTH, SSD_CONV_DIM), 0.01),
        "dt_bias": dt0 + jnp.log(-jnp.expm1(-dt0)),
        "a_log": jnp.log(jax.random.uniform(ks[11], (DEPTH, SSD_HEADS), f32, 1.0, 16.0)),
        "d_skip": 1.0 + nrm(ks[12], (DEPTH, SSD_HEADS), 0.02),
        "ssd_norm_w": 1.0 + nrm(ks[13], (DEPTH, SSD_INNER), 0.02),
        "w_proj_ssd": nrm(ks[14], (DEPTH, SSD_INNER, D_MODEL), SSD_INNER ** -0.5),
        "pool_mix_w": nrm(ks[15], (DEPTH, POOL_GROUPS, POOL_GROUP_WIDTH, POOL_GROUP_WIDTH), POOL_GROUP_WIDTH ** -0.5),
        "pool_mix_b": nrm(ks[16], (DEPTH, POOL_GROUPS, POOL_GROUP_WIDTH), 0.01),
        "pool_scale": 1.0 + nrm(ks[17], (DEPTH, POOL_WIDTH), 0.02),
        "w_proj_pool": nrm(ks[18], (DEPTH, POOL_WIDTH, D_MODEL), POOL_WIDTH ** -0.5),
        "w_out": nrm(ks[19], (DEPTH, D_MODEL, D_MODEL), D_MODEL ** -0.5),
        "final_norm_w": 1.0 + nrm(ks[20], (D_MODEL,), 0.02),
    }


def reference(x_prompt, x_sample, state_conv, state_ssm, state_pool, meta_tokens, norm_w, w_in,
              conv_w, conv_b, dt_bias, a_log, d_skip, ssd_norm_w, w_proj_ssd, pool_mix_w,
              pool_mix_b, pool_scale, w_proj_pool, w_out, final_norm_w):
    n_b, seq, _ = x_prompt.shape
    dec_seq = x_sample.shape[1]
    meta = jnp.broadcast_to(meta_tokens[None].astype(x_prompt.dtype), (n_b, N_META, D_MODEL))
    h_p = jnp.concatenate([meta, x_prompt], axis=1)
    h_s = x_sample
    seg_p = ((N_META, N_META), (seq, SSD_CHUNK))
    seg_s = ((dec_seq, dec_seq),)
    conv_p, ssm_p, pool_p, conv_s, ssm_s, pool_s = [], [], [], [], [], []
    for layer in range(DEPTH):
        lp = (norm_w[layer], w_in[layer], conv_w[layer], conv_b[layer], dt_bias[layer], a_log[layer],
              d_skip[layer], ssd_norm_w[layer], w_proj_ssd[layer], pool_mix_w[layer], pool_mix_b[layer],
              pool_scale[layer], w_proj_pool[layer], w_out[layer])
        h_p, c_new, s_new, q_new = mixer_layer(
            h_p,
            jnp.zeros((n_b, SSD_CONV - 1, SSD_CONV_DIM), state_conv.dtype),
            jnp.zeros((n_b, SSD_HEADS, SSD_HEADDIM, SSD_STATE), state_ssm.dtype),
            jnp.zeros((n_b, 0, POOL_WIDTH), state_pool.dtype),
            0, seg_p, lp)
        conv_p.append(c_new)
        ssm_p.append(s_new)
        pool_p.append(q_new)
        h_s, c_new, s_new, q_new = mixer_layer(
            h_s, state_conv[layer], state_ssm[layer], state_pool[layer], PAST_LEN, seg_s, lp)
        conv_s.append(c_new)
        ssm_s.append(s_new)
        pool_s.append(q_new)
    y_prompt = rms_norm(h_p, final_norm_w)[:, N_META:]
    y_sample = rms_norm(h_s, final_norm_w)
    new_conv_prompt = jnp.stack(conv_p)
    new_ssm_prompt = jnp.stack(ssm_p)
    new_pool_prompt = jnp.stack(pool_p)
    new_conv_sample = jnp.stack(conv_s)
    new_ssm_sample = jnp.stack(ssm_s)
    new_pool_sample = jnp.stack(pool_s)
    return (y_prompt, y_sample, new_conv_prompt, new_ssm_prompt, new_pool_prompt,
            new_conv_sample, new_ssm_sample, new_pool_sample)
```

```python
import functools

import jax
import jax.numpy as jnp
from jax import lax
from jax.experimental import pallas as pl
from jax.experimental.pallas import tpu as pltpu

F32 = jnp.float32
BF16 = jnp.bfloat16

D_MODEL = 2048
N_META = 16
EPS = 1e-6
HEADS = 32
HEADDIM = 64
GROUPS = 4
STATE = 128
CONV_K = 4
INNER = HEADS * HEADDIM
GROUP_W = INNER // GROUPS
CONV_DIM = INNER + 2 * GROUPS * STATE
POOL_WINDOWS = (2, 4, 8, 16)
POOL_MAX = 16
POOL_GW = D_MODEL // len(POOL_WINDOWS)

N_MAIN = 5 * D_MODEL + CONV_DIM
CB_ZS, CB_ZP, CB_U, CB_G1, CB_G2, CB_XS = 0, 1, 2, 3, 4, 5
CB_BC = 12
OFF_U = CB_U * D_MODEL
OFF_XBC = CB_XS * D_MODEL

LANES = 128
CHUNK = 128
SEQ_BLOCK = 8
VMEM_LIMIT = 60 * 1024 * 1024


def _silu(x):
    hx = 0.5 * x
    return hx + hx * jnp.tanh(hx)


def _sigmoid(x):
    return 0.5 + 0.5 * jnp.tanh(0.5 * x)


def _dot(a, b):
    return jnp.dot(a, b, preferred_element_type=F32)


def _dot_nt(a, b):
    return lax.dot_general(a, b, (((1,), (1,)), ((), ())), preferred_element_type=F32)


def _dot_exact(a, b):
    return jnp.dot(a, b, preferred_element_type=F32, precision=lax.Precision.HIGHEST)


def _iota2(shape, dim):
    return lax.broadcasted_iota(jnp.int32, shape, dim)


def _div_pow2(x, d):
    assert d & (d - 1) == 0
    return jnp.right_shift(x, d.bit_length() - 1)


def _pad_rows(x, rows):
    if x.shape[0] == rows:
        return x
    return jnp.concatenate([x, jnp.zeros((rows - x.shape[0], x.shape[1]), x.dtype)], axis=0)


def _proj_kernel(x_ref, nw_ref, w_ref, wdt_ref, o_ref, dt_ref, xn_ref, *, sub):
    tm = x_ref.shape[0]

    @pl.when(pl.program_id(1) == 0)
    def _():
        def body(r, carry):
            rows = pl.ds(pl.multiple_of(r * sub, sub), sub)
            x = x_ref[rows, :]
            ms = jnp.mean(x * x, axis=-1, keepdims=True)
            xn = (x * lax.rsqrt(ms + EPS)) * nw_ref[...]
            xn_ref[rows, :] = xn.astype(BF16)
            return carry

        lax.fori_loop(0, tm // sub, body, 0)
        dt_ref[...] = _dot(xn_ref[...], wdt_ref[...])

    o_ref[...] = _dot(xn_ref[...], w_ref[...])


def _proj(x, norm_w, w_main, w_dt, *, tm, tn=1024):
    m = x.shape[0]
    assert m % tm == 0 and N_MAIN % tn == 0
    sub = 16 if tm % 128 else 128
    return pl.pallas_call(
        functools.partial(_proj_kernel, sub=sub),
        grid=(m // tm, N_MAIN // tn),
        in_specs=[
            pl.BlockSpec((tm, D_MODEL), lambda i, j: (i, 0)),
            pl.BlockSpec((1, D_MODEL), lambda i, j: (0, 0)),
            pl.BlockSpec((D_MODEL, tn), lambda i, j: (0, j)),
            pl.BlockSpec((D_MODEL, LANES), lambda i, j: (0, 0)),
        ],
        out_specs=[
            pl.BlockSpec((tm, tn), lambda i, j: (i, j)),
            pl.BlockSpec((tm, LANES), lambda i, j: (i, 0)),
        ],
        out_shape=[
            jax.ShapeDtypeStruct((m, N_MAIN), F32),
            jax.ShapeDtypeStruct((m, LANES), F32),
        ],
        scratch_shapes=[pltpu.VMEM((tm, D_MODEL), BF16)],
        compiler_params=pltpu.CompilerParams(
            dimension_semantics=("arbitrary", "arbitrary"),
            vmem_limit_bytes=VMEM_LIMIT),
        name="proj",
    )(x, norm_w, w_main, w_dt)


def _conv_silu(ext_ref, first, rows, cw_ref, cbias_ref, act_ref, row0):
    for cs in range(CONV_DIM // GROUP_W):
        cols = slice(cs * GROUP_W, (cs + 1) * GROUP_W)
        acc = cbias_ref[:, cols] + ext_ref[first:first + rows, cols] * cw_ref[0:1, cols]
        for k in range(1, CONV_K):
            acc = acc + ext_ref[first + k:first + k + rows, cols] * cw_ref[k:k + 1, cols]
        act_ref[row0:row0 + rows, cols] = _silu(acc)


def _dt_terms(dt_ref, dtb_ref, alog_ref, tri):
    x = dt_ref[...] + dtb_ref[...]
    dtv = jnp.maximum(x, 0.0) + jnp.log1p(jnp.exp(-jnp.abs(x)))
    da = dtv * (-jnp.exp(alog_ref[...]))
    acum = _dot_exact(tri, da)
    return dtv, da, acum


def _pool_group(ext_ref, first, rows, gi, u_cur):
    w = POOL_WINDOWS[gi]
    cols = slice(gi * POOL_GW, (gi + 1) * POOL_GW)
    s = u_cur
    for k in range(1, w):
        s = s + ext_ref[first - k:first - k + rows, cols]
    return s * (1.0 / w) - u_cur


def _gate_norm_store(y_g, zs, nw, y_ref, cols):
    y_g = y_g * _silu(zs)
    ms = jnp.mean(y_g * y_g, axis=-1, keepdims=True)
    y_ref[:, cols] = (y_g * lax.rsqrt(ms + EPS) * nw).astype(BF16)


def _seq_prompt_kernel(zs_ref, u_ref, xs_ref, bc_ref, dt_ref, h0_ref, ct0_ref, pt0_ref, e64_ref,
                       cw_ref, cbias_ref, dtb_ref, alog_ref, dskip_ref, nw_ref,
                       y_ref, pooled_ref, hfin_ref,
                       ht_ref, xext_ref, uext_ref, act_ref, *, with_y):
    L = xs_ref.shape[0]
    c = pl.program_id(1)
    nc = pl.num_programs(1)

    @pl.when(c == 0)
    def _():
        ht_ref[...] = h0_ref[...]
        xext_ref[0:8, :] = ct0_ref[...]
        uext_ref[0:POOL_MAX, :] = pt0_ref[...]

    xext_ref[8:8 + L, 0:INNER] = xs_ref[...]
    xext_ref[8:8 + L, INNER:CONV_DIM] = bc_ref[...]
    _conv_silu(xext_ref, 8 - (CONV_K - 1), L, cw_ref, cbias_ref, act_ref, 0)
    xext_ref[0:8, :] = xext_ref[L:L + 8, :]

    r_i = _iota2((L, L), 0)
    c_i = _iota2((L, L), 1)
    causal = c_i <= r_i
    tri = jnp.where(causal, 1.0, 0.0).astype(F32)
    dtv, da, acum = _dt_terms(dt_ref, dtb_ref, alog_ref, tri)
    if with_y:
        acum_t = acum.T
        lane = _iota2((L, LANES), 1)

    for g in range(GROUPS):
        cols = slice(g * GROUP_W, (g + 1) * GROUP_W)
        e_g = e64_ref[:, cols]
        acx = _dot_exact(acum, e_g)
        dtx = _dot_exact(dtv, e_g)
        xs_g = act_ref[:, cols]
        b_g = act_ref[:, INNER + g * STATE:INNER + (g + 1) * STATE]
        xdt = xs_g * dtx
        aend = acx[L - 1:L, :]
        xdtw = _pad_rows(xdt * jnp.exp(aend - acx), LANES).astype(BF16)
        b_t = _pad_rows(b_g, LANES).T
        h_old = ht_ref[:, cols]
        ht_ref[:, cols] = h_old * jnp.exp(aend) + _dot(b_t.astype(BF16), xdtw)

        if with_y:
            c_g = act_ref[:, INNER + GROUPS * STATE + g * STATE:
                          INNER + GROUPS * STATE + (g + 1) * STATE].astype(BF16)
            cb = _dot_nt(c_g, b_g.astype(BF16))
            yoff = _dot(c_g, h_old.astype(BF16)) * jnp.exp(acx)
            yds = []
            for q in range(GROUP_W // LANES):
                ms_ = []
                for h in (g * 8 + 2 * q, g * 8 + 2 * q + 1):
                    seg = acum[:, h:h + 1] - acum_t[h:h + 1, :]
                    dec = jnp.where(causal, jnp.exp(seg), 0.0)
                    ms_.append((cb * dec).astype(BF16))
                m_cat = jnp.concatenate(ms_, axis=1)
                xp = xdt[:, q * LANES:(q + 1) * LANES]
                r_m = jnp.concatenate([jnp.where(lane < HEADDIM, xp, 0.0),
                                       jnp.where(lane >= HEADDIM, xp, 0.0)], axis=0).astype(BF16)
                yds.append(_dot(m_cat, r_m))
            y_g = jnp.concatenate(yds, axis=1) + yoff + xs_g * dskip_ref[:, cols]
            _gate_norm_store(y_g, zs_ref[:, cols], nw_ref[:, cols], y_ref, cols)

    if with_y:
        uext_ref[POOL_MAX:POOL_MAX + L, :] = u_ref[...]
        for gi in range(len(POOL_WINDOWS)):
            cols = slice(gi * POOL_GW, (gi + 1) * POOL_GW)
            pooled_ref[:, cols] = _pool_group(uext_ref, POOL_MAX, L, gi, u_ref[:, cols]).astype(BF16)
        uext_ref[0:POOL_MAX, :] = uext_ref[L:L + POOL_MAX, :]

        @pl.when(c == nc - 1)
        def _():
            for k in range(INNER // LANES):
                hfin_ref[k * LANES:(k + 1) * LANES, :] = ht_ref[:, k * LANES:(k + 1) * LANES].T
    else:
        hfin_ref[...] = ht_ref[...]
        y_ref[...] = jnp.zeros(y_ref.shape, y_ref.dtype)
        pooled_ref[...] = jnp.zeros(pooled_ref.shape, pooled_ref.dtype)


def _param_specs(index_map):
    return [
        pl.BlockSpec((CONV_K, CONV_DIM), index_map),
        pl.BlockSpec((1, CONV_DIM), index_map),
        pl.BlockSpec((1, LANES), index_map),
        pl.BlockSpec((1, LANES), index_map),
        pl.BlockSpec((1, INNER), index_map),
        pl.BlockSpec((1, INNER), index_map),
    ]


def _seq_prompt(proj, dt, h0t, ct0, pt0, e64, params, *, batch, seq):
    nc = seq // CHUNK
    row = lambda b, c: b * nc + c
    const = lambda b, c: (0, 0)
    in_specs = [
        pl.BlockSpec((CHUNK, INNER), lambda b, c: (row(b, c), CB_ZS)),
        pl.BlockSpec((CHUNK, INNER), lambda b, c: (row(b, c), CB_U)),
        pl.BlockSpec((CHUNK, INNER), lambda b, c: (row(b, c), CB_XS)),
        pl.BlockSpec((CHUNK, 2 * GROUPS * STATE), lambda b, c: (row(b, c), CB_BC)),
        pl.BlockSpec((CHUNK, LANES), lambda b, c: (row(b, c), 0)),
        pl.BlockSpec((STATE, INNER), const),
        pl.BlockSpec((8, CONV_DIM), const),
        pl.BlockSpec((POOL_MAX, D_MODEL), const),
        pl.BlockSpec((LANES, INNER), const),
    ] + _param_specs(const)
    return pl.pallas_call(
        functools.partial(_seq_prompt_kernel, with_y=True),
        grid=(batch, nc),
        in_specs=in_specs,
        out_specs=[
            pl.BlockSpec((CHUNK, INNER), lambda b, c: (row(b, c), 0)),
            pl.BlockSpec((CHUNK, D_MODEL), lambda b, c: (row(b, c), 0)),
            pl.BlockSpec((None, INNER, STATE), lambda b, c: (b, 0, 0)),
        ],
        out_shape=[
            jax.ShapeDtypeStruct((batch * seq, INNER), BF16),
            jax.ShapeDtypeStruct((batch * seq, D_MODEL), BF16),
            jax.ShapeDtypeStruct((batch, INNER, STATE), F32),
        ],
        scratch_shapes=[
            pltpu.VMEM((STATE, INNER), F32),
            pltpu.VMEM((8 + CHUNK, CONV_DIM), F32),
            pltpu.VMEM((POOL_MAX + CHUNK, D_MODEL), F32),
            pltpu.VMEM((CHUNK, CONV_DIM), F32),
        ],
        compiler_params=pltpu.CompilerParams(
            dimension_semantics=("arbitrary", "arbitrary"),
            vmem_limit_bytes=VMEM_LIMIT),
        name="seq_prompt",
    )(proj, proj, proj, proj, dt, h0t, ct0, pt0, e64, *params)


def _meta_state(proj, dt, e64, params, *, row_block):
    const = lambda b, c: (0, 0)
    zeros = functools.partial(jnp.zeros, dtype=F32)
    in_specs = [
        pl.BlockSpec((N_META, INNER), lambda b, c: (row_block, CB_ZS)),
        pl.BlockSpec((N_META, INNER), lambda b, c: (row_block, CB_U)),
        pl.BlockSpec((N_META, INNER), lambda b, c: (row_block, CB_XS)),
        pl.BlockSpec((N_META, 2 * GROUPS * STATE), lambda b, c: (row_block, CB_BC)),
        pl.BlockSpec((N_META, LANES), lambda b, c: (row_block, 0)),
        pl.BlockSpec((STATE, INNER), const),
        pl.BlockSpec((8, CONV_DIM), const),
        pl.BlockSpec((POOL_MAX, D_MODEL), const),
        pl.BlockSpec((LANES, INNER), const),
    ] + _param_specs(const)
    outs = pl.pallas_call(
        functools.partial(_seq_prompt_kernel, with_y=False),
        grid=(1, 1),
        in_specs=in_specs,
        out_specs=[
            pl.BlockSpec((N_META, INNER), const),
            pl.BlockSpec((N_META, D_MODEL), const),
            pl.BlockSpec((STATE, INNER), const),
        ],
        out_shape=[
            jax.ShapeDtypeStruct((N_META, INNER), BF16),
            jax.ShapeDtypeStruct((N_META, D_MODEL), BF16),
            jax.ShapeDtypeStruct((STATE, INNER), F32),
        ],
        scratch_shapes=[
            pltpu.VMEM((STATE, INNER), F32),
            pltpu.VMEM((8 + N_META, CONV_DIM), F32),
            pltpu.VMEM((POOL_MAX + N_META, D_MODEL), F32),
            pltpu.VMEM((N_META, CONV_DIM), F32),
        ],
        compiler_params=pltpu.CompilerParams(
            dimension_semantics=("arbitrary", "arbitrary"),
            vmem_limit_bytes=VMEM_LIMIT),
        name="meta_state",
    )(proj, proj, proj, proj, dt, zeros((STATE, INNER)), zeros((8, CONV_DIM)),
      zeros((POOL_MAX, D_MODEL)), e64, *params)
    return outs[2]


def _seq_sample_kernel(zs_ref, u_ref, xs_ref, bc_ref, dt_ref, chist_ref, phist_ref, h0_ref, e64_ref,
                       cw_ref, cbias_ref, dtb_ref, alog_ref, dskip_ref, nw_ref,
                       y_ref, pooled_ref, hnew_ref,
                       xext_ref, uext_ref, act_ref, yoff_ref, dv_ref, s_ref):
    R = xs_ref.shape[0]
    T = R // SEQ_BLOCK
    XS = 8 + T
    US = POOL_MAX + T
    HIST = CONV_K - 1
    half = R

    for i in range(SEQ_BLOCK):
        xext_ref[i * XS + 8 - HIST:i * XS + 8, :] = chist_ref[i]
        xext_ref[i * XS + 8:i * XS + 8 + T, 0:INNER] = xs_ref[i * T:(i + 1) * T, :]
        xext_ref[i * XS + 8:i * XS + 8 + T, INNER:CONV_DIM] = bc_ref[i * T:(i + 1) * T, :]
    for i in range(SEQ_BLOCK):
        _conv_silu(xext_ref, i * XS + 8 - HIST, T, cw_ref, cbias_ref, act_ref, i * T)

    r_i = _iota2((R, R), 0)
    c_i = _iota2((R, R), 1)
    same = _div_pow2(r_i, T) == _div_pow2(c_i, T)
    tri = jnp.where(same & (c_i <= r_i), 1.0, 0.0).astype(F32)
    ones_bd = jnp.where(same, 1.0, 0.0).astype(F32)
    dtv, da, acum = _dt_terms(dt_ref, dtb_ref, alog_ref, tri)
    a_end = _dot_exact(ones_bd, da)

    acum_t2 = jnp.concatenate([acum, acum], axis=0).T
    aend_t = _pad_rows(a_end, LANES).T

    r_s = _iota2((LANES, SEQ_BLOCK * STATE), 0)
    c_s = _iota2((LANES, SEQ_BLOCK * STATE), 1)
    sel = jnp.where(r_s == _div_pow2(c_s, STATE) * T, 1.0, 0.0).astype(F32)
    dv_ref[...] = jnp.exp(_dot_exact(aend_t, sel))

    lane2 = _iota2((R, 2 * R), 1)
    r_2 = _iota2((R, 2 * R), 0)
    c_2 = jnp.bitwise_and(lane2, R - 1)
    mask2 = (_div_pow2(r_2, T) == _div_pow2(c_2, T)) & (c_2 <= r_2)
    first_half = lane2 < half
    lane = _iota2((R, LANES), 1)
    r_b = _iota2((R, SEQ_BLOCK * STATE), 0)
    c_b = _iota2((R, SEQ_BLOCK * STATE), 1)
    bd_sel = _div_pow2(r_b, T) == _div_pow2(c_b, STATE)

    for g in range(GROUPS):
        cols = slice(g * GROUP_W, (g + 1) * GROUP_W)
        e_g = e64_ref[:, cols]
        acx = _dot_exact(acum, e_g)
        dtx = _dot_exact(dtv, e_g)
        aendx = _dot_exact(a_end, e_g)
        xs_g = act_ref[:, cols]
        b_g = act_ref[:, INNER + g * STATE:INNER + (g + 1) * STATE]
        c_g = act_ref[:, INNER + GROUPS * STATE + g * STATE:
                      INNER + GROUPS * STATE + (g + 1) * STATE].astype(BF16)
        xdt = xs_g * dtx
        xdtw = xdt * jnp.exp(aendx - acx)

        for i in range(SEQ_BLOCK):
            h_i = h0_ref[i, g * GROUP_W:(g + 1) * GROUP_W, :].astype(BF16)
            yo = _dot_nt(c_g, h_i)
            yoff_ref[i * T:(i + 1) * T, cols] = yo[i * T:(i + 1) * T, :]
        xt = _pad_rows(xdtw, LANES).T
        b_big = jnp.where(bd_sel, jnp.concatenate([b_g] * SEQ_BLOCK, axis=1), 0.0)
        b_big = _pad_rows(b_big, LANES)
        s_ref[...] = _dot(xt.astype(BF16), b_big.astype(BF16))
        for i in range(SEQ_BLOCK):
            for hh in range(GROUP_W // HEADDIM):
                h = g * (GROUP_W // HEADDIM) + hh
                rows = slice(h * HEADDIM, (h + 1) * HEADDIM)
                hnew_ref[i, rows, :] = (
                    h0_ref[i, rows, :] * dv_ref[h:h + 1, i * STATE:(i + 1) * STATE]
                    + s_ref[hh * HEADDIM:(hh + 1) * HEADDIM, i * STATE:(i + 1) * STATE])

        cb2 = _dot_nt(c_g, jnp.concatenate([b_g, b_g], axis=0).astype(BF16))
        yds = []
        for q in range(GROUP_W // LANES):
            ha = g * 8 + 2 * q
            colc = jnp.where(first_half, acum[:, ha:ha + 1], acum[:, ha + 1:ha + 2])
            rowc = jnp.where(first_half[0:1, :], acum_t2[ha:ha + 1, :], acum_t2[ha + 1:ha + 2, :])
            dec = jnp.where(mask2, jnp.exp(colc - rowc), 0.0)
            m_p = (cb2 * dec).astype(BF16)
            xp = xdt[:, q * LANES:(q + 1) * LANES]
            r_m = jnp.concatenate([jnp.where(lane < HEADDIM, xp, 0.0),
                                   jnp.where(lane >= HEADDIM, xp, 0.0)], axis=0).astype(BF16)
            yds.append(_dot(m_p, r_m))
        y_g = (jnp.concatenate(yds, axis=1) + yoff_ref[:, cols] * jnp.exp(acx)
               + xs_g * dskip_ref[:, cols])
        _gate_norm_store(y_g, zs_ref[:, cols], nw_ref[:, cols], y_ref, cols)

    for i in range(SEQ_BLOCK):
        uext_ref[i * US + 1:i * US + POOL_MAX, :] = phist_ref[i]
        uext_ref[i * US + POOL_MAX:(i + 1) * US, :] = u_ref[i * T:(i + 1) * T, :]
    for i in range(SEQ_BLOCK):
        for gi in range(len(POOL_WINDOWS)):
            cols = slice(gi * POOL_GW, (gi + 1) * POOL_GW)
            pooled_ref[i * T:(i + 1) * T, cols] = _pool_group(
                uext_ref, i * US + POOL_MAX, T, gi, u_ref[i * T:(i + 1) * T, cols]).astype(BF16)


def _seq_sample(proj, dt, conv_hist, pool_hist, h0, e64, params, *, nseq, t_new):
    rb = SEQ_BLOCK * t_new
    assert 2 * rb == LANES
    row = lambda s: s
    const = lambda s: (0, 0)
    in_specs = [
        pl.BlockSpec((rb, INNER), lambda s: (s, CB_ZS)),
        pl.BlockSpec((rb, INNER), lambda s: (s, CB_U)),
        pl.BlockSpec((rb, INNER), lambda s: (s, CB_XS)),
        pl.BlockSpec((rb, 2 * GROUPS * STATE), lambda s: (s, CB_BC)),
        pl.BlockSpec((rb, LANES), lambda s: (s, 0)),
        pl.BlockSpec((SEQ_BLOCK, CONV_K - 1, CONV_DIM), lambda s: (s, 0, 0)),
        pl.BlockSpec((SEQ_BLOCK, POOL_MAX - 1, D_MODEL), lambda s: (s, 0, 0)),
        pl.BlockSpec((SEQ_BLOCK, INNER, STATE), lambda s: (s, 0, 0)),
        pl.BlockSpec((LANES, INNER), const),
    ] + _param_specs(const)
    return pl.pallas_call(
        _seq_sample_kernel,
        grid=(nseq // SEQ_BLOCK,),
        in_specs=in_specs,
        out_specs=[
            pl.BlockSpec((rb, INNER), lambda s: (s, 0)),
            pl.BlockSpec((rb, D_MODEL), lambda s: (s, 0)),
            pl.BlockSpec((SEQ_BLOCK, INNER, STATE), lambda s: (s, 0, 0)),
        ],
        out_shape=[
            jax.ShapeDtypeStruct((nseq * t_new, INNER), BF16),
            jax.ShapeDtypeStruct((nseq * t_new, D_MODEL), BF16),
            jax.ShapeDtypeStruct((nseq, INNER, STATE), F32),
        ],
        scratch_shapes=[
            pltpu.VMEM((SEQ_BLOCK * (8 + t_new), CONV_DIM), F32),
            pltpu.VMEM((SEQ_BLOCK * (POOL_MAX + t_new), D_MODEL), F32),
            pltpu.VMEM((rb, CONV_DIM), F32),
            pltpu.VMEM((rb, INNER), F32),
            pltpu.VMEM((LANES, SEQ_BLOCK * STATE), F32),
            pltpu.VMEM((GROUP_W, SEQ_BLOCK * STATE), F32),
        ],
        compiler_params=pltpu.CompilerParams(
            dimension_semantics=("arbitrary",),
            vmem_limit_bytes=VMEM_LIMIT),
        name="seq_sample",
    )(proj, proj, proj, proj, dt, conv_hist, pool_hist, h0, e64, *params)


def _merge_kernel(y_ref, pooled_ref, zp_ref, g1_ref, g2_ref, wmix_ref, bmix_ref, pscale_ref,
                  wps_ref, wpp_ref, o_ref, pout_ref):
    @pl.when(pl.program_id(1) == 0)
    def _():
        for g in range(len(POOL_WINDOWS)):
            cols = slice(g * POOL_GW, (g + 1) * POOL_GW)
            mixed = _dot(pooled_ref[:, cols], wmix_ref[g]) + bmix_ref[:, cols]
            pout_ref[:, cols] = (mixed * pscale_ref[:, cols] * _silu(zp_ref[:, cols])).astype(BF16)

    bs = _dot(y_ref[...], wps_ref[...])
    bp = _dot(pout_ref[...], wpp_ref[...])
    o_ref[...] = (_sigmoid(g1_ref[...]) * bs + _sigmoid(g2_ref[...]) * bp).astype(BF16)


def _merge(y, pooled, proj, wmix, bmix, pscale, wps, wpp, *, m, tm=512, tn=1024):
    nj = D_MODEL // tn
    const2 = lambda i, j: (0, 0)
    return pl.pallas_call(
        _merge_kernel,
        grid=(m // tm, nj),
        in_specs=[
            pl.BlockSpec((tm, INNER), lambda i, j: (i, 0)),
            pl.BlockSpec((tm, D_MODEL), lambda i, j: (i, 0)),
            pl.BlockSpec((tm, D_MODEL), lambda i, j: (i, CB_ZP)),
            pl.BlockSpec((tm, tn), lambda i, j: (i, CB_G1 * nj + j)),
            pl.BlockSpec((tm, tn), lambda i, j: (i, CB_G2 * nj + j)),
            pl.BlockSpec((len(POOL_WINDOWS), POOL_GW, POOL_GW), lambda i, j: (0, 0, 0)),
            pl.BlockSpec((1, D_MODEL), const2),
            pl.BlockSpec((1, D_MODEL), const2),
            pl.BlockSpec((INNER, tn), lambda i, j: (0, j)),
            pl.BlockSpec((D_MODEL, tn), lambda i, j: (0, j)),
        ],
        out_specs=pl.BlockSpec((tm, tn), lambda i, j: (i, j)),
        out_shape=jax.ShapeDtypeStruct((m, D_MODEL), BF16),
        scratch_shapes=[pltpu.VMEM((tm, D_MODEL), BF16)],
        compiler_params=pltpu.CompilerParams(
            dimension_semantics=("arbitrary", "arbitrary"),
            vmem_limit_bytes=VMEM_LIMIT),
        name="merge",
    )(y, pooled, proj, proj, proj, wmix, bmix, pscale, wps, wpp)


def _out_kernel(m_ref, h_ref, wout_ref, fw_ref, o_ref):
    hn = h_ref[...] + _dot(m_ref[...], wout_ref[...])
    ms = jnp.mean(hn * hn, axis=-1, keepdims=True)
    o_ref[...] = (hn * lax.rsqrt(ms + EPS)) * fw_ref[...]


def _out(merged, h, wout, fw, *, m, tm=512):
    return pl.pallas_call(
        _out_kernel,
        grid=(m // tm,),
        in_specs=[
            pl.BlockSpec((tm, D_MODEL), lambda i: (i, 0)),
            pl.BlockSpec((tm, D_MODEL), lambda i: (i, 0)),
            pl.BlockSpec((D_MODEL, D_MODEL), lambda i: (0, 0)),
            pl.BlockSpec((1, D_MODEL), lambda i: (0, 0)),
        ],
        out_specs=pl.BlockSpec((tm, D_MODEL), lambda i: (i, 0)),
        out_shape=jax.ShapeDtypeStruct((m, D_MODEL), F32),
        compiler_params=pltpu.CompilerParams(
            dimension_semantics=("arbitrary",),
            vmem_limit_bytes=VMEM_LIMIT),
        name="out",
    )(merged, h, wout, fw)


def kernel(x_prompt, x_sample, state_conv, state_ssm, state_pool, meta_tokens, norm_w, w_in,
           conv_w, conv_b, dt_bias, a_log, d_skip, ssd_norm_w, w_proj_ssd, pool_mix_w,
           pool_mix_b, pool_scale, w_proj_pool, w_out, final_norm_w):
    batch, seq, _ = x_prompt.shape
    nseq, t_new, _ = x_sample.shape
    assert norm_w.shape[0] == 1, "single layer"
    off_xbc, off_dt = INNER, INNER + CONV_DIM
    off_zp = off_dt + HEADS

    w = w_in[0]
    w_main = jnp.concatenate([w[:, :off_xbc], w[:, off_zp:], w[:, off_xbc:off_dt]], axis=1).astype(BF16)
    w_dt = jnp.pad(w[:, off_dt:off_zp], ((0, 0), (0, LANES - HEADS))).astype(BF16)
    wps = w_proj_ssd[0].astype(BF16)
    wpp = w_proj_pool[0].astype(BF16)
    wout = w_out[0].astype(BF16)
    wmix = pool_mix_w[0].astype(BF16)
    lane_pad = lambda v: jnp.pad(v.reshape(1, HEADS), ((0, 0), (0, LANES - HEADS)))
    params = (conv_w[0], conv_b[0].reshape(1, CONV_DIM), lane_pad(dt_bias[0]), lane_pad(a_log[0]),
              jnp.repeat(d_skip[0], HEADDIM).reshape(1, INNER), ssd_norm_w[0].reshape(1, INNER))
    e64 = (jnp.arange(LANES)[:, None] == (jnp.arange(INNER)[None, :] // HEADDIM)).astype(F32)
    nw = norm_w[0].reshape(1, D_MODEL)
    fw = final_norm_w.reshape(1, D_MODEL)
    bmix = pool_mix_b[0].reshape(1, D_MODEL)
    pscale = pool_scale[0].reshape(1, D_MODEL)

    xp = x_prompt.reshape(batch * seq, D_MODEL)
    m_s = nseq * t_new
    x_sm = jnp.concatenate([x_sample.reshape(m_s, D_MODEL), meta_tokens], axis=0)

    proj_p, dt_p = _proj(xp, nw, w_main, w_dt, tm=1024)
    proj_s, dt_s = _proj(x_sm, nw, w_main, w_dt, tm=m_s + N_META)

    h0t = _meta_state(proj_s, dt_s, e64, params, row_block=m_s // N_META)
    ct0 = proj_s[m_s + N_META - 8:m_s + N_META, OFF_XBC:]
    pt0 = proj_s[m_s:m_s + N_META, OFF_U:OFF_U + D_MODEL]

    y_p, pooled_p, ssm_p = _seq_prompt(proj_p, dt_p, h0t, ct0, pt0, e64, params, batch=batch, seq=seq)
    y_s, pooled_s, ssm_s = _seq_sample(
        proj_s, dt_s, state_conv[0], state_pool[0],
        state_ssm[0].reshape(nseq, INNER, STATE), e64, params, nseq=nseq, t_new=t_new)

    merged_p = _merge(y_p, pooled_p, proj_p, wmix, bmix, pscale, wps, wpp, m=batch * seq)
    merged_s = _merge(y_s, pooled_s, proj_s, wmix, bmix, pscale, wps, wpp, m=m_s)
    out_p = _out(merged_p, xp, wout, fw, m=batch * seq)
    out_s = _out(merged_s, x_sm, wout, fw, m=m_s)

    proj_p3 = proj_p.reshape(batch, seq, N_MAIN)
    proj_s3 = proj_s[:m_s].reshape(nseq, t_new, N_MAIN)
    new_conv_p = proj_p3[:, seq - (CONV_K - 1):, OFF_XBC:][None]
    new_pool_p = proj_p3[:, seq - (POOL_MAX - 1):, OFF_U:OFF_U + D_MODEL][None]
    new_conv_s = proj_s3[:, t_new - (CONV_K - 1):, OFF_XBC:][None]
    new_pool_s = jnp.concatenate(
        [state_pool[0][:, t_new:], proj_s3[:, :, OFF_U:OFF_U + D_MODEL]], axis=1)[None]
    return (out_p.reshape(batch, seq, D_MODEL),
            out_s.reshape(nseq, t_new, D_MODEL),
            new_conv_p.astype(state_conv.dtype),
            ssm_p.reshape(1, batch, HEADS, HEADDIM, STATE).astype(state_ssm.dtype),
            new_pool_p.astype(state_pool.dtype),
            new_conv_s.astype(state_conv.dtype),
            ssm_s.reshape(1, nseq, HEADS, HEADDIM, STATE).astype(state_ssm.dtype),
            new_pool_s.astype(state_pool.dtype))
```

```python
import functools

import jax
import jax.numpy as jnp
from jax import lax
from jax.experimental import pallas as pl
from jax.experimental.pallas import tpu as pltpu

F32 = jnp.float32
BF16 = jnp.bfloat16

D_MODEL = 2048
N_META = 16
EPS = 1e-6
HEADS = 32
HEADDIM = 64
GROUPS = 4
STATE = 128
CONV_K = 4
INNER = HEADS * HEADDIM
GROUP_W = INNER // GROUPS
CONV_DIM = INNER + 2 * GROUPS * STATE
POOL_WINDOWS = (2, 4, 8, 16)
POOL_MAX = 16
POOL_GW = D_MODEL // len(POOL_WINDOWS)

IN_XBC = INNER
IN_DT = IN_XBC + CONV_DIM
IN_ZP = IN_DT + HEADS
N_MAIN = 5 * D_MODEL + CONV_DIM
CB_ZS, CB_ZP, CB_U, CB_G1, CB_G2, CB_XS = 0, 1, 2, 3, 4, 5
CB_BC = 12

LANES = 128
CHUNK = 128
SEQ_BLOCK = 8
VMEM_LIMIT = 60 * 1024 * 1024
PREP_TN = 1024
PU_BASE = 2 * POOL_MAX


def _silu(x):
    hx = 0.5 * x
    return hx + hx * jnp.tanh(hx)


def _sigmoid(x):
    return 0.5 + 0.5 * jnp.tanh(0.5 * x)


def _dot(a, b):
    return jnp.dot(a, b, preferred_element_type=F32)


def _dot_nt(a, b):
    return lax.dot_general(a, b, (((1,), (1,)), ((), ())), preferred_element_type=F32)


def _split3(a):
    a1 = a.astype(BF16)
    r1 = a - a1.astype(F32)
    a2 = r1.astype(BF16)
    a3 = (r1 - a2.astype(F32)).astype(BF16)
    return a1, a2, a3


def _dot_exact(parts, b):
    out = _dot(parts[0], b)
    for p in parts[1:]:
        out = out + _dot(p, b)
    return out


def _dot_exact_l(a01, x):
    parts = _split3(x)
    out = _dot(a01, parts[0])
    for p in parts[1:]:
        out = out + _dot(a01, p)
    return out


def _iota2(shape, dim):
    return lax.broadcasted_iota(jnp.int32, shape, dim)


def _div_pow2(x, d):
    assert d & (d - 1) == 0
    return jnp.right_shift(x, d.bit_length() - 1)


def _pad_rows(x, rows):
    if x.shape[0] == rows:
        return x
    return jnp.concatenate([x, jnp.zeros((rows - x.shape[0], x.shape[1]), x.dtype)], axis=0)


def _prep_kernel(a_ref, b_ref, d_ref, o_ref, dt_ref, *, sub):
    c = pl.program_id(0)
    n_sub = D_MODEL // sub
    shift = IN_ZP - IN_DT

    @pl.when(c == 0)
    def _():
        lane = _iota2((D_MODEL, LANES), 1)
        dt_ref[...] = jnp.where(lane < HEADS, d_ref[...], 0.0).astype(BF16)

    shifted = jnp.logical_and(c >= 2, c < 10)

    @pl.when(shifted)
    def _():
        def body(r, carry):
            rows = pl.ds(pl.multiple_of(r * sub, sub), sub)
            cat = jnp.concatenate([a_ref[rows, :], b_ref[rows, :]], axis=1)
            o_ref[rows, :] = cat[:, shift:shift + PREP_TN].astype(BF16)
            return carry

        lax.fori_loop(0, n_sub, body, 0)

    @pl.when(jnp.logical_not(shifted))
    def _():
        def body(r, carry):
            rows = pl.ds(pl.multiple_of(r * sub, sub), sub)
            o_ref[rows, :] = a_ref[rows, :].astype(BF16)
            return carry

        lax.fori_loop(0, n_sub, body, 0)


def _prep_weights(w_in):
    nb = IN_DT // PREP_TN
    per = PREP_TN // LANES

    def a_map(c):
        return (0, 0, jnp.where(c < 2, c, jnp.where(c < 10, c + nb - 2, c - 8)))

    def b_map(c):
        inside = jnp.logical_and(c >= 2, c < 10)
        return (0, 0, jnp.where(inside, nb * per + (c - 1) * per, 0))

    return pl.pallas_call(
        functools.partial(_prep_kernel, sub=256),
        grid=(N_MAIN // PREP_TN,),
        in_specs=[
            pl.BlockSpec((None, D_MODEL, PREP_TN), a_map),
            pl.BlockSpec((None, D_MODEL, LANES), b_map),
            pl.BlockSpec((None, D_MODEL, LANES), lambda c: (0, 0, nb * per)),
        ],
        out_specs=[
            pl.BlockSpec((D_MODEL, PREP_TN), lambda c: (0, c)),
            pl.BlockSpec((D_MODEL, LANES), lambda c: (0, 0)),
        ],
        out_shape=[
            jax.ShapeDtypeStruct((D_MODEL, N_MAIN), BF16),
            jax.ShapeDtypeStruct((D_MODEL, LANES), BF16),
        ],
        compiler_params=pltpu.CompilerParams(
            dimension_semantics=("arbitrary",), vmem_limit_bytes=VMEM_LIMIT),
        name="prep",
    )(w_in, w_in, w_in)


def _proj_kernel(x_ref, nw_ref, w_ref, wdt_ref, o_ref, dt_ref, xn_ref, *, sub):
    tm = x_ref.shape[0]

    @pl.when(pl.program_id(1) == 0)
    def _():
        def body(r, carry):
            rows = pl.ds(pl.multiple_of(r * sub, sub), sub)
            x = x_ref[rows, :]
            ms = jnp.mean(x * x, axis=-1, keepdims=True)
            xn = (x * lax.rsqrt(ms + EPS)) * nw_ref[...]
            xn_ref[rows, :] = xn.astype(BF16)
            return carry

        lax.fori_loop(0, tm // sub, body, 0)
        dt_ref[...] = _dot(xn_ref[...], wdt_ref[...])

    o_ref[...] = _dot(xn_ref[...], w_ref[...])


def _proj(x, norm_w, w_main, w_dt, *, tm, tn=1024):
    m = x.shape[0]
    assert m % tm == 0 and N_MAIN % tn == 0
    sub = 16 if tm % 128 else 128
    return pl.pallas_call(
        functools.partial(_proj_kernel, sub=sub),
        grid=(m // tm, N_MAIN // tn),
        in_specs=[
            pl.BlockSpec((tm, D_MODEL), lambda i, j: (i, 0)),
            pl.BlockSpec((1, D_MODEL), lambda i, j: (0, 0)),
            pl.BlockSpec((D_MODEL, tn), lambda i, j: (0, j)),
            pl.BlockSpec((D_MODEL, LANES), lambda i, j: (0, 0)),
        ],
        out_specs=[
            pl.BlockSpec((tm, tn), lambda i, j: (i, j)),
            pl.BlockSpec((tm, LANES), lambda i, j: (i, 0)),
        ],
        out_shape=[
            jax.ShapeDtypeStruct((m, N_MAIN), F32),
            jax.ShapeDtypeStruct((m, LANES), F32),
        ],
        scratch_shapes=[pltpu.VMEM((tm, D_MODEL), BF16)],
        compiler_params=pltpu.CompilerParams(
            dimension_semantics=("arbitrary", "arbitrary"),
            vmem_limit_bytes=VMEM_LIMIT),
        name="proj",
    )(x, norm_w, w_main, w_dt)


def _conv_silu(ext_ref, first, rows, cw_ref, cbias_ref, act_ref, row0):
    for cs in range(CONV_DIM // GROUP_W):
        cols = slice(cs * GROUP_W, (cs + 1) * GROUP_W)
        acc = cbias_ref[:, cols] + ext_ref[first:first + rows, cols] * cw_ref[0:1, cols]
        for k in range(1, CONV_K):
            acc = acc + ext_ref[first + k:first + k + rows, cols] * cw_ref[k:k + 1, cols]
        act_ref[row0:row0 + rows, cols] = _silu(acc)


def _dt_terms(dt_ref, dtb_ref, alog_ref):
    x = dt_ref[...] + dtb_ref[...]
    dtv = jnp.maximum(x, 0.0) + jnp.log1p(jnp.exp(-jnp.abs(x)))
    da = dtv * (-jnp.exp(alog_ref[...]))
    return dtv, da


def _gate_norm_store(y_g, zs, nw, y_ref, cols):
    y_g = y_g * _silu(zs)
    ms = jnp.mean(y_g * y_g, axis=-1, keepdims=True)
    y_ref[:, cols] = (y_g * lax.rsqrt(ms + EPS) * nw).astype(BF16)


def _pool_prompt(uext_ref, l1_ref, l2_ref, l3_ref, pooled_ref, rows):
    base, end = PU_BASE, PU_BASE + rows
    for gi, w in enumerate(POOL_WINDOWS):
        cols = slice(gi * POOL_GW, (gi + 1) * POOL_GW)
        levels = w.bit_length() - 1
        lo = base - 8 * (levels - 1)
        if levels == 1:
            s = uext_ref[base:end, cols] + uext_ref[base - 1:end - 1, cols]
        else:
            l1_ref[lo:end, :] = uext_ref[lo:end, cols] + uext_ref[lo - 1:end - 1, cols]
            if levels == 2:
                s = l1_ref[base:end, :] + l1_ref[base - 2:end - 2, :]
            else:
                lo2 = lo + 8
                l2_ref[lo2:end, :] = l1_ref[lo2:end, :] + l1_ref[lo2 - 2:end - 2, :]
                if levels == 3:
                    s = l2_ref[base:end, :] + l2_ref[base - 4:end - 4, :]
                else:
                    lo3 = lo2 + 8
                    l3_ref[lo3:end, :] = l2_ref[lo3:end, :] + l2_ref[lo3 - 4:end - 4, :]
                    s = l3_ref[base:end, :] + l3_ref[base - 8:end - 8, :]
        pooled_ref[:, cols] = (s * (1.0 / w) - uext_ref[base:end, cols]).astype(BF16)


def _seq_prompt_kernel(zs_ref, u_ref, xs_ref, bc_ref, dt_ref, h0_ref, ct0_ref, pt0_ref, e64_ref,
                       cw_ref, cbias_ref, dtb_ref, alog_ref, dskip_ref, nw_ref,
                       y_ref, pooled_ref, hfin_ref, ctail_ref, ptail_ref,
                       ht_ref, xext_ref, uext_ref, act_ref, l1_ref, l2_ref, l3_ref, *, with_y):
    L = xs_ref.shape[0]
    c = pl.program_id(1)
    nc = pl.num_programs(1)

    @pl.when(c == 0)
    def _():
        ht_ref[...] = h0_ref[...]
        xext_ref[0:8, :] = ct0_ref[...]
        uext_ref[0:POOL_MAX, :] = jnp.zeros((POOL_MAX, D_MODEL), F32)
        uext_ref[POOL_MAX:PU_BASE, :] = pt0_ref[...]

    xext_ref[8:8 + L, 0:INNER] = xs_ref[...]
    xext_ref[8:8 + L, INNER:CONV_DIM] = bc_ref[...]
    _conv_silu(xext_ref, 8 - (CONV_K - 1), L, cw_ref, cbias_ref, act_ref, 0)
    xext_ref[0:8, :] = xext_ref[L:L + 8, :]

    r_i = _iota2((L, L), 0)
    c_i = _iota2((L, L), 1)
    causal = c_i <= r_i
    tri = jnp.where(causal, 1.0, 0.0).astype(BF16)
    dtv, da = _dt_terms(dt_ref, dtb_ref, alog_ref)
    acum = _dot_exact_l(tri, da)
    stacked = _split3(jnp.concatenate([acum, dtv], axis=0))
    if with_y:
        acum_t = acum.T
        lane = _iota2((L, LANES), 1)

    for g in range(GROUPS):
        cols = slice(g * GROUP_W, (g + 1) * GROUP_W)
        ex = _dot_exact(stacked, e64_ref[:, cols])
        acx = ex[0:L, :]
        dtx = ex[L:2 * L, :]
        xs_g = act_ref[:, cols]
        b_g = act_ref[:, INNER + g * STATE:INNER + (g + 1) * STATE]
        xdt = xs_g * dtx
        aend = acx[L - 1:L, :]
        xdtw = _pad_rows(xdt * jnp.exp(aend - acx), LANES).astype(BF16)
        b_t = _pad_rows(b_g, LANES).T
        h_old = ht_ref[:, cols]
        ht_ref[:, cols] = h_old * jnp.exp(aend) + _dot(b_t.astype(BF16), xdtw)

        if with_y:
            c_g = act_ref[:, INNER + GROUPS * STATE + g * STATE:
                          INNER + GROUPS * STATE + (g + 1) * STATE].astype(BF16)
            cb = _dot_nt(c_g, b_g.astype(BF16))
            yoff = _dot(c_g, h_old.astype(BF16)) * jnp.exp(acx)
            yds = []
            for q in range(GROUP_W // LANES):
                ms_ = []
                for h in (g * 8 + 2 * q, g * 8 + 2 * q + 1):
                    seg = acum[:, h:h + 1] - acum_t[h:h + 1, :]
                    dec = jnp.where(causal, jnp.exp(seg), 0.0)
                    ms_.append((cb * dec).astype(BF16))
                m_cat = jnp.concatenate(ms_, axis=1)
                xp = xdt[:, q * LANES:(q + 1) * LANES]
                r_m = jnp.concatenate([jnp.where(lane < HEADDIM, xp, 0.0),
                                       jnp.where(lane >= HEADDIM, xp, 0.0)], axis=0).astype(BF16)
                yds.append(_dot(m_cat, r_m))
            y_g = jnp.concatenate(yds, axis=1) + yoff + xs_g * dskip_ref[:, cols]
            _gate_norm_store(y_g, zs_ref[:, cols], nw_ref[:, cols], y_ref, cols)

    if with_y:
        uext_ref[PU_BASE:PU_BASE + L, :] = u_ref[...]
        _pool_prompt(uext_ref, l1_ref, l2_ref, l3_ref, pooled_ref, L)
        uext_ref[POOL_MAX:PU_BASE, :] = uext_ref[L + POOL_MAX:L + PU_BASE, :]

        @pl.when(c == nc - 1)
        def _():
            for k in range(INNER // LANES):
                hfin_ref[k * LANES:(k + 1) * LANES, :] = ht_ref[:, k * LANES:(k + 1) * LANES].T
            ctail_ref[...] = xext_ref[0:8, :]
            ptail_ref[...] = uext_ref[POOL_MAX:PU_BASE, :]
    else:
        hfin_ref[...] = ht_ref[...]
        ctail_ref[...] = xext_ref[0:8, :]
        ptail_ref[...] = u_ref[...]
        y_ref[...] = jnp.zeros(y_ref.shape, y_ref.dtype)
        pooled_ref[...] = jnp.zeros(pooled_ref.shape, pooled_ref.dtype)


def _param_specs(index_map):
    return [
        pl.BlockSpec((CONV_K, CONV_DIM), index_map),
        pl.BlockSpec((1, CONV_DIM), index_map),
        pl.BlockSpec((1, LANES), index_map),
        pl.BlockSpec((1, LANES), index_map),
        pl.BlockSpec((1, INNER), index_map),
        pl.BlockSpec((1, INNER), index_map),
    ]


def _seq_scratch(rows):
    return [
        pltpu.VMEM((STATE, INNER), F32),
        pltpu.VMEM((8 + rows, CONV_DIM), F32),
        pltpu.VMEM((PU_BASE + rows, D_MODEL), F32),
        pltpu.VMEM((rows, CONV_DIM), F32),
        pltpu.VMEM((PU_BASE + rows, POOL_GW), F32),
        pltpu.VMEM((PU_BASE + rows, POOL_GW), F32),
        pltpu.VMEM((PU_BASE + rows, POOL_GW), F32),
    ]


def _seq_prompt(proj, dt, h0t, ct0, pt0, e64, params, *, batch, seq):
    nc = seq // CHUNK
    row = lambda b, c: b * nc + c
    const = lambda b, c: (0, 0)
    in_specs = [
        pl.BlockSpec((CHUNK, INNER), lambda b, c: (row(b, c), CB_ZS)),
        pl.BlockSpec((CHUNK, INNER), lambda b, c: (row(b, c), CB_U)),
        pl.BlockSpec((CHUNK, INNER), lambda b, c: (row(b, c), CB_XS)),
        pl.BlockSpec((CHUNK, 2 * GROUPS * STATE), lambda b, c: (row(b, c), CB_BC)),
        pl.BlockSpec((CHUNK, LANES), lambda b, c: (row(b, c), 0)),
        pl.BlockSpec((STATE, INNER), const),
        pl.BlockSpec((8, CONV_DIM), const),
        pl.BlockSpec((POOL_MAX, D_MODEL), const),
        pl.BlockSpec((LANES, INNER), const),
    ] + _param_specs(const)
    return pl.pallas_call(
        functools.partial(_seq_prompt_kernel, with_y=True),
        grid=(batch, nc),
        in_specs=in_specs,
        out_specs=[
            pl.BlockSpec((CHUNK, INNER), lambda b, c: (row(b, c), 0)),
            pl.BlockSpec((CHUNK, D_MODEL), lambda b, c: (row(b, c), 0)),
            pl.BlockSpec((None, INNER, STATE), lambda b, c: (b, 0, 0)),
            pl.BlockSpec((None, 8, CONV_DIM), lambda b, c: (b, 0, 0)),
            pl.BlockSpec((None, POOL_MAX, D_MODEL), lambda b, c: (b, 0, 0)),
        ],
        out_shape=[
            jax.ShapeDtypeStruct((batch * seq, INNER), BF16),
            jax.ShapeDtypeStruct((batch * seq, D_MODEL), BF16),
            jax.ShapeDtypeStruct((batch, INNER, STATE), F32),
            jax.ShapeDtypeStruct((batch, 8, CONV_DIM), F32),
            jax.ShapeDtypeStruct((batch, POOL_MAX, D_MODEL), F32),
        ],
        scratch_shapes=_seq_scratch(CHUNK),
        compiler_params=pltpu.CompilerParams(
            dimension_semantics=("arbitrary", "arbitrary"),
            vmem_limit_bytes=VMEM_LIMIT),
        name="seq_prompt",
    )(proj, proj, proj, proj, dt, h0t, ct0, pt0, e64, *params)


def _meta_state(proj, dt, e64, params, *, row_block):
    const = lambda b, c: (0, 0)
    zeros = functools.partial(jnp.zeros, dtype=F32)
    in_specs = [
        pl.BlockSpec((N_META, INNER), lambda b, c: (row_block, CB_ZS)),
        pl.BlockSpec((N_META, INNER), lambda b, c: (row_block, CB_U)),
        pl.BlockSpec((N_META, INNER), lambda b, c: (row_block, CB_XS)),
        pl.BlockSpec((N_META, 2 * GROUPS * STATE), lambda b, c: (row_block, CB_BC)),
        pl.BlockSpec((N_META, LANES), lambda b, c: (row_block, 0)),
        pl.BlockSpec((STATE, INNER), const),
        pl.BlockSpec((8, CONV_DIM), const),
        pl.BlockSpec((POOL_MAX, D_MODEL), const),
        pl.BlockSpec((LANES, INNER), const),
    ] + _param_specs(const)
    outs = pl.pallas_call(
        functools.partial(_seq_prompt_kernel, with_y=False),
        grid=(1, 1),
        in_specs=in_specs,
        out_specs=[
            pl.BlockSpec((N_META, INNER), const),
            pl.BlockSpec((N_META, D_MODEL), const),
            pl.BlockSpec((STATE, INNER), const),
            pl.BlockSpec((8, CONV_DIM), const),
            pl.BlockSpec((POOL_MAX, D_MODEL), const),
        ],
        out_shape=[
            jax.ShapeDtypeStruct((N_META, INNER), BF16),
            jax.ShapeDtypeStruct((N_META, D_MODEL), BF16),
            jax.ShapeDtypeStruct((STATE, INNER), F32),
            jax.ShapeDtypeStruct((8, CONV_DIM), F32),
            jax.ShapeDtypeStruct((POOL_MAX, D_MODEL), F32),
        ],
        scratch_shapes=_seq_scratch(N_META),
        compiler_params=pltpu.CompilerParams(
            dimension_semantics=("arbitrary", "arbitrary"),
            vmem_limit_bytes=VMEM_LIMIT),
        name="meta_state",
    )(proj, proj, proj, proj, dt, zeros((STATE, INNER)), zeros((8, CONV_DIM)),
      zeros((POOL_MAX, D_MODEL)), e64, *params)
    return outs[2], outs[3], outs[4]


def _seq_sample_kernel(zs_ref, u_ref, xs_ref, bc_ref, dt_ref, chist_ref, phist_ref, h0_ref, e64_ref,
                       cw_ref, cbias_ref, dtb_ref, alog_ref, dskip_ref, nw_ref,
                       y_ref, pooled_ref, hnew_ref, cnew_ref, pnew_ref,
                       xext_ref, uext_ref, act_ref, yoff_ref, dv_ref, s_ref):
    R = xs_ref.shape[0]
    T = R // SEQ_BLOCK
    XS = 8 + T
    US = POOL_MAX + T
    HIST = CONV_K - 1
    PH = POOL_MAX - 1
    half = R

    for i in range(SEQ_BLOCK):
        xext_ref[i * XS + 8 - HIST:i * XS + 8, :] = chist_ref[i]
        xext_ref[i * XS + 8:i * XS + 8 + T, 0:INNER] = xs_ref[i * T:(i + 1) * T, :]
        xext_ref[i * XS + 8:i * XS + 8 + T, INNER:CONV_DIM] = bc_ref[i * T:(i + 1) * T, :]
    for i in range(SEQ_BLOCK):
        _conv_silu(xext_ref, i * XS + 8 - HIST, T, cw_ref, cbias_ref, act_ref, i * T)
        cnew_ref[i] = xext_ref[(i + 1) * XS - HIST:(i + 1) * XS, :]

    r_i = _iota2((R, R), 0)
    c_i = _iota2((R, R), 1)
    same = _div_pow2(r_i, T) == _div_pow2(c_i, T)
    tri = jnp.where(same & (c_i <= r_i), 1.0, 0.0).astype(BF16)
    ones_bd = jnp.where(same, 1.0, 0.0).astype(BF16)
    dtv, da = _dt_terms(dt_ref, dtb_ref, alog_ref)
    da_parts = _split3(da)
    acum = _dot(tri, da_parts[0]) + _dot(tri, da_parts[1]) + _dot(tri, da_parts[2])
    a_end = (_dot(ones_bd, da_parts[0]) + _dot(ones_bd, da_parts[1])
             + _dot(ones_bd, da_parts[2]))

    acum_t2 = jnp.concatenate([acum, acum], axis=0).T
    aend_t = _pad_rows(a_end, LANES).T

    r_s = _iota2((LANES, SEQ_BLOCK * STATE), 0)
    c_s = _iota2((LANES, SEQ_BLOCK * STATE), 1)
    sel = jnp.where(r_s == _div_pow2(c_s, STATE) * T, 1.0, 0.0).astype(BF16)
    dv_ref[...] = jnp.exp(_dot_exact(_split3(aend_t), sel))

    lane2 = _iota2((R, 2 * R), 1)
    r_2 = _iota2((R, 2 * R), 0)
    c_2 = jnp.bitwise_and(lane2, R - 1)
    mask2 = (_div_pow2(r_2, T) == _div_pow2(c_2, T)) & (c_2 <= r_2)
    first_half = lane2 < half
    lane = _iota2((R, LANES), 1)
    r_b = _iota2((R, SEQ_BLOCK * STATE), 0)
    c_b = _iota2((R, SEQ_BLOCK * STATE), 1)
    bd_sel = _div_pow2(r_b, T) == _div_pow2(c_b, STATE)
    stacked = _split3(jnp.concatenate([acum, dtv, a_end], axis=0))

    for g in range(GROUPS):
        cols = slice(g * GROUP_W, (g + 1) * GROUP_W)
        ex = _dot_exact(stacked, e64_ref[:, cols])
        acx = ex[0:R, :]
        dtx = ex[R:2 * R, :]
        aendx = ex[2 * R:3 * R, :]
        xs_g = act_ref[:, cols]
        b_g = act_ref[:, INNER + g * STATE:INNER + (g + 1) * STATE]
        c_g = act_ref[:, INNER + GROUPS * STATE + g * STATE:
                      INNER + GROUPS * STATE + (g + 1) * STATE].astype(BF16)
        xdt = xs_g * dtx
        xdtw = xdt * jnp.exp(aendx - acx)

        for i in range(SEQ_BLOCK):
            h_i = h0_ref[i, g * GROUP_W:(g + 1) * GROUP_W, :].astype(BF16)
            yo = _dot_nt(c_g, h_i)
            yoff_ref[i * T:(i + 1) * T, cols] = yo[i * T:(i + 1) * T, :]
        xt = _pad_rows(xdtw, LANES).T
        b_big = jnp.where(bd_sel, jnp.concatenate([b_g] * SEQ_BLOCK, axis=1), 0.0)
        b_big = _pad_rows(b_big, LANES)
        s_ref[...] = _dot(xt.astype(BF16), b_big.astype(BF16))
        for i in range(SEQ_BLOCK):
            for hh in range(GROUP_W // HEADDIM):
                h = g * (GROUP_W // HEADDIM) + hh
                rows = slice(h * HEADDIM, (h + 1) * HEADDIM)
                hnew_ref[i, rows, :] = (
                    h0_ref[i, rows, :] * dv_ref[h:h + 1, i * STATE:(i + 1) * STATE]
                    + s_ref[hh * HEADDIM:(hh + 1) * HEADDIM, i * STATE:(i + 1) * STATE])

        cb2 = _dot_nt(c_g, jnp.concatenate([b_g, b_g], axis=0).astype(BF16))
        yds = []
        for q in range(GROUP_W // LANES):
            ha = g * 8 + 2 * q
            colc = jnp.where(first_half, acum[:, ha:ha + 1], acum[:, ha + 1:ha + 2])
            rowc = jnp.where(first_half[0:1, :], acum_t2[ha:ha + 1, :], acum_t2[ha + 1:ha + 2, :])
            dec = jnp.where(mask2, jnp.exp(colc - rowc), 0.0)
            m_p = (cb2 * dec).astype(BF16)
            xp = xdt[:, q * LANES:(q + 1) * LANES]
            r_m = jnp.concatenate([jnp.where(lane < HEADDIM, xp, 0.0),
                                   jnp.where(lane >= HEADDIM, xp, 0.0)], axis=0).astype(BF16)
            yds.append(_dot(m_p, r_m))
        y_g = (jnp.concatenate(yds, axis=1) + yoff_ref[:, cols] * jnp.exp(acx)
               + xs_g * dskip_ref[:, cols])
        _gate_norm_store(y_g, zs_ref[:, cols], nw_ref[:, cols], y_ref, cols)

    for i in range(SEQ_BLOCK):
        uext_ref[i * US + 1:i * US + POOL_MAX, :] = phist_ref[i]
        uext_ref[i * US + POOL_MAX:(i + 1) * US, :] = u_ref[i * T:(i + 1) * T, :]
    for i in range(SEQ_BLOCK):
        pnew_ref[i] = uext_ref[(i + 1) * US - PH:(i + 1) * US, :]
        for gi, w in enumerate(POOL_WINDOWS):
            cols = slice(gi * POOL_GW, (gi + 1) * POOL_GW)
            first = i * US + POOL_MAX
            u_cur = u_ref[i * T:(i + 1) * T, cols]
            s = u_cur
            for k in range(1, w):
                s = s + uext_ref[first - k:first - k + T, cols]
            pooled_ref[i * T:(i + 1) * T, cols] = (s * (1.0 / w) - u_cur).astype(BF16)


def _seq_sample(proj, dt, conv_hist, pool_hist, h0, e64, params, *, nseq, t_new):
    rb = SEQ_BLOCK * t_new
    assert 2 * rb == LANES
    const = lambda s: (0, 0)
    in_specs = [
        pl.BlockSpec((rb, INNER), lambda s: (s, CB_ZS)),
        pl.BlockSpec((rb, INNER), lambda s: (s, CB_U)),
        pl.BlockSpec((rb, INNER), lambda s: (s, CB_XS)),
        pl.BlockSpec((rb, 2 * GROUPS * STATE), lambda s: (s, CB_BC)),
        pl.BlockSpec((rb, LANES), lambda s: (s, 0)),
        pl.BlockSpec((SEQ_BLOCK, CONV_K - 1, CONV_DIM), lambda s: (s, 0, 0)),
        pl.BlockSpec((SEQ_BLOCK, POOL_MAX - 1, D_MODEL), lambda s: (s, 0, 0)),
        pl.BlockSpec((SEQ_BLOCK, INNER, STATE), lambda s: (s, 0, 0)),
        pl.BlockSpec((LANES, INNER), const),
    ] + _param_specs(const)
    return pl.pallas_call(
        _seq_sample_kernel,
        grid=(nseq // SEQ_BLOCK,),
        in_specs=in_specs,
        out_specs=[
            pl.BlockSpec((rb, INNER), lambda s: (s, 0)),
            pl.BlockSpec((rb, D_MODEL), lambda s: (s, 0)),
            pl.BlockSpec((SEQ_BLOCK, INNER, STATE), lambda s: (s, 0, 0)),
            pl.BlockSpec((SEQ_BLOCK, CONV_K - 1, CONV_DIM), lambda s: (s, 0, 0)),
            pl.BlockSpec((SEQ_BLOCK, POOL_MAX - 1, D_MODEL), lambda s: (s, 0, 0)),
        ],
        out_shape=[
            jax.ShapeDtypeStruct((nseq * t_new, INNER), BF16),
            jax.ShapeDtypeStruct((nseq * t_new, D_MODEL), BF16),
            jax.ShapeDtypeStruct((nseq, INNER, STATE), F32),
            jax.ShapeDtypeStruct((nseq, CONV_K - 1, CONV_DIM), F32),
            jax.ShapeDtypeStruct((nseq, POOL_MAX - 1, D_MODEL), F32),
        ],
        scratch_shapes=[
            pltpu.VMEM((SEQ_BLOCK * (8 + t_new), CONV_DIM), F32),
            pltpu.VMEM((SEQ_BLOCK * (POOL_MAX + t_new), D_MODEL), F32),
            pltpu.VMEM((rb, CONV_DIM), F32),
            pltpu.VMEM((rb, INNER), F32),
            pltpu.VMEM((LANES, SEQ_BLOCK * STATE), F32),
            pltpu.VMEM((GROUP_W, SEQ_BLOCK * STATE), F32),
        ],
        compiler_params=pltpu.CompilerParams(
            dimension_semantics=("arbitrary",),
            vmem_limit_bytes=VMEM_LIMIT),
        name="seq_sample",
    )(proj, proj, proj, proj, dt, conv_hist, pool_hist, h0, e64, *params)


def _merge_kernel(y_ref, pooled_ref, zp_ref, g1_ref, g2_ref, wmix_ref, bmix_ref, pscale_ref,
                  wps_ref, wpp_ref, o_ref, pout_ref):
    for g in range(len(POOL_WINDOWS)):
        cols = slice(g * POOL_GW, (g + 1) * POOL_GW)
        mixed = _dot(pooled_ref[:, cols], wmix_ref[g]) + bmix_ref[:, cols]
        pout_ref[:, cols] = (mixed * pscale_ref[:, cols] * _silu(zp_ref[:, cols])).astype(BF16)
    bs = _dot(y_ref[...], wps_ref[...])
    bp = _dot(pout_ref[...], wpp_ref[...])
    o_ref[...] = (_sigmoid(g1_ref[...]) * bs + _sigmoid(g2_ref[...]) * bp).astype(BF16)


def _merge(y, pooled, proj, wmix, bmix, pscale, wps, wpp, *, m, tm=256):
    const2 = lambda i: (0, 0)
    resident = dict(pipeline_mode=pl.Buffered(1))
    return pl.pallas_call(
        _merge_kernel,
        grid=(m // tm,),
        in_specs=[
            pl.BlockSpec((tm, INNER), lambda i: (i, 0)),
            pl.BlockSpec((tm, D_MODEL), lambda i: (i, 0)),
            pl.BlockSpec((tm, D_MODEL), lambda i: (i, CB_ZP)),
            pl.BlockSpec((tm, D_MODEL), lambda i: (i, CB_G1)),
            pl.BlockSpec((tm, D_MODEL), lambda i: (i, CB_G2)),
            pl.BlockSpec((len(POOL_WINDOWS), POOL_GW, POOL_GW), lambda i: (0, 0, 0), **resident),
            pl.BlockSpec((1, D_MODEL), const2),
            pl.BlockSpec((1, D_MODEL), const2),
            pl.BlockSpec((INNER, D_MODEL), const2, **resident),
            pl.BlockSpec((D_MODEL, D_MODEL), const2, **resident),
        ],
        out_specs=pl.BlockSpec((tm, D_MODEL), lambda i: (i, 0)),
        out_shape=jax.ShapeDtypeStruct((m, D_MODEL), BF16),
        scratch_shapes=[pltpu.VMEM((tm, D_MODEL), BF16)],
        compiler_params=pltpu.CompilerParams(
            dimension_semantics=("arbitrary",),
            vmem_limit_bytes=VMEM_LIMIT),
        name="merge",
    )(y, pooled, proj, proj, proj, wmix, bmix, pscale, wps, wpp)


def _out_kernel(m_ref, h_ref, wout_ref, fw_ref, o_ref):
    hn = h_ref[...] + _dot(m_ref[...], wout_ref[...])
    ms = jnp.mean(hn * hn, axis=-1, keepdims=True)
    o_ref[...] = (hn * lax.rsqrt(ms + EPS)) * fw_ref[...]


def _out(merged, h, wout, fw, *, m, tm=512):
    return pl.pallas_call(
        _out_kernel,
        grid=(m // tm,),
        in_specs=[
            pl.BlockSpec((tm, D_MODEL), lambda i: (i, 0)),
            pl.BlockSpec((tm, D_MODEL), lambda i: (i, 0)),
            pl.BlockSpec((D_MODEL, D_MODEL), lambda i: (0, 0)),
            pl.BlockSpec((1, D_MODEL), lambda i: (0, 0)),
        ],
        out_specs=pl.BlockSpec((tm, D_MODEL), lambda i: (i, 0)),
        out_shape=jax.ShapeDtypeStruct((m, D_MODEL), F32),
        compiler_params=pltpu.CompilerParams(
            dimension_semantics=("arbitrary",),
            vmem_limit_bytes=VMEM_LIMIT),
        name="out",
    )(merged, h, wout, fw)


def kernel(x_prompt, x_sample, state_conv, state_ssm, state_pool, meta_tokens, norm_w, w_in,
           conv_w, conv_b, dt_bias, a_log, d_skip, ssd_norm_w, w_proj_ssd, pool_mix_w,
           pool_mix_b, pool_scale, w_proj_pool, w_out, final_norm_w):
    batch, seq, _ = x_prompt.shape
    nseq, t_new, _ = x_sample.shape
    assert norm_w.shape[0] == 1, "single layer"
    assert w_in.shape == (1, D_MODEL, IN_ZP + 4 * D_MODEL)

    w_main, w_dt = _prep_weights(w_in)
    wps = w_proj_ssd[0].astype(BF16)
    wpp = w_proj_pool[0].astype(BF16)
    wout = w_out[0].astype(BF16)
    wmix = pool_mix_w[0].astype(BF16)
    lane_pad = lambda v: jnp.pad(v.reshape(1, HEADS), ((0, 0), (0, LANES - HEADS)))
    params = (conv_w[0], conv_b[0].reshape(1, CONV_DIM), lane_pad(dt_bias[0]), lane_pad(a_log[0]),
              jnp.repeat(d_skip[0], HEADDIM).reshape(1, INNER), ssd_norm_w[0].reshape(1, INNER))
    e64 = (jnp.arange(LANES)[:, None] == (jnp.arange(INNER)[None, :] // HEADDIM)).astype(BF16)
    nw = norm_w[0].reshape(1, D_MODEL)
    fw = final_norm_w.reshape(1, D_MODEL)
    bmix = pool_mix_b[0].reshape(1, D_MODEL)
    pscale = pool_scale[0].reshape(1, D_MODEL)

    xp = x_prompt.reshape(batch * seq, D_MODEL)
    m_s = nseq * t_new
    x_sm = jnp.concatenate([x_sample.reshape(m_s, D_MODEL), meta_tokens], axis=0)

    proj_p, dt_p = _proj(xp, nw, w_main, w_dt, tm=1024)
    proj_s, dt_s = _proj(x_sm, nw, w_main, w_dt, tm=m_s + N_META)

    h0t, ct0, pt0 = _meta_state(proj_s, dt_s, e64, params, row_block=m_s // N_META)

    y_p, pooled_p, ssm_p, conv_p, pool_p = _seq_prompt(
        proj_p, dt_p, h0t, ct0, pt0, e64, params, batch=batch, seq=seq)
    y_s, pooled_s, ssm_s, conv_s, pool_s = _seq_sample(
        proj_s, dt_s, state_conv[0], state_pool[0],
        state_ssm[0].reshape(nseq, INNER, STATE), e64, params, nseq=nseq, t_new=t_new)

    merged_p = _merge(y_p, pooled_p, proj_p, wmix, bmix, pscale, wps, wpp, m=batch * seq)
    merged_s = _merge(y_s, pooled_s, proj_s, wmix, bmix, pscale, wps, wpp, m=m_s)
    out_p = _out(merged_p, xp, wout, fw, m=batch * seq)
    out_s = _out(merged_s, x_sm, wout, fw, m=m_s)

    return (out_p.reshape(batch, seq, D_MODEL),
            out_s.reshape(nseq, t_new, D_MODEL),
            conv_p[:, 8 - (CONV_K - 1):][None].astype(state_conv.dtype),
            ssm_p.reshape(1, batch, HEADS, HEADDIM, STATE).astype(state_ssm.dtype),
            pool_p[:, 1:][None].astype(state_pool.dtype),
            conv_s[None].astype(state_conv.dtype),
            ssm_s.reshape(1, nseq, HEADS, HEADDIM, STATE).astype(state_ssm.dtype),
            pool_s[None].astype(state_pool.dtype))
```

```python
import functools

import jax
import jax.numpy as jnp
from jax import lax
from jax.experimental import pallas as pl
from jax.experimental.pallas import tpu as pltpu

F32 = jnp.float32
BF16 = jnp.bfloat16

D_MODEL = 2048
N_META = 16
EPS = 1e-6
HEADS = 32
HEADDIM = 64
GROUPS = 4
STATE = 128
CONV_K = 4
INNER = HEADS * HEADDIM
GROUP_W = INNER // GROUPS
CONV_DIM = INNER + 2 * GROUPS * STATE
POOL_WINDOWS = (2, 4, 8, 16)
POOL_MAX = 16
POOL_GW = D_MODEL // len(POOL_WINDOWS)

IN_XBC = INNER
IN_DT = IN_XBC + CONV_DIM
IN_ZP = IN_DT + HEADS
N_MAIN = 5 * D_MODEL + CONV_DIM
CB_ZS, CB_ZP, CB_U, CB_G1, CB_G2, CB_XS = 0, 1, 2, 3, 4, 5
CB_BC = 12

LANES = 128
CHUNK = 128
SEQ_BLOCK = 8
VMEM_LIMIT = 60 * 1024 * 1024
PREP_TN = 1024
PU_BASE = 2 * POOL_MAX


def _silu(x):
    hx = 0.5 * x
    return hx + hx * jnp.tanh(hx)


def _sigmoid(x):
    return 0.5 + 0.5 * jnp.tanh(0.5 * x)


def _dot(a, b):
    return jnp.dot(a, b, preferred_element_type=F32)


def _dot_nt(a, b):
    return lax.dot_general(a, b, (((1,), (1,)), ((), ())), preferred_element_type=F32)


def _split3(a):
    a1 = a.astype(BF16)
    r1 = a - a1.astype(F32)
    a2 = r1.astype(BF16)
    a3 = (r1 - a2.astype(F32)).astype(BF16)
    return a1, a2, a3


def _dot_exact(parts, b):
    out = _dot(parts[0], b)
    for p in parts[1:]:
        out = out + _dot(p, b)
    return out


def _dot_exact_l(a01, x):
    parts = _split3(x)
    out = _dot(a01, parts[0])
    for p in parts[1:]:
        out = out + _dot(a01, p)
    return out


def _iota2(shape, dim):
    return lax.broadcasted_iota(jnp.int32, shape, dim)


def _div_pow2(x, d):
    assert d & (d - 1) == 0
    return jnp.right_shift(x, d.bit_length() - 1)


def _pad_rows(x, rows):
    if x.shape[0] == rows:
        return x
    return jnp.concatenate([x, jnp.zeros((rows - x.shape[0], x.shape[1]), x.dtype)], axis=0)


def _prep_kernel(a_ref, d_ref, o_ref, dt_ref, *, sub):
    @pl.when(pl.program_id(0) == 0)
    def _():
        row = _iota2((LANES, D_MODEL), 0)
        dt_ref[...] = jnp.where(row < HEADS, d_ref[...], 0.0).astype(BF16)

    def body(r, carry):
        rows = pl.ds(pl.multiple_of(r * sub, sub), sub)
        o_ref[rows, :] = a_ref[rows, :].astype(BF16)
        return carry

    lax.fori_loop(0, PREP_TN // sub, body, 0)


def _prep_weights(w_t):
    def a_map(c):
        src = jnp.where(c < 2, c * PREP_TN,
                        jnp.where(c < 10, IN_ZP + (c - 2) * PREP_TN, IN_XBC + (c - 10) * PREP_TN))
        return (pl.multiple_of(src, 32), 0)

    return pl.pallas_call(
        functools.partial(_prep_kernel, sub=256),
        grid=(N_MAIN // PREP_TN,),
        in_specs=[
            pl.BlockSpec((pl.Element(PREP_TN), pl.Element(D_MODEL)), a_map),
            pl.BlockSpec((pl.Element(LANES), pl.Element(D_MODEL)), lambda c: (IN_DT, 0)),
        ],
        out_specs=[
            pl.BlockSpec((PREP_TN, D_MODEL), lambda c: (c, 0)),
            pl.BlockSpec((LANES, D_MODEL), lambda c: (0, 0)),
        ],
        out_shape=[
            jax.ShapeDtypeStruct((N_MAIN, D_MODEL), BF16),
            jax.ShapeDtypeStruct((LANES, D_MODEL), BF16),
        ],
        compiler_params=pltpu.CompilerParams(
            dimension_semantics=("arbitrary",), vmem_limit_bytes=VMEM_LIMIT),
        name="prep",
    )(w_t, w_t)


def _proj_kernel(x_ref, nw_ref, w_ref, wdt_ref, o_ref, dt_ref, xn_ref, *, sub):
    tm = x_ref.shape[0]

    @pl.when(pl.program_id(1) == 0)
    def _():
        def body(r, carry):
            rows = pl.ds(pl.multiple_of(r * sub, sub), sub)
            x = x_ref[rows, :]
            ms = jnp.mean(x * x, axis=-1, keepdims=True)
            xn = (x * lax.rsqrt(ms + EPS)) * nw_ref[...]
            xn_ref[rows, :] = xn.astype(BF16)
            return carry

        lax.fori_loop(0, tm // sub, body, 0)
        dt_ref[...] = _dot_nt(xn_ref[...], wdt_ref[...])

    o_ref[...] = _dot_nt(xn_ref[...], w_ref[...])


def _proj(x, norm_w, w_main, w_dt, *, tm, tn=1024):
    m = x.shape[0]
    assert m % tm == 0 and N_MAIN % tn == 0
    sub = 16 if tm % 128 else 128
    return pl.pallas_call(
        functools.partial(_proj_kernel, sub=sub),
        grid=(m // tm, N_MAIN // tn),
        in_specs=[
            pl.BlockSpec((tm, D_MODEL), lambda i, j: (i, 0)),
            pl.BlockSpec((1, D_MODEL), lambda i, j: (0, 0)),
            pl.BlockSpec((tn, D_MODEL), lambda i, j: (j, 0)),
            pl.BlockSpec((LANES, D_MODEL), lambda i, j: (0, 0)),
        ],
        out_specs=[
            pl.BlockSpec((tm, tn), lambda i, j: (i, j)),
            pl.BlockSpec((tm, LANES), lambda i, j: (i, 0)),
        ],
        out_shape=[
            jax.ShapeDtypeStruct((m, N_MAIN), F32),
            jax.ShapeDtypeStruct((m, LANES), F32),
        ],
        scratch_shapes=[pltpu.VMEM((tm, D_MODEL), BF16)],
        compiler_params=pltpu.CompilerParams(
            dimension_semantics=("arbitrary", "arbitrary"),
            vmem_limit_bytes=VMEM_LIMIT),
        name="proj",
    )(x, norm_w, w_main, w_dt)


def _conv_silu(ext_ref, first, rows, cw_ref, cbias_ref, act_ref, row0):
    for cs in range(CONV_DIM // GROUP_W):
        cols = slice(cs * GROUP_W, (cs + 1) * GROUP_W)
        acc = cbias_ref[:, cols] + ext_ref[first:first + rows, cols] * cw_ref[0:1, cols]
        for k in range(1, CONV_K):
            acc = acc + ext_ref[first + k:first + k + rows, cols] * cw_ref[k:k + 1, cols]
        act_ref[row0:row0 + rows, cols] = _silu(acc)


def _dt_terms(dt_ref, dtb_ref, alog_ref):
    x = dt_ref[...] + dtb_ref[...]
    dtv = jnp.maximum(x, 0.0) + jnp.log1p(jnp.exp(-jnp.abs(x)))
    da = dtv * (-jnp.exp(alog_ref[...]))
    return dtv, da


def _gate_norm_store(y_g, zs, nw, y_ref, cols):
    y_g = y_g * _silu(zs)
    ms = jnp.mean(y_g * y_g, axis=-1, keepdims=True)
    y_ref[:, cols] = (y_g * lax.rsqrt(ms + EPS) * nw).astype(BF16)


def _pool_prompt(uext_ref, l1_ref, l2_ref, l3_ref, pooled_ref, rows):
    base, end = PU_BASE, PU_BASE + rows
    for gi, w in enumerate(POOL_WINDOWS):
        cols = slice(gi * POOL_GW, (gi + 1) * POOL_GW)
        levels = w.bit_length() - 1
        lo = base - 8 * (levels - 1)
        if levels == 1:
            s = uext_ref[base:end, cols] + uext_ref[base - 1:end - 1, cols]
        else:
            l1_ref[lo:end, :] = uext_ref[lo:end, cols] + uext_ref[lo - 1:end - 1, cols]
            if levels == 2:
                s = l1_ref[base:end, :] + l1_ref[base - 2:end - 2, :]
            else:
                lo2 = lo + 8
                l2_ref[lo2:end, :] = l1_ref[lo2:end, :] + l1_ref[lo2 - 2:end - 2, :]
                if levels == 3:
                    s = l2_ref[base:end, :] + l2_ref[base - 4:end - 4, :]
                else:
                    lo3 = lo2 + 8
                    l3_ref[lo3:end, :] = l2_ref[lo3:end, :] + l2_ref[lo3 - 4:end - 4, :]
                    s = l3_ref[base:end, :] + l3_ref[base - 8:end - 8, :]
        pooled_ref[:, cols] = (s * (1.0 / w) - uext_ref[base:end, cols]).astype(BF16)


def _seq_prompt_kernel(zs_ref, u_ref, xs_ref, bc_ref, dt_ref, h0_ref, ct0_ref, pt0_ref, e64_ref,
                       cw_ref, cbias_ref, dtb_ref, alog_ref, dskip_ref, nw_ref,
                       y_ref, pooled_ref, hfin_ref, ctail_ref, ptail_ref,
                       ht_ref, xext_ref, uext_ref, act_ref, l1_ref, l2_ref, l3_ref, *, with_y):
    L = xs_ref.shape[0]
    c = pl.program_id(1)
    nc = pl.num_programs(1)

    @pl.when(c == 0)
    def _():
        ht_ref[...] = h0_ref[...]
        xext_ref[0:8, :] = ct0_ref[...]
        uext_ref[0:POOL_MAX, :] = jnp.zeros((POOL_MAX, D_MODEL), F32)
        uext_ref[POOL_MAX:PU_BASE, :] = pt0_ref[...]

    xext_ref[8:8 + L, 0:INNER] = xs_ref[...]
    xext_ref[8:8 + L, INNER:CONV_DIM] = bc_ref[...]
    _conv_silu(xext_ref, 8 - (CONV_K - 1), L, cw_ref, cbias_ref, act_ref, 0)
    xext_ref[0:8, :] = xext_ref[L:L + 8, :]

    r_i = _iota2((L, L), 0)
    c_i = _iota2((L, L), 1)
    causal = c_i <= r_i
    tri = jnp.where(causal, 1.0, 0.0).astype(BF16)
    dtv, da = _dt_terms(dt_ref, dtb_ref, alog_ref)
    acum = _dot_exact_l(tri, da)
    stacked = _split3(jnp.concatenate([acum, dtv], axis=0))
    if with_y:
        acum_t = acum.T
        lane = _iota2((L, LANES), 1)

    for g in range(GROUPS):
        cols = slice(g * GROUP_W, (g + 1) * GROUP_W)
        ex = _dot_exact(stacked, e64_ref[:, cols])
        acx = ex[0:L, :]
        dtx = ex[L:2 * L, :]
        xs_g = act_ref[:, cols]
        b_g = act_ref[:, INNER + g * STATE:INNER + (g + 1) * STATE]
        xdt = xs_g * dtx
        aend = acx[L - 1:L, :]
        xdtw = _pad_rows(xdt * jnp.exp(aend - acx), LANES).astype(BF16)
        b_t = _pad_rows(b_g, LANES).T
        h_old = ht_ref[:, cols]
        ht_ref[:, cols] = h_old * jnp.exp(aend) + _dot(b_t.astype(BF16), xdtw)

        if with_y:
            c_g = act_ref[:, INNER + GROUPS * STATE + g * STATE:
                          INNER + GROUPS * STATE + (g + 1) * STATE].astype(BF16)
            cb = _dot_nt(c_g, b_g.astype(BF16))
            yoff = _dot(c_g, h_old.astype(BF16)) * jnp.exp(acx)
            yds = []
            for q in range(GROUP_W // LANES):
                ms_ = []
                for h in (g * 8 + 2 * q, g * 8 + 2 * q + 1):
                    seg = acum[:, h:h + 1] - acum_t[h:h + 1, :]
                    dec = jnp.where(causal, jnp.exp(seg), 0.0)
                    ms_.append((cb * dec).astype(BF16))
                m_cat = jnp.concatenate(ms_, axis=1)
                xp = xdt[:, q * LANES:(q + 1) * LANES]
                r_m = jnp.concatenate([jnp.where(lane < HEADDIM, xp, 0.0),
                                       jnp.where(lane >= HEADDIM, xp, 0.0)], axis=0).astype(BF16)
                yds.append(_dot(m_cat, r_m))
            y_g = jnp.concatenate(yds, axis=1) + yoff + xs_g * dskip_ref[:, cols]
            _gate_norm_store(y_g, zs_ref[:, cols], nw_ref[:, cols], y_ref, cols)

    if with_y:
        uext_ref[PU_BASE:PU_BASE + L, :] = u_ref[...]
        _pool_prompt(uext_ref, l1_ref, l2_ref, l3_ref, pooled_ref, L)
        uext_ref[POOL_MAX:PU_BASE, :] = uext_ref[L + POOL_MAX:L + PU_BASE, :]

        @pl.when(c == nc - 1)
        def _():
            for k in range(INNER // LANES):
                hfin_ref[k * LANES:(k + 1) * LANES, :] = ht_ref[:, k * LANES:(k + 1) * LANES].T
            ctail_ref[...] = xext_ref[0:8, :]
            ptail_ref[...] = uext_ref[POOL_MAX:PU_BASE, :]
    else:
        hfin_ref[...] = ht_ref[...]
        ctail_ref[...] = xext_ref[0:8, :]
        ptail_ref[...] = u_ref[...]
        y_ref[...] = jnp.zeros(y_ref.shape, y_ref.dtype)
        pooled_ref[...] = jnp.zeros(pooled_ref.shape, pooled_ref.dtype)


def _param_specs(index_map):
    return [
        pl.BlockSpec((CONV_K, CONV_DIM), index_map),
        pl.BlockSpec((1, CONV_DIM), index_map),
        pl.BlockSpec((1, LANES), index_map),
        pl.BlockSpec((1, LANES), index_map),
        pl.BlockSpec((1, INNER), index_map),
        pl.BlockSpec((1, INNER), index_map),
    ]


def _seq_scratch(rows):
    return [
        pltpu.VMEM((STATE, INNER), F32),
        pltpu.VMEM((8 + rows, CONV_DIM), F32),
        pltpu.VMEM((PU_BASE + rows, D_MODEL), F32),
        pltpu.VMEM((rows, CONV_DIM), F32),
        pltpu.VMEM((PU_BASE + rows, POOL_GW), F32),
        pltpu.VMEM((PU_BASE + rows, POOL_GW), F32),
        pltpu.VMEM((PU_BASE + rows, POOL_GW), F32),
    ]


def _seq_prompt(proj, dt, h0t, ct0, pt0, e64, params, *, batch, seq):
    nc = seq // CHUNK
    row = lambda b, c: b * nc + c
    const = lambda b, c: (0, 0)
    in_specs = [
        pl.BlockSpec((CHUNK, INNER), lambda b, c: (row(b, c), CB_ZS)),
        pl.BlockSpec((CHUNK, INNER), lambda b, c: (row(b, c), CB_U)),
        pl.BlockSpec((CHUNK, INNER), lambda b, c: (row(b, c), CB_XS)),
        pl.BlockSpec((CHUNK, 2 * GROUPS * STATE), lambda b, c: (row(b, c), CB_BC)),
        pl.BlockSpec((CHUNK, LANES), lambda b, c: (row(b, c), 0)),
        pl.BlockSpec((STATE, INNER), const),
        pl.BlockSpec((8, CONV_DIM), const),
        pl.BlockSpec((POOL_MAX, D_MODEL), const),
        pl.BlockSpec((LANES, INNER), const),
    ] + _param_specs(const)
    return pl.pallas_call(
        functools.partial(_seq_prompt_kernel, with_y=True),
        grid=(batch, nc),
        in_specs=in_specs,
        out_specs=[
            pl.BlockSpec((CHUNK, INNER), lambda b, c: (row(b, c), 0)),
            pl.BlockSpec((CHUNK, D_MODEL), lambda b, c: (row(b, c), 0)),
            pl.BlockSpec((None, INNER, STATE), lambda b, c: (b, 0, 0)),
            pl.BlockSpec((None, 8, CONV_DIM), lambda b, c: (b, 0, 0)),
            pl.BlockSpec((None, POOL_MAX, D_MODEL), lambda b, c: (b, 0, 0)),
        ],
        out_shape=[
            jax.ShapeDtypeStruct((batch * seq, INNER), BF16),
            jax.ShapeDtypeStruct((batch * seq, D_MODEL), BF16),
            jax.ShapeDtypeStruct((batch, INNER, STATE), F32),
            jax.ShapeDtypeStruct((batch, 8, CONV_DIM), F32),
            jax.ShapeDtypeStruct((batch, POOL_MAX, D_MODEL), F32),
        ],
        scratch_shapes=_seq_scratch(CHUNK),
        compiler_params=pltpu.CompilerParams(
            dimension_semantics=("arbitrary", "arbitrary"),
            vmem_limit_bytes=VMEM_LIMIT),
        name="seq_prompt",
    )(proj, proj, proj, proj, dt, h0t, ct0, pt0, e64, *params)


def _meta_state(proj, dt, e64, params, *, row_block):
    const = lambda b, c: (0, 0)
    zeros = functools.partial(jnp.zeros, dtype=F32)
    in_specs = [
        pl.BlockSpec((N_META, INNER), lambda b, c: (row_block, CB_ZS)),
        pl.BlockSpec((N_META, INNER), lambda b, c: (row_block, CB_U)),
        pl.BlockSpec((N_META, INNER), lambda b, c: (row_block, CB_XS)),
        pl.BlockSpec((N_META, 2 * GROUPS * STATE), lambda b, c: (row_block, CB_BC)),
        pl.BlockSpec((N_META, LANES), lambda b, c: (row_block, 0)),
        pl.BlockSpec((STATE, INNER), const),
        pl.BlockSpec((8, CONV_DIM), const),
        pl.BlockSpec((POOL_MAX, D_MODEL), const),
        pl.BlockSpec((LANES, INNER), const),
    ] + _param_specs(const)
    outs = pl.pallas_call(
        functools.partial(_seq_prompt_kernel, with_y=False),
        grid=(1, 1),
        in_specs=in_specs,
        out_specs=[
            pl.BlockSpec((N_META, INNER), const),
            pl.BlockSpec((N_META, D_MODEL), const),
            pl.BlockSpec((STATE, INNER), const),
            pl.BlockSpec((8, CONV_DIM), const),
            pl.BlockSpec((POOL_MAX, D_MODEL), const),
        ],
        out_shape=[
            jax.ShapeDtypeStruct((N_META, INNER), BF16),
            jax.ShapeDtypeStruct((N_META, D_MODEL), BF16),
            jax.ShapeDtypeStruct((STATE, INNER), F32),
            jax.ShapeDtypeStruct((8, CONV_DIM), F32),
            jax.ShapeDtypeStruct((POOL_MAX, D_MODEL), F32),
        ],
        scratch_shapes=_seq_scratch(N_META),
        compiler_params=pltpu.CompilerParams(
            dimension_semantics=("arbitrary", "arbitrary"),
            vmem_limit_bytes=VMEM_LIMIT),
        name="meta_state",
    )(proj, proj, proj, proj, dt, zeros((STATE, INNER)), zeros((8, CONV_DIM)),
      zeros((POOL_MAX, D_MODEL)), e64, *params)
    return outs[2], outs[3], outs[4]


def _seq_sample_kernel(zs_ref, u_ref, xs_ref, bc_ref, dt_ref, chist_ref, phist_ref, h0_ref, e64_ref,
                       cw_ref, cbias_ref, dtb_ref, alog_ref, dskip_ref, nw_ref,
                       y_ref, pooled_ref, hnew_ref, cnew_ref, pnew_ref,
                       xext_ref, uext_ref, act_ref, yoff_ref, dv_ref, s_ref):
    R = xs_ref.shape[0]
    T = R // SEQ_BLOCK
    XS = 8 + T
    US = POOL_MAX + T
    HIST = CONV_K - 1
    PH = POOL_MAX - 1
    half = R

    for i in range(SEQ_BLOCK):
        xext_ref[i * XS + 8 - HIST:i * XS + 8, :] = chist_ref[i]
        xext_ref[i * XS + 8:i * XS + 8 + T, 0:INNER] = xs_ref[i * T:(i + 1) * T, :]
        xext_ref[i * XS + 8:i * XS + 8 + T, INNER:CONV_DIM] = bc_ref[i * T:(i + 1) * T, :]
    for i in range(SEQ_BLOCK):
        _conv_silu(xext_ref, i * XS + 8 - HIST, T, cw_ref, cbias_ref, act_ref, i * T)
        cnew_ref[i] = xext_ref[(i + 1) * XS - HIST:(i + 1) * XS, :]

    r_i = _iota2((R, R), 0)
    c_i = _iota2((R, R), 1)
    same = _div_pow2(r_i, T) == _div_pow2(c_i, T)
    tri = jnp.where(same & (c_i <= r_i), 1.0, 0.0).astype(BF16)
    ones_bd = jnp.where(same, 1.0, 0.0).astype(BF16)
    dtv, da = _dt_terms(dt_ref, dtb_ref, alog_ref)
    da_parts = _split3(da)
    acum = _dot(tri, da_parts[0]) + _dot(tri, da_parts[1]) + _dot(tri, da_parts[2])
    a_end = (_dot(ones_bd, da_parts[0]) + _dot(ones_bd, da_parts[1])
             + _dot(ones_bd, da_parts[2]))

    acum_t2 = jnp.concatenate([acum, acum], axis=0).T
    aend_t = _pad_rows(a_end, LANES).T

    r_s = _iota2((LANES, SEQ_BLOCK * STATE), 0)
    c_s = _iota2((LANES, SEQ_BLOCK * STATE), 1)
    sel = jnp.where(r_s == _div_pow2(c_s, STATE) * T, 1.0, 0.0).astype(BF16)
    dv_ref[...] = jnp.exp(_dot_exact(_split3(aend_t), sel))

    lane2 = _iota2((R, 2 * R), 1)
    r_2 = _iota2((R, 2 * R), 0)
    c_2 = jnp.bitwise_and(lane2, R - 1)
    mask2 = (_div_pow2(r_2, T) == _div_pow2(c_2, T)) & (c_2 <= r_2)
    first_half = lane2 < half
    lane = _iota2((R, LANES), 1)
    r_b = _iota2((R, SEQ_BLOCK * STATE), 0)
    c_b = _iota2((R, SEQ_BLOCK * STATE), 1)
    bd_sel = _div_pow2(r_b, T) == _div_pow2(c_b, STATE)
    stacked = _split3(jnp.concatenate([acum, dtv, a_end], axis=0))

    for g in range(GROUPS):
        cols = slice(g * GROUP_W, (g + 1) * GROUP_W)
        ex = _dot_exact(stacked, e64_ref[:, cols])
        acx = ex[0:R, :]
        dtx = ex[R:2 * R, :]
        aendx = ex[2 * R:3 * R, :]
        xs_g = act_ref[:, cols]
        b_g = act_ref[:, INNER + g * STATE:INNER + (g + 1) * STATE]
        c_g = act_ref[:, INNER + GROUPS * STATE + g * STATE:
                      INNER + GROUPS * STATE + (g + 1) * STATE].astype(BF16)
        xdt = xs_g * dtx
        xdtw = xdt * jnp.exp(aendx - acx)

        for i in range(SEQ_BLOCK):
            h_i = h0_ref[i, g * GROUP_W:(g + 1) * GROUP_W, :].astype(BF16)
            yo = _dot_nt(c_g, h_i)
            yoff_ref[i * T:(i + 1) * T, cols] = yo[i * T:(i + 1) * T, :]
        xt = _pad_rows(xdtw, LANES).T
        b_big = jnp.where(bd_sel, jnp.concatenate([b_g] * SEQ_BLOCK, axis=1), 0.0)
        b_big = _pad_rows(b_big, LANES)
        s_ref[...] = _dot(xt.astype(BF16), b_big.astype(BF16))
        for i in range(SEQ_BLOCK):
            for hh in range(GROUP_W // HEADDIM):
                h = g * (GROUP_W // HEADDIM) + hh
                rows = slice(h * HEADDIM, (h + 1) * HEADDIM)
                hnew_ref[i, rows, :] = (
                    h0_ref[i, rows, :] * dv_ref[h:h + 1, i * STATE:(i + 1) * STATE]
                    + s_ref[hh * HEADDIM:(hh + 1) * HEADDIM, i * STATE:(i + 1) * STATE])

        cb2 = _dot_nt(c_g, jnp.concatenate([b_g, b_g], axis=0).astype(BF16))
        yds = []
        for q in range(GROUP_W // LANES):
            ha = g * 8 + 2 * q
            colc = jnp.where(first_half, acum[:, ha:ha + 1], acum[:, ha + 1:ha + 2])
            rowc = jnp.where(first_half[0:1, :], acum_t2[ha:ha + 1, :], acum_t2[ha + 1:ha + 2, :])
            dec = jnp.where(mask2, jnp.exp(colc - rowc), 0.0)
            m_p = (cb2 * dec).astype(BF16)
            xp = xdt[:, q * LANES:(q + 1) * LANES]
            r_m = jnp.concatenate([jnp.where(lane < HEADDIM, xp, 0.0),
                                   jnp.where(lane >= HEADDIM, xp, 0.0)], axis=0).astype(BF16)
            yds.append(_dot(m_p, r_m))
        y_g = (jnp.concatenate(yds, axis=1) + yoff_ref[:, cols] * jnp.exp(acx)
               + xs_g * dskip_ref[:, cols])
        _gate_norm_store(y_g, zs_ref[:, cols], nw_ref[:, cols], y_ref, cols)

    for i in range(SEQ_BLOCK):
        uext_ref[i * US + 1:i * US + POOL_MAX, :] = phist_ref[i]
        uext_ref[i * US + POOL_MAX:(i + 1) * US, :] = u_ref[i * T:(i + 1) * T, :]
    for i in range(SEQ_BLOCK):
        pnew_ref[i] = uext_ref[(i + 1) * US - PH:(i + 1) * US, :]
        for gi, w in enumerate(POOL_WINDOWS):
            cols = slice(gi * POOL_GW, (gi + 1) * POOL_GW)
            first = i * US + POOL_MAX
            u_cur = u_ref[i * T:(i + 1) * T, cols]
            s = u_cur
            for k in range(1, w):
                s = s + uext_ref[first - k:first - k + T, cols]
            pooled_ref[i * T:(i + 1) * T, cols] = (s * (1.0 / w) - u_cur).astype(BF16)


def _seq_sample(proj, dt, conv_hist, pool_hist, h0, e64, params, *, nseq, t_new):
    rb = SEQ_BLOCK * t_new
    assert 2 * rb == LANES
    const = lambda s: (0, 0)
    in_specs = [
        pl.BlockSpec((rb, INNER), lambda s: (s, CB_ZS)),
        pl.BlockSpec((rb, INNER), lambda s: (s, CB_U)),
        pl.BlockSpec((rb, INNER), lambda s: (s, CB_XS)),
        pl.BlockSpec((rb, 2 * GROUPS * STATE), lambda s: (s, CB_BC)),
        pl.BlockSpec((rb, LANES), lambda s: (s, 0)),
        pl.BlockSpec((SEQ_BLOCK, CONV_K - 1, CONV_DIM), lambda s: (s, 0, 0)),
        pl.BlockSpec((SEQ_BLOCK, POOL_MAX - 1, D_MODEL), lambda s: (s, 0, 0)),
        pl.BlockSpec((SEQ_BLOCK, INNER, STATE), lambda s: (s, 0, 0)),
        pl.BlockSpec((LANES, INNER), const),
    ] + _param_specs(const)
    return pl.pallas_call(
        _seq_sample_kernel,
        grid=(nseq // SEQ_BLOCK,),
        in_specs=in_specs,
        out_specs=[
            pl.BlockSpec((rb, INNER), lambda s: (s, 0)),
            pl.BlockSpec((rb, D_MODEL), lambda s: (s, 0)),
            pl.BlockSpec((SEQ_BLOCK, INNER, STATE), lambda s: (s, 0, 0)),
            pl.BlockSpec((SEQ_BLOCK, CONV_K - 1, CONV_DIM), lambda s: (s, 0, 0)),
            pl.BlockSpec((SEQ_BLOCK, POOL_MAX - 1, D_MODEL), lambda s: (s, 0, 0)),
        ],
        out_shape=[
            jax.ShapeDtypeStruct((nseq * t_new, INNER), BF16),
            jax.ShapeDtypeStruct((nseq * t_new, D_MODEL), BF16),
            jax.ShapeDtypeStruct((nseq, INNER, STATE), F32),
            jax.ShapeDtypeStruct((nseq, CONV_K - 1, CONV_DIM), F32),
            jax.ShapeDtypeStruct((nseq, POOL_MAX - 1, D_MODEL), F32),
        ],
        scratch_shapes=[
            pltpu.VMEM((SEQ_BLOCK * (8 + t_new), CONV_DIM), F32),
            pltpu.VMEM((SEQ_BLOCK * (POOL_MAX + t_new), D_MODEL), F32),
            pltpu.VMEM((rb, CONV_DIM), F32),
            pltpu.VMEM((rb, INNER), F32),
            pltpu.VMEM((LANES, SEQ_BLOCK * STATE), F32),
            pltpu.VMEM((GROUP_W, SEQ_BLOCK * STATE), F32),
        ],
        compiler_params=pltpu.CompilerParams(
            dimension_semantics=("arbitrary",),
            vmem_limit_bytes=VMEM_LIMIT),
        name="seq_sample",
    )(proj, proj, proj, proj, dt, conv_hist, pool_hist, h0, e64, *params)


def _merge_kernel(y_ref, pooled_ref, zp_ref, g1_ref, g2_ref, wmix_ref, bmix_ref, pscale_ref,
                  wps_ref, wpp_ref, o_ref, pout_ref):
    for g in range(len(POOL_WINDOWS)):
        cols = slice(g * POOL_GW, (g + 1) * POOL_GW)
        mixed = _dot(pooled_ref[:, cols], wmix_ref[g]) + bmix_ref[:, cols]
        pout_ref[:, cols] = (mixed * pscale_ref[:, cols] * _silu(zp_ref[:, cols])).astype(BF16)
    bs = _dot(y_ref[...], wps_ref[...])
    bp = _dot(pout_ref[...], wpp_ref[...])
    o_ref[...] = (_sigmoid(g1_ref[...]) * bs + _sigmoid(g2_ref[...]) * bp).astype(BF16)


def _merge(y, pooled, proj, wmix, bmix, pscale, wps, wpp, *, m, tm=256):
    const2 = lambda i: (0, 0)
    resident = dict(pipeline_mode=pl.Buffered(1))
    return pl.pallas_call(
        _merge_kernel,
        grid=(m // tm,),
        in_specs=[
            pl.BlockSpec((tm, INNER), lambda i: (i, 0)),
            pl.BlockSpec((tm, D_MODEL), lambda i: (i, 0)),
            pl.BlockSpec((tm, D_MODEL), lambda i: (i, CB_ZP)),
            pl.BlockSpec((tm, D_MODEL), lambda i: (i, CB_G1)),
            pl.BlockSpec((tm, D_MODEL), lambda i: (i, CB_G2)),
            pl.BlockSpec((len(POOL_WINDOWS), POOL_GW, POOL_GW), lambda i: (0, 0, 0), **resident),
            pl.BlockSpec((1, D_MODEL), const2),
            pl.BlockSpec((1, D_MODEL), const2),
            pl.BlockSpec((INNER, D_MODEL), const2, **resident),
            pl.BlockSpec((D_MODEL, D_MODEL), const2, **resident),
        ],
        out_specs=pl.BlockSpec((tm, D_MODEL), lambda i: (i, 0)),
        out_shape=jax.ShapeDtypeStruct((m, D_MODEL), BF16),
        scratch_shapes=[pltpu.VMEM((tm, D_MODEL), BF16)],
        compiler_params=pltpu.CompilerParams(
            dimension_semantics=("arbitrary",),
            vmem_limit_bytes=VMEM_LIMIT),
        name="merge",
    )(y, pooled, proj, proj, proj, wmix, bmix, pscale, wps, wpp)


def _out_kernel(m_ref, h_ref, wout_ref, fw_ref, o_ref):
    hn = h_ref[...] + _dot(m_ref[...], wout_ref[...])
    ms = jnp.mean(hn * hn, axis=-1, keepdims=True)
    o_ref[...] = (hn * lax.rsqrt(ms + EPS)) * fw_ref[...]


def _out(merged, h, wout, fw, *, m, tm=512):
    return pl.pallas_call(
        _out_kernel,
        grid=(m // tm,),
        in_specs=[
            pl.BlockSpec((tm, D_MODEL), lambda i: (i, 0)),
            pl.BlockSpec((tm, D_MODEL), lambda i: (i, 0)),
            pl.BlockSpec((D_MODEL, D_MODEL), lambda i: (0, 0)),
            pl.BlockSpec((1, D_MODEL), lambda i: (0, 0)),
        ],
        out_specs=pl.BlockSpec((tm, D_MODEL), lambda i: (i, 0)),
        out_shape=jax.ShapeDtypeStruct((m, D_MODEL), F32),
        compiler_params=pltpu.CompilerParams(
            dimension_semantics=("arbitrary",),
            vmem_limit_bytes=VMEM_LIMIT),
        name="out",
    )(merged, h, wout, fw)


def kernel(x_prompt, x_sample, state_conv, state_ssm, state_pool, meta_tokens, norm_w, w_in,
           conv_w, conv_b, dt_bias, a_log, d_skip, ssd_norm_w, w_proj_ssd, pool_mix_w,
           pool_mix_b, pool_scale, w_proj_pool, w_out, final_norm_w):
    batch, seq, _ = x_prompt.shape
    nseq, t_new, _ = x_sample.shape
    assert norm_w.shape[0] == 1, "single layer"
    assert w_in.shape == (1, D_MODEL, IN_ZP + 4 * D_MODEL)

    w_main, w_dt = _prep_weights(jnp.swapaxes(w_in, 1, 2)[0])
    wps = w_proj_ssd[0].astype(BF16)
    wpp = w_proj_pool[0].astype(BF16)
    wout = w_out[0].astype(BF16)
    wmix = pool_mix_w[0].astype(BF16)
    lane_pad = lambda v: jnp.pad(v.reshape(1, HEADS), ((0, 0), (0, LANES - HEADS)))
    params = (conv_w[0], conv_b[0].reshape(1, CONV_DIM), lane_pad(dt_bias[0]), lane_pad(a_log[0]),
              jnp.repeat(d_skip[0], HEADDIM).reshape(1, INNER), ssd_norm_w[0].reshape(1, INNER))
    e64 = (jnp.arange(LANES)[:, None] == (jnp.arange(INNER)[None, :] // HEADDIM)).astype(BF16)
    nw = norm_w[0].reshape(1, D_MODEL)
    fw = final_norm_w.reshape(1, D_MODEL)
    bmix = pool_mix_b[0].reshape(1, D_MODEL)
    pscale = pool_scale[0].reshape(1, D_MODEL)

    xp = x_prompt.reshape(batch * seq, D_MODEL)
    m_s = nseq * t_new
    x_sm = jnp.concatenate([x_sample.reshape(m_s, D_MODEL), meta_tokens], axis=0)

    proj_p, dt_p = _proj(xp, nw, w_main, w_dt, tm=1024)
    proj_s, dt_s = _proj(x_sm, nw, w_main, w_dt, tm=m_s + N_META)

    h0t, ct0, pt0 = _meta_state(proj_s, dt_s, e64, params, row_block=m_s // N_META)

    y_p, pooled_p, ssm_p, conv_p, pool_p = _seq_prompt(
        proj_p, dt_p, h0t, ct0, pt0, e64, params, batch=batch, seq=seq)
    y_s, pooled_s, ssm_s, conv_s, pool_s = _seq_sample(
        proj_s, dt_s, state_conv[0], state_pool[0],
        state_ssm[0].reshape(nseq, INNER, STATE), e64, params, nseq=nseq, t_new=t_new)

    merged_p = _merge(y_p, pooled_p, proj_p, wmix, bmix, pscale, wps, wpp, m=batch * seq)
    merged_s = _merge(y_s, pooled_s, proj_s, wmix, bmix, pscale, wps, wpp, m=m_s)
    out_p = _out(merged_p, xp, wout, fw, m=batch * seq)
    out_s = _out(merged_s, x_sm, wout, fw, m=m_s)

    return (out_p.reshape(batch, seq, D_MODEL),
            out_s.reshape(nseq, t_new, D_MODEL),
            conv_p[:, 8 - (CONV_K - 1):][None].astype(state_conv.dtype),
            ssm_p.reshape(1, batch, HEADS, HEADDIM, STATE).astype(state_ssm.dtype),
            pool_p[:, 1:][None].astype(state_pool.dtype),
            conv_s[None].astype(state_conv.dtype),
            ssm_s.reshape(1, nseq, HEADS, HEADDIM, STATE).astype(state_ssm.dtype),
            pool_s[None].astype(state_pool.dtype))
```

```python
import functools

import jax
import jax.numpy as jnp
from jax import lax
from jax.experimental import pallas as pl
from jax.experimental.pallas import tpu as pltpu

F32 = jnp.float32
BF16 = jnp.bfloat16

D_MODEL = 2048
N_META = 16
EPS = 1e-6
HEADS = 32
HEADDIM = 64
GROUPS = 4
STATE = 128
CONV_K = 4
INNER = HEADS * HEADDIM
GROUP_W = INNER // GROUPS
CONV_DIM = INNER + 2 * GROUPS * STATE
POOL_WINDOWS = (2, 4, 8, 16)
POOL_MAX = 16
POOL_GW = D_MODEL // len(POOL_WINDOWS)

IN_XBC = INNER
IN_DT = IN_XBC + CONV_DIM
IN_ZP = IN_DT + HEADS
N_MAIN = 5 * D_MODEL + CONV_DIM
CB_ZS, CB_ZP, CB_U, CB_G1, CB_G2, CB_XS = 0, 1, 2, 3, 4, 5
CB_BC = 12

LANES = 128
CHUNK = 128
SEQ_BLOCK = 8
VMEM_LIMIT = 60 * 1024 * 1024
PREP_TN = 1024
PU_BASE = 2 * POOL_MAX


def _silu(x):
    hx = 0.5 * x
    return hx + hx * jnp.tanh(hx)


def _sigmoid(x):
    return 0.5 + 0.5 * jnp.tanh(0.5 * x)


def _dot(a, b):
    return jnp.dot(a, b, preferred_element_type=F32)


def _dot_nt(a, b):
    return lax.dot_general(a, b, (((1,), (1,)), ((), ())), preferred_element_type=F32)


def _split3(a):
    a1 = a.astype(BF16)
    r1 = a - a1.astype(F32)
    a2 = r1.astype(BF16)
    a3 = (r1 - a2.astype(F32)).astype(BF16)
    return a1, a2, a3


def _dot_exact(parts, b):
    out = _dot(parts[0], b)
    for p in parts[1:]:
        out = out + _dot(p, b)
    return out


def _dot_exact_l(a01, x):
    parts = _split3(x)
    out = _dot(a01, parts[0])
    for p in parts[1:]:
        out = out + _dot(a01, p)
    return out


def _iota2(shape, dim):
    return lax.broadcasted_iota(jnp.int32, shape, dim)


def _div_pow2(x, d):
    assert d & (d - 1) == 0
    return jnp.right_shift(x, d.bit_length() - 1)


def _pad_rows(x, rows):
    if x.shape[0] == rows:
        return x
    return jnp.concatenate([x, jnp.zeros((rows - x.shape[0], x.shape[1]), x.dtype)], axis=0)


def _prep_kernel(a_ref, d_ref, o_ref, dt_ref, *, sub):
    @pl.when(pl.program_id(0) == 0)
    def _():
        row = _iota2((LANES, D_MODEL), 0)
        dt_ref[...] = jnp.where(row < HEADS, d_ref[...], 0.0).astype(BF16)

    def body(r, carry):
        rows = pl.ds(pl.multiple_of(r * sub, sub), sub)
        o_ref[rows, :] = a_ref[rows, :].astype(BF16)
        return carry

    lax.fori_loop(0, PREP_TN // sub, body, 0)


def _prep_weights(w_t):
    def a_map(c):
        src = jnp.where(c < 2, c * PREP_TN,
                        jnp.where(c < 10, IN_ZP + (c - 2) * PREP_TN, IN_XBC + (c - 10) * PREP_TN))
        return (pl.multiple_of(src, 32), 0)

    return pl.pallas_call(
        functools.partial(_prep_kernel, sub=256),
        grid=(N_MAIN // PREP_TN,),
        in_specs=[
            pl.BlockSpec((pl.Element(PREP_TN), pl.Element(D_MODEL)), a_map),
            pl.BlockSpec((pl.Element(LANES), pl.Element(D_MODEL)), lambda c: (IN_DT, 0)),
        ],
        out_specs=[
            pl.BlockSpec((PREP_TN, D_MODEL), lambda c: (c, 0)),
            pl.BlockSpec((LANES, D_MODEL), lambda c: (0, 0)),
        ],
        out_shape=[
            jax.ShapeDtypeStruct((N_MAIN, D_MODEL), BF16),
            jax.ShapeDtypeStruct((LANES, D_MODEL), BF16),
        ],
        compiler_params=pltpu.CompilerParams(
            dimension_semantics=("arbitrary",), vmem_limit_bytes=VMEM_LIMIT),
        name="prep",
    )(w_t, w_t)


def _proj_kernel(x_ref, nw_ref, w_ref, wdt_ref, o_ref, dt_ref, xn_ref, *, sub):
    tm = x_ref.shape[0]

    @pl.when(pl.program_id(1) == 0)
    def _():
        def body(r, carry):
            rows = pl.ds(pl.multiple_of(r * sub, sub), sub)
            x = x_ref[rows, :]
            ms = jnp.mean(x * x, axis=-1, keepdims=True)
            xn = (x * lax.rsqrt(ms + EPS)) * nw_ref[...]
            xn_ref[rows, :] = xn.astype(BF16)
            return carry

        lax.fori_loop(0, tm // sub, body, 0)
        dt_ref[...] = _dot_nt(xn_ref[...], wdt_ref[...])

    o_ref[...] = _dot_nt(xn_ref[...], w_ref[...])


def _proj(x, norm_w, w_main, w_dt, *, tm, tn=1024):
    m = x.shape[0]
    assert m % tm == 0 and N_MAIN % tn == 0
    sub = 16 if tm % 128 else 128
    return pl.pallas_call(
        functools.partial(_proj_kernel, sub=sub),
        grid=(m // tm, N_MAIN // tn),
        in_specs=[
            pl.BlockSpec((tm, D_MODEL), lambda i, j: (i, 0)),
            pl.BlockSpec((1, D_MODEL), lambda i, j: (0, 0)),
            pl.BlockSpec((tn, D_MODEL), lambda i, j: (j, 0)),
            pl.BlockSpec((LANES, D_MODEL), lambda i, j: (0, 0)),
        ],
        out_specs=[
            pl.BlockSpec((tm, tn), lambda i, j: (i, j)),
            pl.BlockSpec((tm, LANES), lambda i, j: (i, 0)),
        ],
        out_shape=[
            jax.ShapeDtypeStruct((m, N_MAIN), F32),
            jax.ShapeDtypeStruct((m, LANES), F32),
        ],
        scratch_shapes=[pltpu.VMEM((tm, D_MODEL), BF16)],
        compiler_params=pltpu.CompilerParams(
            dimension_semantics=("arbitrary", "arbitrary"),
            vmem_limit_bytes=VMEM_LIMIT),
        name="proj",
    )(x, norm_w, w_main, w_dt)


def _conv_silu(ext_ref, first, rows, cw_ref, cbias_ref, act_ref, row0):
    for cs in range(CONV_DIM // GROUP_W):
        cols = slice(cs * GROUP_W, (cs + 1) * GROUP_W)
        acc = cbias_ref[:, cols] + ext_ref[first:first + rows, cols] * cw_ref[0:1, cols]
        for k in range(1, CONV_K):
            acc = acc + ext_ref[first + k:first + k + rows, cols] * cw_ref[k:k + 1, cols]
        act_ref[row0:row0 + rows, cols] = _silu(acc)


def _dt_terms(dt_ref, dtb_ref, alog_ref):
    x = dt_ref[...] + dtb_ref[...]
    dtv = jnp.maximum(x, 0.0) + jnp.log1p(jnp.exp(-jnp.abs(x)))
    da = dtv * (-jnp.exp(alog_ref[...]))
    return dtv, da


def _gate_norm_store(y_g, zs, nw, y_ref, cols):
    y_g = y_g * _silu(zs)
    ms = jnp.mean(y_g * y_g, axis=-1, keepdims=True)
    y_ref[:, cols] = (y_g * lax.rsqrt(ms + EPS) * nw).astype(BF16)


def _pool_prompt(uext_ref, l1_ref, l2_ref, l3_ref, pooled_ref, rows):
    base, end = PU_BASE, PU_BASE + rows
    for gi, w in enumerate(POOL_WINDOWS):
        cols = slice(gi * POOL_GW, (gi + 1) * POOL_GW)
        levels = w.bit_length() - 1
        lo = base - 8 * (levels - 1)
        if levels == 1:
            s = uext_ref[base:end, cols] + uext_ref[base - 1:end - 1, cols]
        else:
            l1_ref[lo:end, :] = uext_ref[lo:end, cols] + uext_ref[lo - 1:end - 1, cols]
            if levels == 2:
                s = l1_ref[base:end, :] + l1_ref[base - 2:end - 2, :]
            else:
                lo2 = lo + 8
                l2_ref[lo2:end, :] = l1_ref[lo2:end, :] + l1_ref[lo2 - 2:end - 2, :]
                if levels == 3:
                    s = l2_ref[base:end, :] + l2_ref[base - 4:end - 4, :]
                else:
                    lo3 = lo2 + 8
                    l3_ref[lo3:end, :] = l2_ref[lo3:end, :] + l2_ref[lo3 - 4:end - 4, :]
                    s = l3_ref[base:end, :] + l3_ref[base - 8:end - 8, :]
        pooled_ref[:, cols] = (s * (1.0 / w) - uext_ref[base:end, cols]).astype(BF16)


def _seq_prompt_kernel(zs_ref, u_ref, xs_ref, bc_ref, dt_ref, h0_ref, ct0_ref, pt0_ref, e64_ref,
                       cw_ref, cbias_ref, dtb_ref, alog_ref, dskip_ref, nw_ref,
                       y_ref, pooled_ref, hfin_ref, ctail_ref, ptail_ref,
                       ht_ref, xext_ref, uext_ref, act_ref, l1_ref, l2_ref, l3_ref, *, with_y):
    L = xs_ref.shape[0]
    c = pl.program_id(1)
    nc = pl.num_programs(1)

    @pl.when(c == 0)
    def _():
        ht_ref[...] = h0_ref[...]
        xext_ref[0:8, :] = ct0_ref[...]
        uext_ref[0:POOL_MAX, :] = jnp.zeros((POOL_MAX, D_MODEL), F32)
        uext_ref[POOL_MAX:PU_BASE, :] = pt0_ref[...]

    xext_ref[8:8 + L, 0:INNER] = xs_ref[...]
    xext_ref[8:8 + L, INNER:CONV_DIM] = bc_ref[...]
    _conv_silu(xext_ref, 8 - (CONV_K - 1), L, cw_ref, cbias_ref, act_ref, 0)
    xext_ref[0:8, :] = xext_ref[L:L + 8, :]

    r_i = _iota2((L, L), 0)
    c_i = _iota2((L, L), 1)
    causal = c_i <= r_i
    tri = jnp.where(causal, 1.0, 0.0).astype(BF16)
    dtv, da = _dt_terms(dt_ref, dtb_ref, alog_ref)
    acum = _dot_exact_l(tri, da)
    stacked = _split3(jnp.concatenate([acum, dtv], axis=0))
    if with_y:
        acum_t = acum.T
        lane = _iota2((L, LANES), 1)

    for g in range(GROUPS):
        cols = slice(g * GROUP_W, (g + 1) * GROUP_W)
        ex = _dot_exact(stacked, e64_ref[:, cols])
        acx = ex[0:L, :]
        dtx = ex[L:2 * L, :]
        xs_g = act_ref[:, cols]
        b_g = act_ref[:, INNER + g * STATE:INNER + (g + 1) * STATE]
        xdt = xs_g * dtx
        aend = acx[L - 1:L, :]
        xdtw = _pad_rows(xdt * jnp.exp(aend - acx), LANES).astype(BF16)
        b_t = _pad_rows(b_g, LANES).T
        h_old = ht_ref[:, cols]
        ht_ref[:, cols] = h_old * jnp.exp(aend) + _dot(b_t.astype(BF16), xdtw)

        if with_y:
            c_g = act_ref[:, INNER + GROUPS * STATE + g * STATE:
                          INNER + GROUPS * STATE + (g + 1) * STATE].astype(BF16)
            cb = _dot_nt(c_g, b_g.astype(BF16))
            yoff = _dot(c_g, h_old.astype(BF16)) * jnp.exp(acx)
            yds = []
            for q in range(GROUP_W // LANES):
                ms_ = []
                for h in (g * 8 + 2 * q, g * 8 + 2 * q + 1):
                    seg = acum[:, h:h + 1] - acum_t[h:h + 1, :]
                    dec = jnp.where(causal, jnp.exp(seg), 0.0)
                    ms_.append((cb * dec).astype(BF16))
                m_cat = jnp.concatenate(ms_, axis=1)
                xp = xdt[:, q * LANES:(q + 1) * LANES]
                r_m = jnp.concatenate([jnp.where(lane < HEADDIM, xp, 0.0),
                                       jnp.where(lane >= HEADDIM, xp, 0.0)], axis=0).astype(BF16)
                yds.append(_dot(m_cat, r_m))
            y_g = jnp.concatenate(yds, axis=1) + yoff + xs_g * dskip_ref[:, cols]
            _gate_norm_store(y_g, zs_ref[:, cols], nw_ref[:, cols], y_ref, cols)

    if with_y:
        uext_ref[PU_BASE:PU_BASE + L, :] = u_ref[...]
        _pool_prompt(uext_ref, l1_ref, l2_ref, l3_ref, pooled_ref, L)
        uext_ref[POOL_MAX:PU_BASE, :] = uext_ref[L + POOL_MAX:L + PU_BASE, :]

        @pl.when(c == nc - 1)
        def _():
            for k in range(INNER // LANES):
                hfin_ref[k * LANES:(k + 1) * LANES, :] = ht_ref[:, k * LANES:(k + 1) * LANES].T
            ctail_ref[...] = xext_ref[0:8, :]
            ptail_ref[...] = uext_ref[POOL_MAX:PU_BASE, :]
    else:
        hfin_ref[...] = ht_ref[...]
        ctail_ref[...] = xext_ref[0:8, :]
        ptail_ref[...] = u_ref[...]
        y_ref[...] = jnp.zeros(y_ref.shape, y_ref.dtype)
        pooled_ref[...] = jnp.zeros(pooled_ref.shape, pooled_ref.dtype)


def _param_specs(index_map):
    return [
        pl.BlockSpec((CONV_K, CONV_DIM), index_map),
        pl.BlockSpec((1, CONV_DIM), index_map),
        pl.BlockSpec((1, LANES), index_map),
        pl.BlockSpec((1, LANES), index_map),
        pl.BlockSpec((1, INNER), index_map),
        pl.BlockSpec((1, INNER), index_map),
    ]


def _seq_scratch(rows):
    return [
        pltpu.VMEM((STATE, INNER), F32),
        pltpu.VMEM((8 + rows, CONV_DIM), F32),
        pltpu.VMEM((PU_BASE + rows, D_MODEL), F32),
        pltpu.VMEM((rows, CONV_DIM), F32),
        pltpu.VMEM((PU_BASE + rows, POOL_GW), F32),
        pltpu.VMEM((PU_BASE + rows, POOL_GW), F32),
        pltpu.VMEM((PU_BASE + rows, POOL_GW), F32),
    ]


def _seq_prompt(proj, dt, h0t, ct0, pt0, e64, params, *, batch, seq):
    nc = seq // CHUNK
    row = lambda b, c: b * nc + c
    const = lambda b, c: (0, 0)
    in_specs = [
        pl.BlockSpec((CHUNK, INNER), lambda b, c: (row(b, c), CB_ZS)),
        pl.BlockSpec((CHUNK, INNER), lambda b, c: (row(b, c), CB_U)),
        pl.BlockSpec((CHUNK, INNER), lambda b, c: (row(b, c), CB_XS)),
        pl.BlockSpec((CHUNK, 2 * GROUPS * STATE), lambda b, c: (row(b, c), CB_BC)),
        pl.BlockSpec((CHUNK, LANES), lambda b, c: (row(b, c), 0)),
        pl.BlockSpec((STATE, INNER), const),
        pl.BlockSpec((8, CONV_DIM), const),
        pl.BlockSpec((POOL_MAX, D_MODEL), const),
        pl.BlockSpec((LANES, INNER), const),
    ] + _param_specs(const)
    return pl.pallas_call(
        functools.partial(_seq_prompt_kernel, with_y=True),
        grid=(batch, nc),
        in_specs=in_specs,
        out_specs=[
            pl.BlockSpec((CHUNK, INNER), lambda b, c: (row(b, c), 0)),
            pl.BlockSpec((CHUNK, D_MODEL), lambda b, c: (row(b, c), 0)),
            pl.BlockSpec((None, INNER, STATE), lambda b, c: (b, 0, 0)),
            pl.BlockSpec((None, 8, CONV_DIM), lambda b, c: (b, 0, 0)),
            pl.BlockSpec((None, POOL_MAX, D_MODEL), lambda b, c: (b, 0, 0)),
        ],
        out_shape=[
            jax.ShapeDtypeStruct((batch * seq, INNER), BF16),
            jax.ShapeDtypeStruct((batch * seq, D_MODEL), BF16),
            jax.ShapeDtypeStruct((batch, INNER, STATE), F32),
            jax.ShapeDtypeStruct((batch, 8, CONV_DIM), F32),
            jax.ShapeDtypeStruct((batch, POOL_MAX, D_MODEL), F32),
        ],
        scratch_shapes=_seq_scratch(CHUNK),
        compiler_params=pltpu.CompilerParams(
            dimension_semantics=("arbitrary", "arbitrary"),
            vmem_limit_bytes=VMEM_LIMIT),
        name="seq_prompt",
    )(proj, proj, proj, proj, dt, h0t, ct0, pt0, e64, *params)


def _meta_state(proj, dt, e64, params, *, row_block):
    const = lambda b, c: (0, 0)
    zeros = functools.partial(jnp.zeros, dtype=F32)
    in_specs = [
        pl.BlockSpec((N_META, INNER), lambda b, c: (row_block, CB_ZS)),
        pl.BlockSpec((N_META, INNER), lambda b, c: (row_block, CB_U)),
        pl.BlockSpec((N_META, INNER), lambda b, c: (row_block, CB_XS)),
        pl.BlockSpec((N_META, 2 * GROUPS * STATE), lambda b, c: (row_block, CB_BC)),
        pl.BlockSpec((N_META, LANES), lambda b, c: (row_block, 0)),
        pl.BlockSpec((STATE, INNER), const),
        pl.BlockSpec((8, CONV_DIM), const),
        pl.BlockSpec((POOL_MAX, D_MODEL), const),
        pl.BlockSpec((LANES, INNER), const),
    ] + _param_specs(const)
    outs = pl.pallas_call(
        functools.partial(_seq_prompt_kernel, with_y=False),
        grid=(1, 1),
        in_specs=in_specs,
        out_specs=[
            pl.BlockSpec((N_META, INNER), const),
            pl.BlockSpec((N_META, D_MODEL), const),
            pl.BlockSpec((STATE, INNER), const),
            pl.BlockSpec((8, CONV_DIM), const),
            pl.BlockSpec((POOL_MAX, D_MODEL), const),
        ],
        out_shape=[
            jax.ShapeDtypeStruct((N_META, INNER), BF16),
            jax.ShapeDtypeStruct((N_META, D_MODEL), BF16),
            jax.ShapeDtypeStruct((STATE, INNER), F32),
            jax.ShapeDtypeStruct((8, CONV_DIM), F32),
            jax.ShapeDtypeStruct((POOL_MAX, D_MODEL), F32),
        ],
        scratch_shapes=_seq_scratch(N_META),
        compiler_params=pltpu.CompilerParams(
            dimension_semantics=("arbitrary", "arbitrary"),
            vmem_limit_bytes=VMEM_LIMIT),
        name="meta_state",
    )(proj, proj, proj, proj, dt, zeros((STATE, INNER)), zeros((8, CONV_DIM)),
      zeros((POOL_MAX, D_MODEL)), e64, *params)
    return outs[2], outs[3], outs[4]


def _seq_sample_kernel(zs_ref, u_ref, xs_ref, bc_ref, dt_ref, chist_ref, phist_ref, h0_ref, e64_ref,
                       cw_ref, cbias_ref, dtb_ref, alog_ref, dskip_ref, nw_ref,
                       y_ref, pooled_ref, hnew_ref, cnew_ref, pnew_ref,
                       xext_ref, uext_ref, act_ref, dv_ref, s_ref):
    R = xs_ref.shape[0]
    T = R // SEQ_BLOCK
    SB = SEQ_BLOCK
    HIST = CONV_K - 1
    PH = POOL_MAX - 1
    half = R

    for k in range(HIST):
        xext_ref[k * SB:(k + 1) * SB, :] = chist_ref[k]
    xext_ref[HIST * SB:HIST * SB + R, 0:INNER] = xs_ref[...]
    xext_ref[HIST * SB:HIST * SB + R, INNER:CONV_DIM] = bc_ref[...]
    for cs in range(CONV_DIM // GROUP_W):
        cols = slice(cs * GROUP_W, (cs + 1) * GROUP_W)
        acc = cbias_ref[:, cols] + xext_ref[0:R, cols] * cw_ref[0:1, cols]
        for k in range(1, CONV_K):
            acc = acc + xext_ref[k * SB:k * SB + R, cols] * cw_ref[k:k + 1, cols]
        act_ref[:, cols] = _silu(acc)
    for k in range(HIST):
        cnew_ref[k] = xext_ref[R + k * SB:R + (k + 1) * SB, :]

    seq_of = lambda x: jnp.bitwise_and(x, SB - 1)
    step_of = lambda x: _div_pow2(x, SB)
    r_i = _iota2((R, R), 0)
    c_i = _iota2((R, R), 1)
    same = seq_of(r_i) == seq_of(c_i)
    tri = jnp.where(same & (step_of(c_i) <= step_of(r_i)), 1.0, 0.0).astype(BF16)
    ones_bd = jnp.where(same, 1.0, 0.0).astype(BF16)
    dtv, da = _dt_terms(dt_ref, dtb_ref, alog_ref)
    da_parts = _split3(da)
    acum = _dot(tri, da_parts[0]) + _dot(tri, da_parts[1]) + _dot(tri, da_parts[2])
    a_end = (_dot(ones_bd, da_parts[0]) + _dot(ones_bd, da_parts[1])
             + _dot(ones_bd, da_parts[2]))

    acum_t2 = jnp.concatenate([acum, acum], axis=0).T
    aend_t = _pad_rows(a_end, LANES).T

    r_s = _iota2((LANES, SB * STATE), 0)
    c_s = _iota2((LANES, SB * STATE), 1)
    sel = jnp.where(r_s == _div_pow2(c_s, STATE), 1.0, 0.0).astype(BF16)
    dv_ref[...] = jnp.exp(_dot_exact(_split3(aend_t), sel))

    lane2 = _iota2((R, 2 * R), 1)
    r_2 = _iota2((R, 2 * R), 0)
    c_2 = jnp.bitwise_and(lane2, R - 1)
    mask2 = (seq_of(r_2) == seq_of(c_2)) & (step_of(c_2) <= step_of(r_2))
    first_half = lane2 < half
    lane = _iota2((R, LANES), 1)
    r_b = _iota2((R, SB * STATE), 0)
    c_b = _iota2((R, SB * STATE), 1)
    bd_sel = seq_of(r_b) == _div_pow2(c_b, STATE)
    stacked = _split3(jnp.concatenate([acum, dtv, a_end], axis=0))

    for g in range(GROUPS):
        cols = slice(g * GROUP_W, (g + 1) * GROUP_W)
        ex = _dot_exact(stacked, e64_ref[:, cols])
        acx = ex[0:R, :]
        dtx = ex[R:2 * R, :]
        aendx = ex[2 * R:3 * R, :]
        xs_g = act_ref[:, cols]
        b_g = act_ref[:, INNER + g * STATE:INNER + (g + 1) * STATE]
        c_f = act_ref[:, INNER + GROUPS * STATE + g * STATE:INNER + GROUPS * STATE + (g + 1) * STATE]
        c_g = c_f.astype(BF16)
        xdt = xs_g * dtx
        xdtw = xdt * jnp.exp(aendx - acx)

        c_big = jnp.where(bd_sel, jnp.concatenate([c_f] * SB, axis=1), 0.0).astype(BF16)
        h_cat = jnp.concatenate(
            [h0_ref[i, g * GROUP_W:(g + 1) * GROUP_W, :].astype(BF16) for i in range(SB)], axis=1)
        yoff = _dot_nt(c_big, h_cat)

        xt = _pad_rows(xdtw, LANES).T
        b_big = jnp.where(bd_sel, jnp.concatenate([b_g] * SB, axis=1), 0.0)
        b_big = _pad_rows(b_big, LANES)
        s_ref[...] = _dot(xt.astype(BF16), b_big.astype(BF16))
        for i in range(SB):
            for hh in range(GROUP_W // HEADDIM):
                h = g * (GROUP_W // HEADDIM) + hh
                rows = slice(h * HEADDIM, (h + 1) * HEADDIM)
                hnew_ref[i, rows, :] = (
                    h0_ref[i, rows, :] * dv_ref[h:h + 1, i * STATE:(i + 1) * STATE]
                    + s_ref[hh * HEADDIM:(hh + 1) * HEADDIM, i * STATE:(i + 1) * STATE])

        cb2 = _dot_nt(c_g, jnp.concatenate([b_g, b_g], axis=0).astype(BF16))
        yds = []
        for q in range(GROUP_W // LANES):
            ha = g * 8 + 2 * q
            colc = jnp.where(first_half, acum[:, ha:ha + 1], acum[:, ha + 1:ha + 2])
            rowc = jnp.where(first_half[0:1, :], acum_t2[ha:ha + 1, :], acum_t2[ha + 1:ha + 2, :])
            dec = jnp.where(mask2, jnp.exp(colc - rowc), 0.0)
            m_p = (cb2 * dec).astype(BF16)
            xp = xdt[:, q * LANES:(q + 1) * LANES]
            r_m = jnp.concatenate([jnp.where(lane < HEADDIM, xp, 0.0),
                                   jnp.where(lane >= HEADDIM, xp, 0.0)], axis=0).astype(BF16)
            yds.append(_dot(m_p, r_m))
        y_g = jnp.concatenate(yds, axis=1) + yoff * jnp.exp(acx) + xs_g * dskip_ref[:, cols]
        _gate_norm_store(y_g, zs_ref[:, cols], nw_ref[:, cols], y_ref, cols)

    for k in range(PH):
        uext_ref[k * SB:(k + 1) * SB, :] = phist_ref[k]
    uext_ref[PH * SB:PH * SB + R, :] = u_ref[...]
    for k in range(PH):
        pnew_ref[k] = uext_ref[(T + k) * SB:(T + k + 1) * SB, :]
    for gi, w in enumerate(POOL_WINDOWS):
        cols = slice(gi * POOL_GW, (gi + 1) * POOL_GW)
        u_cur = u_ref[:, cols]
        s = u_cur
        for k in range(1, w):
            s = s + uext_ref[(PH - k) * SB:(PH - k) * SB + R, cols]
        pooled_ref[:, cols] = (s * (1.0 / w) - u_cur).astype(BF16)


def _seq_sample(proj, dt, conv_hist, pool_hist, h0, e64, params, *, nseq, t_new):
    rb = SEQ_BLOCK * t_new
    assert 2 * rb == LANES
    const = lambda s: (0, 0)
    in_specs = [
        pl.BlockSpec((rb, INNER), lambda s: (s, CB_ZS)),
        pl.BlockSpec((rb, INNER), lambda s: (s, CB_U)),
        pl.BlockSpec((rb, INNER), lambda s: (s, CB_XS)),
        pl.BlockSpec((rb, 2 * GROUPS * STATE), lambda s: (s, CB_BC)),
        pl.BlockSpec((rb, LANES), lambda s: (s, 0)),
        pl.BlockSpec((CONV_K - 1, SEQ_BLOCK, CONV_DIM), lambda s: (0, s, 0)),
        pl.BlockSpec((POOL_MAX - 1, SEQ_BLOCK, D_MODEL), lambda s: (0, s, 0)),
        pl.BlockSpec((SEQ_BLOCK, INNER, STATE), lambda s: (s, 0, 0)),
        pl.BlockSpec((LANES, INNER), const),
    ] + _param_specs(const)
    return pl.pallas_call(
        _seq_sample_kernel,
        grid=(nseq // SEQ_BLOCK,),
        in_specs=in_specs,
        out_specs=[
            pl.BlockSpec((rb, INNER), lambda s: (s, 0)),
            pl.BlockSpec((rb, D_MODEL), lambda s: (s, 0)),
            pl.BlockSpec((SEQ_BLOCK, INNER, STATE), lambda s: (s, 0, 0)),
            pl.BlockSpec((CONV_K - 1, SEQ_BLOCK, CONV_DIM), lambda s: (0, s, 0)),
            pl.BlockSpec((POOL_MAX - 1, SEQ_BLOCK, D_MODEL), lambda s: (0, s, 0)),
        ],
        out_shape=[
            jax.ShapeDtypeStruct((nseq * t_new, INNER), BF16),
            jax.ShapeDtypeStruct((nseq * t_new, D_MODEL), BF16),
            jax.ShapeDtypeStruct((nseq, INNER, STATE), F32),
            jax.ShapeDtypeStruct((CONV_K - 1, nseq, CONV_DIM), F32),
            jax.ShapeDtypeStruct((POOL_MAX - 1, nseq, D_MODEL), F32),
        ],
        scratch_shapes=[
            pltpu.VMEM((SEQ_BLOCK * (CONV_K - 1 + t_new), CONV_DIM), F32),
            pltpu.VMEM((SEQ_BLOCK * (POOL_MAX - 1 + t_new), D_MODEL), F32),
            pltpu.VMEM((rb, CONV_DIM), F32),
            pltpu.VMEM((LANES, SEQ_BLOCK * STATE), F32),
            pltpu.VMEM((GROUP_W, SEQ_BLOCK * STATE), F32),
        ],
        compiler_params=pltpu.CompilerParams(
            dimension_semantics=("arbitrary",),
            vmem_limit_bytes=VMEM_LIMIT),
        name="seq_sample",
    )(proj, proj, proj, proj, dt, conv_hist, pool_hist, h0, e64, *params)


def _merge_kernel(y_ref, pooled_ref, zp_ref, g1_ref, g2_ref, wmix_ref, bmix_ref, pscale_ref,
                  wps_ref, wpp_ref, o_ref, pout_ref):
    for g in range(len(POOL_WINDOWS)):
        cols = slice(g * POOL_GW, (g + 1) * POOL_GW)
        mixed = _dot(pooled_ref[:, cols], wmix_ref[g]) + bmix_ref[:, cols]
        pout_ref[:, cols] = (mixed * pscale_ref[:, cols] * _silu(zp_ref[:, cols])).astype(BF16)
    bs = _dot(y_ref[...], wps_ref[...])
    bp = _dot(pout_ref[...], wpp_ref[...])
    o_ref[...] = (_sigmoid(g1_ref[...]) * bs + _sigmoid(g2_ref[...]) * bp).astype(BF16)


def _merge(y, pooled, proj, wmix, bmix, pscale, wps, wpp, *, m, tm=256):
    const2 = lambda i: (0, 0)
    resident = dict(pipeline_mode=pl.Buffered(1))
    return pl.pallas_call(
        _merge_kernel,
        grid=(m // tm,),
        in_specs=[
            pl.BlockSpec((tm, INNER), lambda i: (i, 0)),
            pl.BlockSpec((tm, D_MODEL), lambda i: (i, 0)),
            pl.BlockSpec((tm, D_MODEL), lambda i: (i, CB_ZP)),
            pl.BlockSpec((tm, D_MODEL), lambda i: (i, CB_G1)),
            pl.BlockSpec((tm, D_MODEL), lambda i: (i, CB_G2)),
            pl.BlockSpec((len(POOL_WINDOWS), POOL_GW, POOL_GW), lambda i: (0, 0, 0), **resident),
            pl.BlockSpec((1, D_MODEL), const2),
            pl.BlockSpec((1, D_MODEL), const2),
            pl.BlockSpec((INNER, D_MODEL), const2, **resident),
            pl.BlockSpec((D_MODEL, D_MODEL), const2, **resident),
        ],
        out_specs=pl.BlockSpec((tm, D_MODEL), lambda i: (i, 0)),
        out_shape=jax.ShapeDtypeStruct((m, D_MODEL), BF16),
        scratch_shapes=[pltpu.VMEM((tm, D_MODEL), BF16)],
        compiler_params=pltpu.CompilerParams(
            dimension_semantics=("arbitrary",),
            vmem_limit_bytes=VMEM_LIMIT),
        name="merge",
    )(y, pooled, proj, proj, proj, wmix, bmix, pscale, wps, wpp)


def _out_kernel(m_ref, h_ref, wout_ref, fw_ref, o_ref):
    hn = h_ref[...] + _dot(m_ref[...], wout_ref[...])
    ms = jnp.mean(hn * hn, axis=-1, keepdims=True)
    o_ref[...] = (hn * lax.rsqrt(ms + EPS)) * fw_ref[...]


def _out(merged, h, wout, fw, *, m, tm=512):
    return pl.pallas_call(
        _out_kernel,
        grid=(m // tm,),
        in_specs=[
            pl.BlockSpec((tm, D_MODEL), lambda i: (i, 0)),
            pl.BlockSpec((tm, D_MODEL), lambda i: (i, 0)),
            pl.BlockSpec((D_MODEL, D_MODEL), lambda i: (0, 0)),
            pl.BlockSpec((1, D_MODEL), lambda i: (0, 0)),
        ],
        out_specs=pl.BlockSpec((tm, D_MODEL), lambda i: (i, 0)),
        out_shape=jax.ShapeDtypeStruct((m, D_MODEL), F32),
        compiler_params=pltpu.CompilerParams(
            dimension_semantics=("arbitrary",),
            vmem_limit_bytes=VMEM_LIMIT),
        name="out",
    )(merged, h, wout, fw)


def kernel(x_prompt, x_sample, state_conv, state_ssm, state_pool, meta_tokens, norm_w, w_in,
           conv_w, conv_b, dt_bias, a_log, d_skip, ssd_norm_w, w_proj_ssd, pool_mix_w,
           pool_mix_b, pool_scale, w_proj_pool, w_out, final_norm_w):
    batch, seq, _ = x_prompt.shape
    nseq, t_new, _ = x_sample.shape
    assert norm_w.shape[0] == 1, "single layer"
    assert w_in.shape == (1, D_MODEL, IN_ZP + 4 * D_MODEL)

    w_main, w_dt = _prep_weights(jnp.swapaxes(w_in, 1, 2)[0])
    wps = w_proj_ssd[0].astype(BF16)
    wpp = w_proj_pool[0].astype(BF16)
    wout = w_out[0].astype(BF16)
    wmix = pool_mix_w[0].astype(BF16)
    lane_pad = lambda v: jnp.pad(v.reshape(1, HEADS), ((0, 0), (0, LANES - HEADS)))
    params = (conv_w[0], conv_b[0].reshape(1, CONV_DIM), lane_pad(dt_bias[0]), lane_pad(a_log[0]),
              jnp.repeat(d_skip[0], HEADDIM).reshape(1, INNER), ssd_norm_w[0].reshape(1, INNER))
    e64 = (jnp.arange(LANES)[:, None] == (jnp.arange(INNER)[None, :] // HEADDIM)).astype(BF16)
    nw = norm_w[0].reshape(1, D_MODEL)
    fw = final_norm_w.reshape(1, D_MODEL)
    bmix = pool_mix_b[0].reshape(1, D_MODEL)
    pscale = pool_scale[0].reshape(1, D_MODEL)

    xp = x_prompt.reshape(batch * seq, D_MODEL)
    m_s = nseq * t_new
    nblk = nseq // SEQ_BLOCK
    xs_tm = x_sample.reshape(nblk, SEQ_BLOCK, t_new, D_MODEL).transpose(0, 2, 1, 3)
    x_sm = jnp.concatenate([xs_tm.reshape(m_s, D_MODEL), meta_tokens], axis=0)

    proj_p, dt_p = _proj(xp, nw, w_main, w_dt, tm=1024)
    proj_s, dt_s = _proj(x_sm, nw, w_main, w_dt, tm=m_s + N_META)

    h0t, ct0, pt0 = _meta_state(proj_s, dt_s, e64, params, row_block=m_s // N_META)

    y_p, pooled_p, ssm_p, conv_p, pool_p = _seq_prompt(
        proj_p, dt_p, h0t, ct0, pt0, e64, params, batch=batch, seq=seq)
    y_s, pooled_s, ssm_s, conv_s, pool_s = _seq_sample(
        proj_s, dt_s, jnp.swapaxes(state_conv[0], 0, 1), jnp.swapaxes(state_pool[0], 0, 1),
        state_ssm[0].reshape(nseq, INNER, STATE), e64, params, nseq=nseq, t_new=t_new)

    merged_p = _merge(y_p, pooled_p, proj_p, wmix, bmix, pscale, wps, wpp, m=batch * seq)
    merged_s = _merge(y_s, pooled_s, proj_s, wmix, bmix, pscale, wps, wpp, m=m_s)
    out_p = _out(merged_p, xp, wout, fw, m=batch * seq)
    out_s = _out(merged_s, x_sm, wout, fw, m=m_s)

    return (out_p.reshape(batch, seq, D_MODEL),
            out_s.reshape(nblk, t_new, SEQ_BLOCK, D_MODEL).transpose(0, 2, 1, 3).reshape(
                nseq, t_new, D_MODEL),
            conv_p[:, 8 - (CONV_K - 1):][None].astype(state_conv.dtype),
            ssm_p.reshape(1, batch, HEADS, HEADDIM, STATE).astype(state_ssm.dtype),
            pool_p[:, 1:][None].astype(state_pool.dtype),
            jnp.swapaxes(conv_s, 0, 1)[None].astype(state_conv.dtype),
            ssm_s.reshape(1, nseq, HEADS, HEADDIM, STATE).astype(state_ssm.dtype),
            jnp.swapaxes(pool_s, 0, 1)[None].astype(state_pool.dtype))
```

```python
import functools

import jax
import jax.numpy as jnp
from jax import lax
from jax.experimental import pallas as pl
from jax.experimental.pallas import tpu as pltpu

F32 = jnp.float32
BF16 = jnp.bfloat16

D_MODEL = 2048
N_META = 16
EPS = 1e-6
HEADS = 32
HEADDIM = 64
GROUPS = 4
STATE = 128
CONV_K = 4
INNER = HEADS * HEADDIM
GROUP_W = INNER // GROUPS
CONV_DIM = INNER + 2 * GROUPS * STATE
POOL_WINDOWS = (2, 4, 8, 16)
POOL_MAX = 16
POOL_GW = D_MODEL // len(POOL_WINDOWS)

IN_XBC = INNER
IN_DT = IN_XBC + CONV_DIM
IN_ZP = IN_DT + HEADS
N_MAIN = 5 * D_MODEL + CONV_DIM
CB_ZS, CB_ZP, CB_U, CB_G1, CB_G2, CB_XS = 0, 1, 2, 3, 4, 5
CB_BC = 12

LANES = 128
CHUNK = 128
SEQ_BLOCK = 8
VMEM_LIMIT = 60 * 1024 * 1024
PREP_TN = 1024
PU_BASE = 2 * POOL_MAX
TPC = 16
TS = CHUNK // TPC


def _silu(x):
    hx = 0.5 * x
    return hx + hx * jnp.tanh(hx)


def _sigmoid(x):
    return 0.5 + 0.5 * jnp.tanh(0.5 * x)


def _dot(a, b):
    return jnp.dot(a, b, preferred_element_type=F32)


def _dot_nt(a, b):
    return lax.dot_general(a, b, (((1,), (1,)), ((), ())), preferred_element_type=F32)


def _split3(a):
    a1 = a.astype(BF16)
    r1 = a - a1.astype(F32)
    a2 = r1.astype(BF16)
    a3 = (r1 - a2.astype(F32)).astype(BF16)
    return a1, a2, a3


def _dot_exact(parts, b):
    out = _dot(parts[0], b)
    for p in parts[1:]:
        out = out + _dot(p, b)
    return out


def _dot_exact_l(a01, x):
    parts = _split3(x)
    out = _dot(a01, parts[0])
    for p in parts[1:]:
        out = out + _dot(a01, p)
    return out


def _iota2(shape, dim):
    return lax.broadcasted_iota(jnp.int32, shape, dim)


def _div_pow2(x, d):
    assert d & (d - 1) == 0
    return jnp.right_shift(x, d.bit_length() - 1)


def _pad_rows(x, rows):
    if x.shape[0] == rows:
        return x
    return jnp.concatenate([x, jnp.zeros((rows - x.shape[0], x.shape[1]), x.dtype)], axis=0)


def _prep_kernel(a_ref, d_ref, o_ref, dt_ref, *, sub):
    @pl.when(pl.program_id(0) == 0)
    def _():
        row = _iota2((LANES, D_MODEL), 0)
        dt_ref[...] = jnp.where(row < HEADS, d_ref[...], 0.0).astype(BF16)

    def body(r, carry):
        rows = pl.ds(pl.multiple_of(r * sub, sub), sub)
        o_ref[rows, :] = a_ref[rows, :].astype(BF16)
        return carry

    lax.fori_loop(0, PREP_TN // sub, body, 0)


def _prep_weights(w_t):
    def a_map(c):
        src = jnp.where(c < 2, c * PREP_TN,
                        jnp.where(c < 10, IN_ZP + (c - 2) * PREP_TN, IN_XBC + (c - 10) * PREP_TN))
        return (pl.multiple_of(src, 32), 0)

    return pl.pallas_call(
        functools.partial(_prep_kernel, sub=256),
        grid=(N_MAIN // PREP_TN,),
        in_specs=[
            pl.BlockSpec((pl.Element(PREP_TN), pl.Element(D_MODEL)), a_map),
            pl.BlockSpec((pl.Element(LANES), pl.Element(D_MODEL)), lambda c: (IN_DT, 0)),
        ],
        out_specs=[
            pl.BlockSpec((PREP_TN, D_MODEL), lambda c: (c, 0)),
            pl.BlockSpec((LANES, D_MODEL), lambda c: (0, 0)),
        ],
        out_shape=[
            jax.ShapeDtypeStruct((N_MAIN, D_MODEL), BF16),
            jax.ShapeDtypeStruct((LANES, D_MODEL), BF16),
        ],
        compiler_params=pltpu.CompilerParams(
            dimension_semantics=("arbitrary",), vmem_limit_bytes=VMEM_LIMIT),
        name="prep",
    )(w_t, w_t)


def _proj_kernel(x_ref, nw_ref, w_ref, wdt_ref, o_ref, dt_ref, xn_ref, *, sub):
    tm = x_ref.shape[0]

    @pl.when(pl.program_id(1) == 0)
    def _():
        def body(r, carry):
            rows = pl.ds(pl.multiple_of(r * sub, sub), sub)
            x = x_ref[rows, :]
            ms = jnp.mean(x * x, axis=-1, keepdims=True)
            xn = (x * lax.rsqrt(ms + EPS)) * nw_ref[...]
            xn_ref[rows, :] = xn.astype(BF16)
            return carry

        lax.fori_loop(0, tm // sub, body, 0)
        dt_ref[...] = _dot_nt(xn_ref[...], wdt_ref[...])

    o_ref[...] = _dot_nt(xn_ref[...], w_ref[...])


def _proj(x, norm_w, w_main, w_dt, *, tm, tn=1024):
    m = x.shape[0]
    assert m % tm == 0 and N_MAIN % tn == 0
    sub = 16 if tm % 128 else 128
    return pl.pallas_call(
        functools.partial(_proj_kernel, sub=sub),
        grid=(m // tm, N_MAIN // tn),
        in_specs=[
            pl.BlockSpec((tm, D_MODEL), lambda i, j: (i, 0)),
            pl.BlockSpec((1, D_MODEL), lambda i, j: (0, 0)),
            pl.BlockSpec((tn, D_MODEL), lambda i, j: (j, 0)),
            pl.BlockSpec((LANES, D_MODEL), lambda i, j: (0, 0)),
        ],
        out_specs=[
            pl.BlockSpec((tm, tn), lambda i, j: (i, j)),
            pl.BlockSpec((tm, LANES), lambda i, j: (i, 0)),
        ],
        out_shape=[
            jax.ShapeDtypeStruct((m, N_MAIN), F32),
            jax.ShapeDtypeStruct((m, LANES), F32),
        ],
        scratch_shapes=[pltpu.VMEM((tm, D_MODEL), BF16)],
        compiler_params=pltpu.CompilerParams(
            dimension_semantics=("arbitrary", "arbitrary"),
            vmem_limit_bytes=VMEM_LIMIT),
        name="proj",
    )(x, norm_w, w_main, w_dt)


def _proj_perm_kernel(*refs, n_chunk):
    x_refs = refs[:TPC]
    nw_ref, w_ref, wdt_ref, o_ref, dt_ref, xn_ref = refs[TPC:]

    @pl.when(pl.program_id(1) == 0)
    def _():
        def body(c, carry):
            base = pl.multiple_of(c * CHUNK, CHUNK)
            for jp in range(TPC // 2):
                x = jnp.concatenate([x_refs[2 * jp][c], x_refs[2 * jp + 1][c]], axis=0)
                ms = jnp.mean(x * x, axis=-1, keepdims=True)
                xn = (x * lax.rsqrt(ms + EPS)) * nw_ref[...]
                xn_ref[pl.ds(base + jp * 2 * TS, 2 * TS), :] = xn.astype(BF16)
            return carry

        lax.fori_loop(0, n_chunk, body, 0)
        dt_ref[...] = _dot_nt(xn_ref[...], wdt_ref[...])

    o_ref[...] = _dot_nt(xn_ref[...], w_ref[...])


def _proj_perm(x, norm_w, w_main, w_dt, *, tm=1024, tn=1024):
    m = x.shape[0]
    n_chunk = tm // CHUNK
    x3 = x.reshape(m // CHUNK, TS, TPC * D_MODEL)
    x_specs = [pl.BlockSpec((n_chunk, TS, D_MODEL), lambda i, j, jj=jj: (i, 0, jj)) for jj in range(TPC)]
    return pl.pallas_call(
        functools.partial(_proj_perm_kernel, n_chunk=n_chunk),
        grid=(m // tm, N_MAIN // tn),
        in_specs=x_specs + [
            pl.BlockSpec((1, D_MODEL), lambda i, j: (0, 0)),
            pl.BlockSpec((tn, D_MODEL), lambda i, j: (j, 0)),
            pl.BlockSpec((LANES, D_MODEL), lambda i, j: (0, 0)),
        ],
        out_specs=[
            pl.BlockSpec((tm, tn), lambda i, j: (i, j)),
            pl.BlockSpec((tm, LANES), lambda i, j: (i, 0)),
        ],
        out_shape=[
            jax.ShapeDtypeStruct((m, N_MAIN), F32),
            jax.ShapeDtypeStruct((m, LANES), F32),
        ],
        scratch_shapes=[pltpu.VMEM((tm, D_MODEL), BF16)],
        compiler_params=pltpu.CompilerParams(
            dimension_semantics=("arbitrary", "arbitrary"),
            vmem_limit_bytes=VMEM_LIMIT),
        name="proj_perm",
    )(*([x3] * TPC), norm_w, w_main, w_dt)


def _conv_silu(ext_ref, first, rows, cw_ref, cbias_ref, act_ref, row0):
    for cs in range(CONV_DIM // GROUP_W):
        cols = slice(cs * GROUP_W, (cs + 1) * GROUP_W)
        acc = cbias_ref[:, cols] + ext_ref[first:first + rows, cols] * cw_ref[0:1, cols]
        for k in range(1, CONV_K):
            acc = acc + ext_ref[first + k:first + k + rows, cols] * cw_ref[k:k + 1, cols]
        act_ref[row0:row0 + rows, cols] = _silu(acc)


def _dt_terms(dt_ref, dtb_ref, alog_ref):
    x = dt_ref[...] + dtb_ref[...]
    dtv = jnp.maximum(x, 0.0) + jnp.log1p(jnp.exp(-jnp.abs(x)))
    da = dtv * (-jnp.exp(alog_ref[...]))
    return dtv, da


def _gate_norm_store(y_g, zs, nw, y_ref, cols):
    y_g = y_g * _silu(zs)
    ms = jnp.mean(y_g * y_g, axis=-1, keepdims=True)
    y_ref[:, cols] = (y_g * lax.rsqrt(ms + EPS) * nw).astype(BF16)


def _pool_prompt(uext_ref, l1_ref, l2_ref, l3_ref, pooled_ref, rows):
    base, end = PU_BASE, PU_BASE + rows
    for gi, w in enumerate(POOL_WINDOWS):
        cols = slice(gi * POOL_GW, (gi + 1) * POOL_GW)
        levels = w.bit_length() - 1
        lo = base - 8 * (levels - 1)
        if levels == 1:
            s = uext_ref[base:end, cols] + uext_ref[base - 1:end - 1, cols]
        else:
            l1_ref[lo:end, :] = uext_ref[lo:end, cols] + uext_ref[lo - 1:end - 1, cols]
            if levels == 2:
                s = l1_ref[base:end, :] + l1_ref[base - 2:end - 2, :]
            else:
                lo2 = lo + 8
                l2_ref[lo2:end, :] = l1_ref[lo2:end, :] + l1_ref[lo2 - 2:end - 2, :]
                if levels == 3:
                    s = l2_ref[base:end, :] + l2_ref[base - 4:end - 4, :]
                else:
                    lo3 = lo2 + 8
                    l3_ref[lo3:end, :] = l2_ref[lo3:end, :] + l2_ref[lo3 - 4:end - 4, :]
                    s = l3_ref[base:end, :] + l3_ref[base - 8:end - 8, :]
        pooled_ref[:, cols] = (s * (1.0 / w) - uext_ref[base:end, cols]).astype(BF16)


def _seq_natural_kernel(zs_ref, u_ref, xs_ref, bc_ref, dt_ref, h0_ref, ct0_ref, pt0_ref, e64_ref,
                        cw_ref, cbias_ref, dtb_ref, alog_ref, dskip_ref, nw_ref,
                        y_ref, pooled_ref, hfin_ref, ctail_ref, ptail_ref,
                        ht_ref, xext_ref, uext_ref, act_ref, l1_ref, l2_ref, l3_ref, *, with_y):
    L = xs_ref.shape[0]
    c = pl.program_id(1)
    nc = pl.num_programs(1)

    @pl.when(c == 0)
    def _():
        ht_ref[...] = h0_ref[...]
        xext_ref[0:8, :] = ct0_ref[...]
        uext_ref[0:POOL_MAX, :] = jnp.zeros((POOL_MAX, D_MODEL), F32)
        uext_ref[POOL_MAX:PU_BASE, :] = pt0_ref[...]

    xext_ref[8:8 + L, 0:INNER] = xs_ref[...]
    xext_ref[8:8 + L, INNER:CONV_DIM] = bc_ref[...]
    _conv_silu(xext_ref, 8 - (CONV_K - 1), L, cw_ref, cbias_ref, act_ref, 0)
    xext_ref[0:8, :] = xext_ref[L:L + 8, :]

    r_i = _iota2((L, L), 0)
    c_i = _iota2((L, L), 1)
    causal = c_i <= r_i
    tri = jnp.where(causal, 1.0, 0.0).astype(BF16)
    dtv, da = _dt_terms(dt_ref, dtb_ref, alog_ref)
    acum = _dot_exact_l(tri, da)
    stacked = _split3(jnp.concatenate([acum, dtv], axis=0))
    if with_y:
        acum_t = acum.T
        lane = _iota2((L, LANES), 1)

    for g in range(GROUPS):
        cols = slice(g * GROUP_W, (g + 1) * GROUP_W)
        ex = _dot_exact(stacked, e64_ref[:, cols])
        acx = ex[0:L, :]
        dtx = ex[L:2 * L, :]
        xs_g = act_ref[:, cols]
        b_g = act_ref[:, INNER + g * STATE:INNER + (g + 1) * STATE]
        xdt = xs_g * dtx
        aend = acx[L - 1:L, :]
        xdtw = _pad_rows(xdt * jnp.exp(aend - acx), LANES).astype(BF16)
        b_t = _pad_rows(b_g, LANES).T
        h_old = ht_ref[:, cols]
        ht_ref[:, cols] = h_old * jnp.exp(aend) + _dot(b_t.astype(BF16), xdtw)

        if with_y:
            c_g = act_ref[:, INNER + GROUPS * STATE + g * STATE:
                          INNER + GROUPS * STATE + (g + 1) * STATE].astype(BF16)
            cb = _dot_nt(c_g, b_g.astype(BF16))
            yoff = _dot(c_g, h_old.astype(BF16)) * jnp.exp(acx)
            yds = []
            for q in range(GROUP_W // LANES):
                ms_ = []
                for h in (g * 8 + 2 * q, g * 8 + 2 * q + 1):
                    seg = acum[:, h:h + 1] - acum_t[h:h + 1, :]
                    dec = jnp.where(causal, jnp.exp(seg), 0.0)
                    ms_.append((cb * dec).astype(BF16))
                m_cat = jnp.concatenate(ms_, axis=1)
                xp = xdt[:, q * LANES:(q + 1) * LANES]
                r_m = jnp.concatenate([jnp.where(lane < HEADDIM, xp, 0.0),
                                       jnp.where(lane >= HEADDIM, xp, 0.0)], axis=0).astype(BF16)
                yds.append(_dot(m_cat, r_m))
            y_g = jnp.concatenate(yds, axis=1) + yoff + xs_g * dskip_ref[:, cols]
            _gate_norm_store(y_g, zs_ref[:, cols], nw_ref[:, cols], y_ref, cols)

    if with_y:
        uext_ref[PU_BASE:PU_BASE + L, :] = u_ref[...]
        _pool_prompt(uext_ref, l1_ref, l2_ref, l3_ref, pooled_ref, L)
        uext_ref[POOL_MAX:PU_BASE, :] = uext_ref[L + POOL_MAX:L + PU_BASE, :]

        @pl.when(c == nc - 1)
        def _():
            for k in range(INNER // LANES):
                hfin_ref[k * LANES:(k + 1) * LANES, :] = ht_ref[:, k * LANES:(k + 1) * LANES].T
            ctail_ref[...] = xext_ref[0:8, :]
            ptail_ref[...] = uext_ref[POOL_MAX:PU_BASE, :]
    else:
        hfin_ref[...] = ht_ref[...]
        ctail_ref[...] = xext_ref[0:8, :]
        ptail_ref[...] = u_ref[...]
        y_ref[...] = jnp.zeros(y_ref.shape, y_ref.dtype)
        pooled_ref[...] = jnp.zeros(pooled_ref.shape, pooled_ref.dtype)


def _time_of(row):
    return jnp.bitwise_and(row, TS - 1) * TPC + jnp.right_shift(row, TS.bit_length() - 1)


def _shifted_tile(buf_ref, new_ref, j, cols_buf, cols_new):
    rows = slice(j * TS, (j + 1) * TS)
    old = buf_ref[CHUNK + j * TS:CHUNK + (j + 1) * TS, cols_buf]
    new = new_ref[rows, cols_new]
    is_last = _iota2(new.shape, 0) == TS - 1
    buf_ref[rows, cols_buf] = pltpu.roll(jnp.where(is_last, old, new), 1, 0)


def _seq_prompt_kernel(zs_ref, u_ref, xs_ref, bc_ref, dt_ref, h0_ref, ct0_ref, pt0_ref, e64_ref,
                       cw_ref, cbias_ref, dtb_ref, alog_ref, dskip_ref, nw_ref,
                       y_ref, pooled_ref, hfin_ref, ctail_ref, ptail_ref,
                       ht_ref, xbuf_ref, ubuf_ref, act_ref):
    L = CHUNK
    c = pl.program_id(1)
    nc = pl.num_programs(1)
    hist = CONV_K - 1
    all_cols = slice(None)

    @pl.when(c == 0)
    def _():
        ht_ref[...] = h0_ref[...]
        xbuf_ref[L:2 * L, :] = jnp.zeros((L, CONV_DIM), F32)
        ubuf_ref[L:2 * L, :] = jnp.zeros((L, D_MODEL), F32)
        for j in range(TPC - hist, TPC):
            r = L + j * TS + TS - 1
            xbuf_ref[r:r + 1, :] = ct0_ref[j - TS:j - TS + 1, :]
        for j in range(1, TPC):
            r = L + j * TS + TS - 1
            ubuf_ref[r:r + 1, :] = pt0_ref[j:j + 1, :]

    for j in range(TPC - hist, TPC):
        _shifted_tile(xbuf_ref, xs_ref, j, slice(0, INNER), all_cols)
        _shifted_tile(xbuf_ref, bc_ref, j, slice(INNER, CONV_DIM), all_cols)
    xbuf_ref[L:2 * L, 0:INNER] = xs_ref[...]
    xbuf_ref[L:2 * L, INNER:CONV_DIM] = bc_ref[...]
    for cs in range(CONV_DIM // GROUP_W):
        cols = slice(cs * GROUP_W, (cs + 1) * GROUP_W)
        acc = cbias_ref[:, cols] + xbuf_ref[L - hist * TS:2 * L - hist * TS, cols] * cw_ref[0:1, cols]
        for k in range(1, CONV_K):
            back = (hist - k) * TS
            acc = acc + xbuf_ref[L - back:2 * L - back, cols] * cw_ref[k:k + 1, cols]
        act_ref[:, cols] = _silu(acc)

    t_r = _time_of(_iota2((L, L), 0))
    t_c = _time_of(_iota2((L, L), 1))
    causal = t_c <= t_r
    tri = jnp.where(causal, 1.0, 0.0).astype(BF16)
    dtv, da = _dt_terms(dt_ref, dtb_ref, alog_ref)
    acum = _dot_exact_l(tri, da)
    stacked = _split3(jnp.concatenate([acum, dtv], axis=0))
    acum_t = acum.T
    lane = _iota2((L, LANES), 1)

    for g in range(GROUPS):
        cols = slice(g * GROUP_W, (g + 1) * GROUP_W)
        ex = _dot_exact(stacked, e64_ref[:, cols])
        acx = ex[0:L, :]
        dtx = ex[L:2 * L, :]
        xs_g = act_ref[:, cols]
        b_g = act_ref[:, INNER + g * STATE:INNER + (g + 1) * STATE]
        xdt = xs_g * dtx
        aend = acx[L - 1:L, :]
        xdtw = (xdt * jnp.exp(aend - acx)).astype(BF16)
        h_old = ht_ref[:, cols]
        ht_ref[:, cols] = h_old * jnp.exp(aend) + _dot(b_g.T.astype(BF16), xdtw)

        c_g = act_ref[:, INNER + GROUPS * STATE + g * STATE:
                      INNER + GROUPS * STATE + (g + 1) * STATE].astype(BF16)
        cb = _dot_nt(c_g, b_g.astype(BF16))
        yoff = _dot(c_g, h_old.astype(BF16)) * jnp.exp(acx)
        yds = []
        for q in range(GROUP_W // LANES):
            ms_ = []
            for h in (g * 8 + 2 * q, g * 8 + 2 * q + 1):
                seg = acum[:, h:h + 1] - acum_t[h:h + 1, :]
                dec = jnp.where(causal, jnp.exp(seg), 0.0)
                ms_.append((cb * dec).astype(BF16))
            m_cat = jnp.concatenate(ms_, axis=1)
            xp = xdt[:, q * LANES:(q + 1) * LANES]
            r_m = jnp.concatenate([jnp.where(lane < HEADDIM, xp, 0.0),
                                   jnp.where(lane >= HEADDIM, xp, 0.0)], axis=0).astype(BF16)
            yds.append(_dot(m_cat, r_m))
        y_g = jnp.concatenate(yds, axis=1) + yoff + xs_g * dskip_ref[:, cols]
        _gate_norm_store(y_g, zs_ref[:, cols], nw_ref[:, cols], y_ref, cols)

    for gi, w in enumerate(POOL_WINDOWS):
        cols = slice(gi * POOL_GW, (gi + 1) * POOL_GW)
        for j in range(TPC - (w - 1), TPC):
            _shifted_tile(ubuf_ref, u_ref, j, cols, cols)
    ubuf_ref[L:2 * L, :] = u_ref[...]
    for gi, w in enumerate(POOL_WINDOWS):
        cols = slice(gi * POOL_GW, (gi + 1) * POOL_GW)
        lo = 2 * L - L - (w - 1) * TS
        s = ubuf_ref[lo:2 * L, cols]
        step = TS
        while step < w * TS:
            n = s.shape[0]
            s = s[step:n, :] + s[0:n - step, :]
            step *= 2
        pooled_ref[:, cols] = (s * (1.0 / w) - u_ref[:, cols]).astype(BF16)

    @pl.when(c == nc - 1)
    def _():
        for k in range(INNER // LANES):
            hfin_ref[k * LANES:(k + 1) * LANES, :] = ht_ref[:, k * LANES:(k + 1) * LANES].T
        ctail_ref[...] = jnp.zeros(ctail_ref.shape, F32)
        ptail_ref[...] = jnp.zeros(ptail_ref.shape, F32)
        for j in range(TPC - hist, TPC):
            r = L + j * TS + TS - 1
            ctail_ref[j - TS:j - TS + 1, :] = xbuf_ref[r:r + 1, :]
        for j in range(1, TPC):
            r = L + j * TS + TS - 1
            ptail_ref[j:j + 1, :] = ubuf_ref[r:r + 1, :]


def _param_specs(index_map):
    return [
        pl.BlockSpec((CONV_K, CONV_DIM), index_map),
        pl.BlockSpec((1, CONV_DIM), index_map),
        pl.BlockSpec((1, LANES), index_map),
        pl.BlockSpec((1, LANES), index_map),
        pl.BlockSpec((1, INNER), index_map),
        pl.BlockSpec((1, INNER), index_map),
    ]


def _seq_scratch(rows):
    return [
        pltpu.VMEM((STATE, INNER), F32),
        pltpu.VMEM((8 + rows, CONV_DIM), F32),
        pltpu.VMEM((PU_BASE + rows, D_MODEL), F32),
        pltpu.VMEM((rows, CONV_DIM), F32),
        pltpu.VMEM((PU_BASE + rows, POOL_GW), F32),
        pltpu.VMEM((PU_BASE + rows, POOL_GW), F32),
        pltpu.VMEM((PU_BASE + rows, POOL_GW), F32),
    ]


def _seq_prompt(proj, dt, h0t, ct0, pt0, e64, params, *, batch, seq):
    nc = seq // CHUNK
    row = lambda b, c: b * nc + c
    const = lambda b, c: (0, 0)
    in_specs = [
        pl.BlockSpec((CHUNK, INNER), lambda b, c: (row(b, c), CB_ZS)),
        pl.BlockSpec((CHUNK, INNER), lambda b, c: (row(b, c), CB_U)),
        pl.BlockSpec((CHUNK, INNER), lambda b, c: (row(b, c), CB_XS)),
        pl.BlockSpec((CHUNK, 2 * GROUPS * STATE), lambda b, c: (row(b, c), CB_BC)),
        pl.BlockSpec((CHUNK, LANES), lambda b, c: (row(b, c), 0)),
        pl.BlockSpec((STATE, INNER), const),
        pl.BlockSpec((8, CONV_DIM), const),
        pl.BlockSpec((POOL_MAX, D_MODEL), const),
        pl.BlockSpec((LANES, INNER), const),
    ] + _param_specs(const)
    return pl.pallas_call(
        _seq_prompt_kernel,
        grid=(batch, nc),
        in_specs=in_specs,
        out_specs=[
            pl.BlockSpec((CHUNK, INNER), lambda b, c: (row(b, c), 0)),
            pl.BlockSpec((CHUNK, D_MODEL), lambda b, c: (row(b, c), 0)),
            pl.BlockSpec((None, INNER, STATE), lambda b, c: (b, 0, 0)),
            pl.BlockSpec((None, 8, CONV_DIM), lambda b, c: (b, 0, 0)),
            pl.BlockSpec((None, POOL_MAX, D_MODEL), lambda b, c: (b, 0, 0)),
        ],
        out_shape=[
            jax.ShapeDtypeStruct((batch * seq, INNER), BF16),
            jax.ShapeDtypeStruct((batch * seq, D_MODEL), BF16),
            jax.ShapeDtypeStruct((batch, INNER, STATE), F32),
            jax.ShapeDtypeStruct((batch, 8, CONV_DIM), F32),
            jax.ShapeDtypeStruct((batch, POOL_MAX, D_MODEL), F32),
        ],
        scratch_shapes=[
            pltpu.VMEM((STATE, INNER), F32),
            pltpu.VMEM((2 * CHUNK, CONV_DIM), F32),
            pltpu.VMEM((2 * CHUNK, D_MODEL), F32),
            pltpu.VMEM((CHUNK, CONV_DIM), F32),
        ],
        compiler_params=pltpu.CompilerParams(
            dimension_semantics=("arbitrary", "arbitrary"),
            vmem_limit_bytes=VMEM_LIMIT),
        name="seq_prompt",
    )(proj, proj, proj, proj, dt, h0t, ct0, pt0, e64, *params)


def _meta_state(proj, dt, e64, params, *, row_block):
    const = lambda b, c: (0, 0)
    zeros = functools.partial(jnp.zeros, dtype=F32)
    in_specs = [
        pl.BlockSpec((N_META, INNER), lambda b, c: (row_block, CB_ZS)),
        pl.BlockSpec((N_META, INNER), lambda b, c: (row_block, CB_U)),
        pl.BlockSpec((N_META, INNER), lambda b, c: (row_block, CB_XS)),
        pl.BlockSpec((N_META, 2 * GROUPS * STATE), lambda b, c: (row_block, CB_BC)),
        pl.BlockSpec((N_META, LANES), lambda b, c: (row_block, 0)),
        pl.BlockSpec((STATE, INNER), const),
        pl.BlockSpec((8, CONV_DIM), const),
        pl.BlockSpec((POOL_MAX, D_MODEL), const),
        pl.BlockSpec((LANES, INNER), const),
    ] + _param_specs(const)
    outs = pl.pallas_call(
        functools.partial(_seq_natural_kernel, with_y=False),
        grid=(1, 1),
        in_specs=in_specs,
        out_specs=[
            pl.BlockSpec((N_META, INNER), const),
            pl.BlockSpec((N_META, D_MODEL), const),
            pl.BlockSpec((STATE, INNER), const),
            pl.BlockSpec((8, CONV_DIM), const),
            pl.BlockSpec((POOL_MAX, D_MODEL), const),
        ],
        out_shape=[
            jax.ShapeDtypeStruct((N_META, INNER), BF16),
            jax.ShapeDtypeStruct((N_META, D_MODEL), BF16),
            jax.ShapeDtypeStruct((STATE, INNER), F32),
            jax.ShapeDtypeStruct((8, CONV_DIM), F32),
            jax.ShapeDtypeStruct((POOL_MAX, D_MODEL), F32),
        ],
        scratch_shapes=_seq_scratch(N_META),
        compiler_params=pltpu.CompilerParams(
            dimension_semantics=("arbitrary", "arbitrary"),
            vmem_limit_bytes=VMEM_LIMIT),
        name="meta_state",
    )(proj, proj, proj, proj, dt, zeros((STATE, INNER)), zeros((8, CONV_DIM)),
      zeros((POOL_MAX, D_MODEL)), e64, *params)
    return outs[2], outs[3], outs[4]


def _seq_sample_kernel(zs_ref, u_ref, xs_ref, bc_ref, dt_ref, chist_ref, phist_ref, h0_ref, e64_ref,
                       cw_ref, cbias_ref, dtb_ref, alog_ref, dskip_ref, nw_ref,
                       y_ref, pooled_ref, hnew_ref, cnew_ref, pnew_ref,
                       xext_ref, uext_ref, act_ref, dv_ref, s_ref):
    R = xs_ref.shape[0]
    T = R // SEQ_BLOCK
    SB = SEQ_BLOCK
    HIST = CONV_K - 1
    PH = POOL_MAX - 1
    half = R

    for k in range(HIST):
        xext_ref[k * SB:(k + 1) * SB, :] = chist_ref[k]
    xext_ref[HIST * SB:HIST * SB + R, 0:INNER] = xs_ref[...]
    xext_ref[HIST * SB:HIST * SB + R, INNER:CONV_DIM] = bc_ref[...]
    for cs in range(CONV_DIM // GROUP_W):
        cols = slice(cs * GROUP_W, (cs + 1) * GROUP_W)
        acc = cbias_ref[:, cols] + xext_ref[0:R, cols] * cw_ref[0:1, cols]
        for k in range(1, CONV_K):
            acc = acc + xext_ref[k * SB:k * SB + R, cols] * cw_ref[k:k + 1, cols]
        act_ref[:, cols] = _silu(acc)
    for k in range(HIST):
        cnew_ref[k] = xext_ref[R + k * SB:R + (k + 1) * SB, :]

    seq_of = lambda x: jnp.bitwise_and(x, SB - 1)
    step_of = lambda x: _div_pow2(x, SB)
    r_i = _iota2((R, R), 0)
    c_i = _iota2((R, R), 1)
    same = seq_of(r_i) == seq_of(c_i)
    tri = jnp.where(same & (step_of(c_i) <= step_of(r_i)), 1.0, 0.0).astype(BF16)
    ones_bd = jnp.where(same, 1.0, 0.0).astype(BF16)
    dtv, da = _dt_terms(dt_ref, dtb_ref, alog_ref)
    da_parts = _split3(da)
    acum = _dot(tri, da_parts[0]) + _dot(tri, da_parts[1]) + _dot(tri, da_parts[2])
    a_end = (_dot(ones_bd, da_parts[0]) + _dot(ones_bd, da_parts[1])
             + _dot(ones_bd, da_parts[2]))

    acum_t2 = jnp.concatenate([acum, acum], axis=0).T
    aend_t = _pad_rows(a_end, LANES).T

    r_s = _iota2((LANES, SB * STATE), 0)
    c_s = _iota2((LANES, SB * STATE), 1)
    sel = jnp.where(r_s == _div_pow2(c_s, STATE), 1.0, 0.0).astype(BF16)
    dv_ref[...] = jnp.exp(_dot_exact(_split3(aend_t), sel))

    lane2 = _iota2((R, 2 * R), 1)
    r_2 = _iota2((R, 2 * R), 0)
    c_2 = jnp.bitwise_and(lane2, R - 1)
    mask2 = (seq_of(r_2) == seq_of(c_2)) & (step_of(c_2) <= step_of(r_2))
    first_half = lane2 < half
    lane = _iota2((R, LANES), 1)
    r_b = _iota2((R, SB * STATE), 0)
    c_b = _iota2((R, SB * STATE), 1)
    bd_sel = seq_of(r_b) == _div_pow2(c_b, STATE)
    stacked = _split3(jnp.concatenate([acum, dtv, a_end], axis=0))

    for g in range(GROUPS):
        cols = slice(g * GROUP_W, (g + 1) * GROUP_W)
        ex = _dot_exact(stacked, e64_ref[:, cols])
        acx = ex[0:R, :]
        dtx = ex[R:2 * R, :]
        aendx = ex[2 * R:3 * R, :]
        xs_g = act_ref[:, cols]
        b_g = act_ref[:, INNER + g * STATE:INNER + (g + 1) * STATE]
        c_f = act_ref[:, INNER + GROUPS * STATE + g * STATE:INNER + GROUPS * STATE + (g + 1) * STATE]
        c_g = c_f.astype(BF16)
        xdt = xs_g * dtx
        xdtw = xdt * jnp.exp(aendx - acx)

        c_big = jnp.where(bd_sel, jnp.concatenate([c_f] * SB, axis=1), 0.0).astype(BF16)
        h_cat = jnp.concatenate(
            [h0_ref[i, g * GROUP_W:(g + 1) * GROUP_W, :].astype(BF16) for i in range(SB)], axis=1)
        yoff = _dot_nt(c_big, h_cat)

        xt = _pad_rows(xdtw, LANES).T
        b_big = jnp.where(bd_sel, jnp.concatenate([b_g] * SB, axis=1), 0.0)
        b_big = _pad_rows(b_big, LANES)
        s_ref[...] = _dot(xt.astype(BF16), b_big.astype(BF16))
        for i in range(SB):
            for hh in range(GROUP_W // HEADDIM):
                h = g * (GROUP_W // HEADDIM) + hh
                rows = slice(h * HEADDIM, (h + 1) * HEADDIM)
                hnew_ref[i, rows, :] = (
                    h0_ref[i, rows, :] * dv_ref[h:h + 1, i * STATE:(i + 1) * STATE]
                    + s_ref[hh * HEADDIM:(hh + 1) * HEADDIM, i * STATE:(i + 1) * STATE])

        cb2 = _dot_nt(c_g, jnp.concatenate([b_g, b_g], axis=0).astype(BF16))
        yds = []
        for q in range(GROUP_W // LANES):
            ha = g * 8 + 2 * q
            colc = jnp.where(first_half, acum[:, ha:ha + 1], acum[:, ha + 1:ha + 2])
            rowc = jnp.where(first_half[0:1, :], acum_t2[ha:ha + 1, :], acum_t2[ha + 1:ha + 2, :])
            dec = jnp.where(mask2, jnp.exp(colc - rowc), 0.0)
            m_p = (cb2 * dec).astype(BF16)
            xp = xdt[:, q * LANES:(q + 1) * LANES]
            r_m = jnp.concatenate([jnp.where(lane < HEADDIM, xp, 0.0),
                                   jnp.where(lane >= HEADDIM, xp, 0.0)], axis=0).astype(BF16)
            yds.append(_dot(m_p, r_m))
        y_g = jnp.concatenate(yds, axis=1) + yoff * jnp.exp(acx) + xs_g * dskip_ref[:, cols]
        _gate_norm_store(y_g, zs_ref[:, cols], nw_ref[:, cols], y_ref, cols)

    for k in range(PH):
        uext_ref[k * SB:(k + 1) * SB, :] = phist_ref[k]
    uext_ref[PH * SB:PH * SB + R, :] = u_ref[...]
    for k in range(PH):
        pnew_ref[k] = uext_ref[(T + k) * SB:(T + k + 1) * SB, :]
    for gi, w in enumerate(POOL_WINDOWS):
        cols = slice(gi * POOL_GW, (gi + 1) * POOL_GW)
        u_cur = u_ref[:, cols]
        s = u_cur
        for k in range(1, w):
            s = s + uext_ref[(PH - k) * SB:(PH - k) * SB + R, cols]
        pooled_ref[:, cols] = (s * (1.0 / w) - u_cur).astype(BF16)


def _seq_sample(proj, dt, conv_hist, pool_hist, h0, e64, params, *, nseq, t_new):
    rb = SEQ_BLOCK * t_new
    assert 2 * rb == LANES
    const = lambda s: (0, 0)
    in_specs = [
        pl.BlockSpec((rb, INNER), lambda s: (s, CB_ZS)),
        pl.BlockSpec((rb, INNER), lambda s: (s, CB_U)),
        pl.BlockSpec((rb, INNER), lambda s: (s, CB_XS)),
        pl.BlockSpec((rb, 2 * GROUPS * STATE), lambda s: (s, CB_BC)),
        pl.BlockSpec((rb, LANES), lambda s: (s, 0)),
        pl.BlockSpec((CONV_K - 1, SEQ_BLOCK, CONV_DIM), lambda s: (0, s, 0)),
        pl.BlockSpec((POOL_MAX - 1, SEQ_BLOCK, D_MODEL), lambda s: (0, s, 0)),
        pl.BlockSpec((SEQ_BLOCK, INNER, STATE), lambda s: (s, 0, 0)),
        pl.BlockSpec((LANES, INNER), const),
    ] + _param_specs(const)
    return pl.pallas_call(
        _seq_sample_kernel,
        grid=(nseq // SEQ_BLOCK,),
        in_specs=in_specs,
        out_specs=[
            pl.BlockSpec((rb, INNER), lambda s: (s, 0)),
            pl.BlockSpec((rb, D_MODEL), lambda s: (s, 0)),
            pl.BlockSpec((SEQ_BLOCK, INNER, STATE), lambda s: (s, 0, 0)),
            pl.BlockSpec((CONV_K - 1, SEQ_BLOCK, CONV_DIM), lambda s: (0, s, 0)),
            pl.BlockSpec((POOL_MAX - 1, SEQ_BLOCK, D_MODEL), lambda s: (0, s, 0)),
        ],
        out_shape=[
            jax.ShapeDtypeStruct((nseq * t_new, INNER), BF16),
            jax.ShapeDtypeStruct((nseq * t_new, D_MODEL), BF16),
            jax.ShapeDtypeStruct((nseq, INNER, STATE), F32),
            jax.ShapeDtypeStruct((CONV_K - 1, nseq, CONV_DIM), F32),
            jax.ShapeDtypeStruct((POOL_MAX - 1, nseq, D_MODEL), F32),
        ],
        scratch_shapes=[
            pltpu.VMEM((SEQ_BLOCK * (CONV_K - 1 + t_new), CONV_DIM), F32),
            pltpu.VMEM((SEQ_BLOCK * (POOL_MAX - 1 + t_new), D_MODEL), F32),
            pltpu.VMEM((rb, CONV_DIM), F32),
            pltpu.VMEM((LANES, SEQ_BLOCK * STATE), F32),
            pltpu.VMEM((GROUP_W, SEQ_BLOCK * STATE), F32),
        ],
        compiler_params=pltpu.CompilerParams(
            dimension_semantics=("arbitrary",),
            vmem_limit_bytes=VMEM_LIMIT),
        name="seq_sample",
    )(proj, proj, proj, proj, dt, conv_hist, pool_hist, h0, e64, *params)


def _merge_kernel(y_ref, pooled_ref, zp_ref, g1_ref, g2_ref, wmix_ref, bmix_ref, pscale_ref,
                  wps_ref, wpp_ref, o_ref, pout_ref):
    for g in range(len(POOL_WINDOWS)):
        cols = slice(g * POOL_GW, (g + 1) * POOL_GW)
        mixed = _dot(pooled_ref[:, cols], wmix_ref[g]) + bmix_ref[:, cols]
        pout_ref[:, cols] = (mixed * pscale_ref[:, cols] * _silu(zp_ref[:, cols])).astype(BF16)
    bs = _dot(y_ref[...], wps_ref[...])
    bp = _dot(pout_ref[...], wpp_ref[...])
    o_ref[...] = (_sigmoid(g1_ref[...]) * bs + _sigmoid(g2_ref[...]) * bp).astype(BF16)


def _merge(y, pooled, proj, wmix, bmix, pscale, wps, wpp, *, m, tm=256):
    const2 = lambda i: (0, 0)
    resident = dict(pipeline_mode=pl.Buffered(1))
    return pl.pallas_call(
        _merge_kernel,
        grid=(m // tm,),
        in_specs=[
            pl.BlockSpec((tm, INNER), lambda i: (i, 0)),
            pl.BlockSpec((tm, D_MODEL), lambda i: (i, 0)),
            pl.BlockSpec((tm, D_MODEL), lambda i: (i, CB_ZP)),
            pl.BlockSpec((tm, D_MODEL), lambda i: (i, CB_G1)),
            pl.BlockSpec((tm, D_MODEL), lambda i: (i, CB_G2)),
            pl.BlockSpec((len(POOL_WINDOWS), POOL_GW, POOL_GW), lambda i: (0, 0, 0), **resident),
            pl.BlockSpec((1, D_MODEL), const2),
            pl.BlockSpec((1, D_MODEL), const2),
            pl.BlockSpec((INNER, D_MODEL), const2, **resident),
            pl.BlockSpec((D_MODEL, D_MODEL), const2, **resident),
        ],
        out_specs=pl.BlockSpec((tm, D_MODEL), lambda i: (i, 0)),
        out_shape=jax.ShapeDtypeStruct((m, D_MODEL), BF16),
        scratch_shapes=[pltpu.VMEM((tm, D_MODEL), BF16)],
        compiler_params=pltpu.CompilerParams(
            dimension_semantics=("arbitrary",),
            vmem_limit_bytes=VMEM_LIMIT),
        name="merge",
    )(y, pooled, proj, proj, proj, wmix, bmix, pscale, wps, wpp)


def _out_kernel(m_ref, h_ref, wout_ref, fw_ref, o_ref):
    hn = h_ref[...] + _dot(m_ref[...], wout_ref[...])
    ms = jnp.mean(hn * hn, axis=-1, keepdims=True)
    o_ref[...] = (hn * lax.rsqrt(ms + EPS)) * fw_ref[...]


def _out(merged, h, wout, fw, *, m, tm=512):
    return pl.pallas_call(
        _out_kernel,
        grid=(m // tm,),
        in_specs=[
            pl.BlockSpec((tm, D_MODEL), lambda i: (i, 0)),
            pl.BlockSpec((tm, D_MODEL), lambda i: (i, 0)),
            pl.BlockSpec((D_MODEL, D_MODEL), lambda i: (0, 0)),
            pl.BlockSpec((1, D_MODEL), lambda i: (0, 0)),
        ],
        out_specs=pl.BlockSpec((tm, D_MODEL), lambda i: (i, 0)),
        out_shape=jax.ShapeDtypeStruct((m, D_MODEL), F32),
        compiler_params=pltpu.CompilerParams(
            dimension_semantics=("arbitrary",),
            vmem_limit_bytes=VMEM_LIMIT),
        name="out",
    )(merged, h, wout, fw)


def _out_perm_kernel(*refs, n_chunk):
    m_refs = refs[:TS]
    h_ref, wout_ref, fw_ref, o_ref, lhs_ref = refs[TS:]
    for c in range(n_chunk):
        for s in range(TS):
            lhs_ref[c * CHUNK + s * TPC:c * CHUNK + (s + 1) * TPC, :] = m_refs[s][c]
    hn = h_ref[...] + _dot(lhs_ref[...], wout_ref[...])
    ms = jnp.mean(hn * hn, axis=-1, keepdims=True)
    o_ref[...] = (hn * lax.rsqrt(ms + EPS)) * fw_ref[...]


def _out_perm(merged, h, wout, fw, *, m, tm=512):
    n_chunk = tm // CHUNK
    m3 = merged.reshape(m // CHUNK, TPC, TS * D_MODEL)
    m_specs = [pl.BlockSpec((n_chunk, TPC, D_MODEL), lambda i, ss=ss: (i, 0, ss)) for ss in range(TS)]
    return pl.pallas_call(
        functools.partial(_out_perm_kernel, n_chunk=n_chunk),
        grid=(m // tm,),
        in_specs=m_specs + [
            pl.BlockSpec((tm, D_MODEL), lambda i: (i, 0)),
            pl.BlockSpec((D_MODEL, D_MODEL), lambda i: (0, 0)),
            pl.BlockSpec((1, D_MODEL), lambda i: (0, 0)),
        ],
        out_specs=pl.BlockSpec((tm, D_MODEL), lambda i: (i, 0)),
        out_shape=jax.ShapeDtypeStruct((m, D_MODEL), F32),
        scratch_shapes=[pltpu.VMEM((tm, D_MODEL), BF16)],
        compiler_params=pltpu.CompilerParams(
            dimension_semantics=("arbitrary",),
            vmem_limit_bytes=VMEM_LIMIT),
        name="out_perm",
    )(*([m3] * TS), h, wout, fw)


def kernel(x_prompt, x_sample, state_conv, state_ssm, state_pool, meta_tokens, norm_w, w_in,
           conv_w, conv_b, dt_bias, a_log, d_skip, ssd_norm_w, w_proj_ssd, pool_mix_w,
           pool_mix_b, pool_scale, w_proj_pool, w_out, final_norm_w):
    batch, seq, _ = x_prompt.shape
    nseq, t_new, _ = x_sample.shape
    assert norm_w.shape[0] == 1, "single layer"
    assert w_in.shape == (1, D_MODEL, IN_ZP + 4 * D_MODEL)

    w_main, w_dt = _prep_weights(jnp.swapaxes(w_in, 1, 2)[0])
    wps = w_proj_ssd[0].astype(BF16)
    wpp = w_proj_pool[0].astype(BF16)
    wout = w_out[0].astype(BF16)
    wmix = pool_mix_w[0].astype(BF16)
    lane_pad = lambda v: jnp.pad(v.reshape(1, HEADS), ((0, 0), (0, LANES - HEADS)))
    params = (conv_w[0], conv_b[0].reshape(1, CONV_DIM), lane_pad(dt_bias[0]), lane_pad(a_log[0]),
              jnp.repeat(d_skip[0], HEADDIM).reshape(1, INNER), ssd_norm_w[0].reshape(1, INNER))
    e64 = (jnp.arange(LANES)[:, None] == (jnp.arange(INNER)[None, :] // HEADDIM)).astype(BF16)
    nw = norm_w[0].reshape(1, D_MODEL)
    fw = final_norm_w.reshape(1, D_MODEL)
    bmix = pool_mix_b[0].reshape(1, D_MODEL)
    pscale = pool_scale[0].reshape(1, D_MODEL)

    xp = x_prompt.reshape(batch * seq, D_MODEL)
    m_s = nseq * t_new
    nblk = nseq // SEQ_BLOCK
    xs_tm = x_sample.reshape(nblk, SEQ_BLOCK, t_new, D_MODEL).transpose(0, 2, 1, 3)
    x_sm = jnp.concatenate([xs_tm.reshape(m_s, D_MODEL), meta_tokens], axis=0)

    proj_p, dt_p = _proj_perm(xp, nw, w_main, w_dt)
    proj_s, dt_s = _proj(x_sm, nw, w_main, w_dt, tm=m_s + N_META)

    h0t, ct0, pt0 = _meta_state(proj_s, dt_s, e64, params, row_block=m_s // N_META)

    y_p, pooled_p, ssm_p, conv_p, pool_p = _seq_prompt(
        proj_p, dt_p, h0t, ct0, pt0, e64, params, batch=batch, seq=seq)
    y_s, pooled_s, ssm_s, conv_s, pool_s = _seq_sample(
        proj_s, dt_s, jnp.swapaxes(state_conv[0], 0, 1), jnp.swapaxes(state_pool[0], 0, 1),
        state_ssm[0].reshape(nseq, INNER, STATE), e64, params, nseq=nseq, t_new=t_new)

    merged_p = _merge(y_p, pooled_p, proj_p, wmix, bmix, pscale, wps, wpp, m=batch * seq)
    merged_s = _merge(y_s, pooled_s, proj_s, wmix, bmix, pscale, wps, wpp, m=m_s)
    out_p = _out_perm(merged_p, xp, wout, fw, m=batch * seq)
    out_s = _out(merged_s, x_sm, wout, fw, m=m_s)

    return (out_p.reshape(batch, seq, D_MODEL),
            out_s.reshape(nblk, t_new, SEQ_BLOCK, D_MODEL).transpose(0, 2, 1, 3).reshape(
                nseq, t_new, D_MODEL),
            conv_p[:, 8 - (CONV_K - 1):][None].astype(state_conv.dtype),
            ssm_p.reshape(1, batch, HEADS, HEADDIM, STATE).astype(state_ssm.dtype),
            pool_p[:, 1:][None].astype(state_pool.dtype),
            jnp.swapaxes(conv_s, 0, 1)[None].astype(state_conv.dtype),
            ssm_s.reshape(1, nseq, HEADS, HEADDIM, STATE).astype(state_ssm.dtype),
            jnp.swapaxes(pool_s, 0, 1)[None].astype(state_pool.dtype))
```

```python
import functools

import jax
import jax.numpy as jnp
from jax import lax
from jax.experimental import pallas as pl
from jax.experimental.pallas import tpu as pltpu

F32 = jnp.float32
BF16 = jnp.bfloat16

D_MODEL = 2048
N_META = 16
EPS = 1e-6
HEADS = 32
HEADDIM = 64
GROUPS = 4
STATE = 128
CONV_K = 4
INNER = HEADS * HEADDIM
GROUP_W = INNER // GROUPS
CONV_DIM = INNER + 2 * GROUPS * STATE
POOL_WINDOWS = (2, 4, 8, 16)
POOL_MAX = 16
POOL_GW = D_MODEL // len(POOL_WINDOWS)

IN_XBC = INNER
IN_DT = IN_XBC + CONV_DIM
IN_ZP = IN_DT + HEADS
N_MAIN = 5 * D_MODEL + CONV_DIM
CB_ZS, CB_ZP, CB_U, CB_G1, CB_G2, CB_XS = 0, 1, 2, 3, 4, 5
CB_BC = 12

LANES = 128
CHUNK = 128
SEQ_BLOCK = 8
VMEM_LIMIT = 60 * 1024 * 1024
PREP_TN = 1024
PU_BASE = 2 * POOL_MAX
TPC = 16
TS = CHUNK // TPC


def _silu(x):
    hx = 0.5 * x
    return hx + hx * jnp.tanh(hx)


def _sigmoid(x):
    return 0.5 + 0.5 * jnp.tanh(0.5 * x)


def _dot(a, b):
    return jnp.dot(a, b, preferred_element_type=F32)


def _dot_nt(a, b):
    return lax.dot_general(a, b, (((1,), (1,)), ((), ())), preferred_element_type=F32)


def _split3(a):
    a1 = a.astype(BF16)
    r1 = a - a1.astype(F32)
    a2 = r1.astype(BF16)
    a3 = (r1 - a2.astype(F32)).astype(BF16)
    return a1, a2, a3


def _dot_exact(parts, b):
    out = _dot(parts[0], b)
    for p in parts[1:]:
        out = out + _dot(p, b)
    return out


def _dot_exact_l(a01, x):
    parts = _split3(x)
    out = _dot(a01, parts[0])
    for p in parts[1:]:
        out = out + _dot(a01, p)
    return out


def _iota2(shape, dim):
    return lax.broadcasted_iota(jnp.int32, shape, dim)


def _div_pow2(x, d):
    assert d & (d - 1) == 0
    return jnp.right_shift(x, d.bit_length() - 1)


def _pad_rows(x, rows):
    if x.shape[0] == rows:
        return x
    return jnp.concatenate([x, jnp.zeros((rows - x.shape[0], x.shape[1]), x.dtype)], axis=0)


def _prep_kernel(a_ref, d_ref, o_ref, dt_ref, *, sub):
    @pl.when(pl.program_id(0) == 0)
    def _():
        row = _iota2((LANES, D_MODEL), 0)
        dt_ref[...] = jnp.where(row < HEADS, d_ref[...], 0.0).astype(BF16)

    def body(r, carry):
        rows = pl.ds(pl.multiple_of(r * sub, sub), sub)
        o_ref[rows, :] = a_ref[rows, :].astype(BF16)
        return carry

    lax.fori_loop(0, PREP_TN // sub, body, 0)


def _prep_weights(w_t):
    def a_map(c):
        src = jnp.where(c < 2, c * PREP_TN,
                        jnp.where(c < 10, IN_ZP + (c - 2) * PREP_TN, IN_XBC + (c - 10) * PREP_TN))
        return (pl.multiple_of(src, 32), 0)

    return pl.pallas_call(
        functools.partial(_prep_kernel, sub=256),
        grid=(N_MAIN // PREP_TN,),
        in_specs=[
            pl.BlockSpec((pl.Element(PREP_TN), pl.Element(D_MODEL)), a_map),
            pl.BlockSpec((pl.Element(LANES), pl.Element(D_MODEL)), lambda c: (IN_DT, 0)),
        ],
        out_specs=[
            pl.BlockSpec((PREP_TN, D_MODEL), lambda c: (c, 0)),
            pl.BlockSpec((LANES, D_MODEL), lambda c: (0, 0)),
        ],
        out_shape=[
            jax.ShapeDtypeStruct((N_MAIN, D_MODEL), BF16),
            jax.ShapeDtypeStruct((LANES, D_MODEL), BF16),
        ],
        compiler_params=pltpu.CompilerParams(
            dimension_semantics=("arbitrary",), vmem_limit_bytes=VMEM_LIMIT),
        name="prep",
    )(w_t, w_t)


def _time_of(row):
    return jnp.bitwise_and(row, TS - 1) * TPC + jnp.right_shift(row, TS.bit_length() - 1)


def _tile_major_perm(transpose=False):
    r = _iota2((CHUNK, CHUNK), 0)
    c = _iota2((CHUNK, CHUNK), 1)
    hit = (r == _time_of(c)) if transpose else (c == _time_of(r))
    return jnp.where(hit, 1.0, 0.0).astype(BF16)


def _proj_kernel(x_ref, nw_ref, w_ref, wdt_ref, o_ref, dt_ref, xn_ref, *, sub, tile_major):
    tm = x_ref.shape[0]

    @pl.when(pl.program_id(1) == 0)
    def _():
        def body(r, carry):
            rows = pl.ds(pl.multiple_of(r * sub, sub), sub)
            x = x_ref[rows, :]
            ms = jnp.mean(x * x, axis=-1, keepdims=True)
            xn = (x * lax.rsqrt(ms + EPS)) * nw_ref[...]
            xn_ref[rows, :] = xn.astype(BF16)
            return carry

        lax.fori_loop(0, tm // sub, body, 0)
        if tile_major:
            perm = _tile_major_perm()

            def permute(r, carry):
                rows = pl.ds(pl.multiple_of(r * CHUNK, CHUNK), CHUNK)
                xn_ref[rows, :] = _dot(perm, xn_ref[rows, :]).astype(BF16)
                return carry

            lax.fori_loop(0, tm // CHUNK, permute, 0)
        dt_ref[...] = _dot_nt(xn_ref[...], wdt_ref[...])

    o_ref[...] = _dot_nt(xn_ref[...], w_ref[...])


def _proj(x, norm_w, w_main, w_dt, *, tm, tn=1024, tile_major=False):
    m = x.shape[0]
    assert m % tm == 0 and N_MAIN % tn == 0
    sub = 16 if tm % 128 else 128
    return pl.pallas_call(
        functools.partial(_proj_kernel, sub=sub, tile_major=tile_major),
        grid=(m // tm, N_MAIN // tn),
        in_specs=[
            pl.BlockSpec((tm, D_MODEL), lambda i, j: (i, 0)),
            pl.BlockSpec((1, D_MODEL), lambda i, j: (0, 0)),
            pl.BlockSpec((tn, D_MODEL), lambda i, j: (j, 0)),
            pl.BlockSpec((LANES, D_MODEL), lambda i, j: (0, 0)),
        ],
        out_specs=[
            pl.BlockSpec((tm, tn), lambda i, j: (i, j)),
            pl.BlockSpec((tm, LANES), lambda i, j: (i, 0)),
        ],
        out_shape=[
            jax.ShapeDtypeStruct((m, N_MAIN), F32),
            jax.ShapeDtypeStruct((m, LANES), F32),
        ],
        scratch_shapes=[pltpu.VMEM((tm, D_MODEL), BF16)],
        compiler_params=pltpu.CompilerParams(
            dimension_semantics=("arbitrary", "arbitrary"),
            vmem_limit_bytes=VMEM_LIMIT),
        name="proj",
    )(x, norm_w, w_main, w_dt)


def _conv_silu(ext_ref, first, rows, cw_ref, cbias_ref, act_ref, row0):
    for cs in range(CONV_DIM // GROUP_W):
        cols = slice(cs * GROUP_W, (cs + 1) * GROUP_W)
        acc = cbias_ref[:, cols] + ext_ref[first:first + rows, cols] * cw_ref[0:1, cols]
        for k in range(1, CONV_K):
            acc = acc + ext_ref[first + k:first + k + rows, cols] * cw_ref[k:k + 1, cols]
        act_ref[row0:row0 + rows, cols] = _silu(acc)


def _dt_terms(dt_ref, dtb_ref, alog_ref):
    x = dt_ref[...] + dtb_ref[...]
    dtv = jnp.maximum(x, 0.0) + jnp.log1p(jnp.exp(-jnp.abs(x)))
    da = dtv * (-jnp.exp(alog_ref[...]))
    return dtv, da


def _gate_norm_store(y_g, zs, nw, y_ref, cols):
    y_g = y_g * _silu(zs)
    ms = jnp.mean(y_g * y_g, axis=-1, keepdims=True)
    y_ref[:, cols] = (y_g * lax.rsqrt(ms + EPS) * nw).astype(BF16)


def _pool_prompt(uext_ref, l1_ref, l2_ref, l3_ref, pooled_ref, rows):
    base, end = PU_BASE, PU_BASE + rows
    for gi, w in enumerate(POOL_WINDOWS):
        cols = slice(gi * POOL_GW, (gi + 1) * POOL_GW)
        levels = w.bit_length() - 1
        lo = base - 8 * (levels - 1)
        if levels == 1:
            s = uext_ref[base:end, cols] + uext_ref[base - 1:end - 1, cols]
        else:
            l1_ref[lo:end, :] = uext_ref[lo:end, cols] + uext_ref[lo - 1:end - 1, cols]
            if levels == 2:
                s = l1_ref[base:end, :] + l1_ref[base - 2:end - 2, :]
            else:
                lo2 = lo + 8
                l2_ref[lo2:end, :] = l1_ref[lo2:end, :] + l1_ref[lo2 - 2:end - 2, :]
                if levels == 3:
                    s = l2_ref[base:end, :] + l2_ref[base - 4:end - 4, :]
                else:
                    lo3 = lo2 + 8
                    l3_ref[lo3:end, :] = l2_ref[lo3:end, :] + l2_ref[lo3 - 4:end - 4, :]
                    s = l3_ref[base:end, :] + l3_ref[base - 8:end - 8, :]
        pooled_ref[:, cols] = (s * (1.0 / w) - uext_ref[base:end, cols]).astype(BF16)


def _seq_natural_kernel(zs_ref, u_ref, xs_ref, bc_ref, dt_ref, h0_ref, ct0_ref, pt0_ref, e64_ref,
                        cw_ref, cbias_ref, dtb_ref, alog_ref, dskip_ref, nw_ref,
                        y_ref, pooled_ref, hfin_ref, ctail_ref, ptail_ref,
                        ht_ref, xext_ref, uext_ref, act_ref, l1_ref, l2_ref, l3_ref, *, with_y):
    L = xs_ref.shape[0]
    c = pl.program_id(1)
    nc = pl.num_programs(1)

    @pl.when(c == 0)
    def _():
        ht_ref[...] = h0_ref[...]
        xext_ref[0:8, :] = ct0_ref[...]
        uext_ref[0:POOL_MAX, :] = jnp.zeros((POOL_MAX, D_MODEL), F32)
        uext_ref[POOL_MAX:PU_BASE, :] = pt0_ref[...]

    xext_ref[8:8 + L, 0:INNER] = xs_ref[...]
    xext_ref[8:8 + L, INNER:CONV_DIM] = bc_ref[...]
    _conv_silu(xext_ref, 8 - (CONV_K - 1), L, cw_ref, cbias_ref, act_ref, 0)
    xext_ref[0:8, :] = xext_ref[L:L + 8, :]

    r_i = _iota2((L, L), 0)
    c_i = _iota2((L, L), 1)
    causal = c_i <= r_i
    tri = jnp.where(causal, 1.0, 0.0).astype(BF16)
    dtv, da = _dt_terms(dt_ref, dtb_ref, alog_ref)
    acum = _dot_exact_l(tri, da)
    stacked = _split3(jnp.concatenate([acum, dtv], axis=0))
    if with_y:
        acum_t = acum.T
        lane = _iota2((L, LANES), 1)

    for g in range(GROUPS):
        cols = slice(g * GROUP_W, (g + 1) * GROUP_W)
        ex = _dot_exact(stacked, e64_ref[:, cols])
        acx = ex[0:L, :]
        dtx = ex[L:2 * L, :]
        xs_g = act_ref[:, cols]
        b_g = act_ref[:, INNER + g * STATE:INNER + (g + 1) * STATE]
        xdt = xs_g * dtx
        aend = acx[L - 1:L, :]
        xdtw = _pad_rows(xdt * jnp.exp(aend - acx), LANES).astype(BF16)
        b_t = _pad_rows(b_g, LANES).T
        h_old = ht_ref[:, cols]
        ht_ref[:, cols] = h_old * jnp.exp(aend) + _dot(b_t.astype(BF16), xdtw)

        if with_y:
            c_g = act_ref[:, INNER + GROUPS * STATE + g * STATE:
                          INNER + GROUPS * STATE + (g + 1) * STATE].astype(BF16)
            cb = _dot_nt(c_g, b_g.astype(BF16))
            yoff = _dot(c_g, h_old.astype(BF16)) * jnp.exp(acx)
            yds = []
            for q in range(GROUP_W // LANES):
                ms_ = []
                for h in (g * 8 + 2 * q, g * 8 + 2 * q + 1):
                    seg = acum[:, h:h + 1] - acum_t[h:h + 1, :]
                    dec = jnp.where(causal, jnp.exp(seg), 0.0)
                    ms_.append((cb * dec).astype(BF16))
                m_cat = jnp.concatenate(ms_, axis=1)
                xp = xdt[:, q * LANES:(q + 1) * LANES]
                r_m = jnp.concatenate([jnp.where(lane < HEADDIM, xp, 0.0),
                                       jnp.where(lane >= HEADDIM, xp, 0.0)], axis=0).astype(BF16)
                yds.append(_dot(m_cat, r_m))
            y_g = jnp.concatenate(yds, axis=1) + yoff + xs_g * dskip_ref[:, cols]
            _gate_norm_store(y_g, zs_ref[:, cols], nw_ref[:, cols], y_ref, cols)

    if with_y:
        uext_ref[PU_BASE:PU_BASE + L, :] = u_ref[...]
        _pool_prompt(uext_ref, l1_ref, l2_ref, l3_ref, pooled_ref, L)
        uext_ref[POOL_MAX:PU_BASE, :] = uext_ref[L + POOL_MAX:L + PU_BASE, :]

        @pl.when(c == nc - 1)
        def _():
            for k in range(INNER // LANES):
                hfin_ref[k * LANES:(k + 1) * LANES, :] = ht_ref[:, k * LANES:(k + 1) * LANES].T
            ctail_ref[...] = xext_ref[0:8, :]
            ptail_ref[...] = uext_ref[POOL_MAX:PU_BASE, :]
    else:
        hfin_ref[...] = ht_ref[...]
        ctail_ref[...] = xext_ref[0:8, :]
        ptail_ref[...] = u_ref[...]
        y_ref[...] = jnp.zeros(y_ref.shape, y_ref.dtype)
        pooled_ref[...] = jnp.zeros(pooled_ref.shape, pooled_ref.dtype)


def _shifted_tile(buf_ref, new_ref, j, cols_buf, cols_new):
    rows = slice(j * TS, (j + 1) * TS)
    old = buf_ref[CHUNK + j * TS:CHUNK + (j + 1) * TS, cols_buf]
    new = new_ref[rows, cols_new]
    is_last = _iota2(new.shape, 0) == TS - 1
    buf_ref[rows, cols_buf] = pltpu.roll(jnp.where(is_last, old, new), 1, 0)


def _seq_prompt_kernel(zs_ref, u_ref, xs_ref, bc_ref, dt_ref, h0_ref, ct0_ref, pt0_ref, e64_ref,
                       cw_ref, cbias_ref, dtb_ref, alog_ref, dskip_ref, nw_ref,
                       y_ref, pooled_ref, hfin_ref, ctail_ref, ptail_ref,
                       ht_ref, xbuf_ref, ubuf_ref, act_ref):
    L = CHUNK
    c = pl.program_id(1)
    nc = pl.num_programs(1)
    hist = CONV_K - 1
    all_cols = slice(None)

    @pl.when(c == 0)
    def _():
        ht_ref[...] = h0_ref[...]
        xbuf_ref[L:2 * L, :] = jnp.zeros((L, CONV_DIM), F32)
        ubuf_ref[L:2 * L, :] = jnp.zeros((L, D_MODEL), F32)
        for j in range(TPC - hist, TPC):
            r = L + j * TS + TS - 1
            xbuf_ref[r:r + 1, :] = ct0_ref[j - TS:j - TS + 1, :]
        for j in range(1, TPC):
            r = L + j * TS + TS - 1
            ubuf_ref[r:r + 1, :] = pt0_ref[j:j + 1, :]

    for j in range(TPC - hist, TPC):
        _shifted_tile(xbuf_ref, xs_ref, j, slice(0, INNER), all_cols)
        _shifted_tile(xbuf_ref, bc_ref, j, slice(INNER, CONV_DIM), all_cols)
    xbuf_ref[L:2 * L, 0:INNER] = xs_ref[...]
    xbuf_ref[L:2 * L, INNER:CONV_DIM] = bc_ref[...]
    for cs in range(CONV_DIM // GROUP_W):
        cols = slice(cs * GROUP_W, (cs + 1) * GROUP_W)
        acc = cbias_ref[:, cols] + xbuf_ref[L - hist * TS:2 * L - hist * TS, cols] * cw_ref[0:1, cols]
        for k in range(1, CONV_K):
            back = (hist - k) * TS
            acc = acc + xbuf_ref[L - back:2 * L - back, cols] * cw_ref[k:k + 1, cols]
        act_ref[:, cols] = _silu(acc)

    t_r = _time_of(_iota2((L, L), 0))
    t_c = _time_of(_iota2((L, L), 1))
    causal = t_c <= t_r
    tri = jnp.where(causal, 1.0, 0.0).astype(BF16)
    dtv, da = _dt_terms(dt_ref, dtb_ref, alog_ref)
    acum = _dot_exact_l(tri, da)
    stacked = _split3(jnp.concatenate([acum, dtv], axis=0))
    acum_t = acum.T
    lane = _iota2((L, LANES), 1)

    for g in range(GROUPS):
        cols = slice(g * GROUP_W, (g + 1) * GROUP_W)
        ex = _dot_exact(stacked, e64_ref[:, cols])
        acx = ex[0:L, :]
        dtx = ex[L:2 * L, :]
        xs_g = act_ref[:, cols]
        b_g = act_ref[:, INNER + g * STATE:INNER + (g + 1) * STATE]
        xdt = xs_g * dtx
        aend = acx[L - 1:L, :]
        xdtw = (xdt * jnp.exp(aend - acx)).astype(BF16)
        h_old = ht_ref[:, cols]
        ht_ref[:, cols] = h_old * jnp.exp(aend) + _dot(b_g.T.astype(BF16), xdtw)

        c_g = act_ref[:, INNER + GROUPS * STATE + g * STATE:
                      INNER + GROUPS * STATE + (g + 1) * STATE].astype(BF16)
        cb = _dot_nt(c_g, b_g.astype(BF16))
        yoff = _dot(c_g, h_old.astype(BF16)) * jnp.exp(acx)
        yds = []
        for q in range(GROUP_W // LANES):
            ms_ = []
            for h in (g * 8 + 2 * q, g * 8 + 2 * q + 1):
                seg = acum[:, h:h + 1] - acum_t[h:h + 1, :]
                dec = jnp.where(causal, jnp.exp(seg), 0.0)
                ms_.append((cb * dec).astype(BF16))
            m_cat = jnp.concatenate(ms_, axis=1)
            xp = xdt[:, q * LANES:(q + 1) * LANES]
            r_m = jnp.concatenate([jnp.where(lane < HEADDIM, xp, 0.0),
                                   jnp.where(lane >= HEADDIM, xp, 0.0)], axis=0).astype(BF16)
            yds.append(_dot(m_cat, r_m))
        y_g = jnp.concatenate(yds, axis=1) + yoff + xs_g * dskip_ref[:, cols]
        _gate_norm_store(y_g, zs_ref[:, cols], nw_ref[:, cols], y_ref, cols)

    for gi, w in enumerate(POOL_WINDOWS):
        cols = slice(gi * POOL_GW, (gi + 1) * POOL_GW)
        for j in range(TPC - (w - 1), TPC):
            _shifted_tile(ubuf_ref, u_ref, j, cols, cols)
    ubuf_ref[L:2 * L, :] = u_ref[...]
    for gi, w in enumerate(POOL_WINDOWS):
        cols = slice(gi * POOL_GW, (gi + 1) * POOL_GW)
        lo = 2 * L - L - (w - 1) * TS
        s = ubuf_ref[lo:2 * L, cols]
        step = TS
        while step < w * TS:
            n = s.shape[0]
            s = s[step:n, :] + s[0:n - step, :]
            step *= 2
        pooled_ref[:, cols] = (s * (1.0 / w) - u_ref[:, cols]).astype(BF16)

    @pl.when(c == nc - 1)
    def _():
        for k in range(INNER // LANES):
            hfin_ref[k * LANES:(k + 1) * LANES, :] = ht_ref[:, k * LANES:(k + 1) * LANES].T
        ctail_ref[...] = jnp.zeros(ctail_ref.shape, F32)
        ptail_ref[...] = jnp.zeros(ptail_ref.shape, F32)
        for j in range(TPC - hist, TPC):
            r = L + j * TS + TS - 1
            ctail_ref[j - TS:j - TS + 1, :] = xbuf_ref[r:r + 1, :]
        for j in range(1, TPC):
            r = L + j * TS + TS - 1
            ptail_ref[j:j + 1, :] = ubuf_ref[r:r + 1, :]


def _param_specs(index_map):
    return [
        pl.BlockSpec((CONV_K, CONV_DIM), index_map),
        pl.BlockSpec((1, CONV_DIM), index_map),
        pl.BlockSpec((1, LANES), index_map),
        pl.BlockSpec((1, LANES), index_map),
        pl.BlockSpec((1, INNER), index_map),
        pl.BlockSpec((1, INNER), index_map),
    ]


def _seq_scratch(rows):
    return [
        pltpu.VMEM((STATE, INNER), F32),
        pltpu.VMEM((8 + rows, CONV_DIM), F32),
        pltpu.VMEM((PU_BASE + rows, D_MODEL), F32),
        pltpu.VMEM((rows, CONV_DIM), F32),
        pltpu.VMEM((PU_BASE + rows, POOL_GW), F32),
        pltpu.VMEM((PU_BASE + rows, POOL_GW), F32),
        pltpu.VMEM((PU_BASE + rows, POOL_GW), F32),
    ]


def _seq_prompt(proj, dt, h0t, ct0, pt0, e64, params, *, batch, seq):
    nc = seq // CHUNK
    row = lambda b, c: b * nc + c
    const = lambda b, c: (0, 0)
    in_specs = [
        pl.BlockSpec((CHUNK, INNER), lambda b, c: (row(b, c), CB_ZS)),
        pl.BlockSpec((CHUNK, INNER), lambda b, c: (row(b, c), CB_U)),
        pl.BlockSpec((CHUNK, INNER), lambda b, c: (row(b, c), CB_XS)),
        pl.BlockSpec((CHUNK, 2 * GROUPS * STATE), lambda b, c: (row(b, c), CB_BC)),
        pl.BlockSpec((CHUNK, LANES), lambda b, c: (row(b, c), 0)),
        pl.BlockSpec((STATE, INNER), const),
        pl.BlockSpec((8, CONV_DIM), const),
        pl.BlockSpec((POOL_MAX, D_MODEL), const),
        pl.BlockSpec((LANES, INNER), const),
    ] + _param_specs(const)
    return pl.pallas_call(
        _seq_prompt_kernel,
        grid=(batch, nc),
        in_specs=in_specs,
        out_specs=[
            pl.BlockSpec((CHUNK, INNER), lambda b, c: (row(b, c), 0)),
            pl.BlockSpec((CHUNK, D_MODEL), lambda b, c: (row(b, c), 0)),
            pl.BlockSpec((None, INNER, STATE), lambda b, c: (b, 0, 0)),
            pl.BlockSpec((None, 8, CONV_DIM), lambda b, c: (b, 0, 0)),
            pl.BlockSpec((None, POOL_MAX, D_MODEL), lambda b, c: (b, 0, 0)),
        ],
        out_shape=[
            jax.ShapeDtypeStruct((batch * seq, INNER), BF16),
            jax.ShapeDtypeStruct((batch * seq, D_MODEL), BF16),
            jax.ShapeDtypeStruct((batch, INNER, STATE), F32),
            jax.ShapeDtypeStruct((batch, 8, CONV_DIM), F32),
            jax.ShapeDtypeStruct((batch, POOL_MAX, D_MODEL), F32),
        ],
        scratch_shapes=[
            pltpu.VMEM((STATE, INNER), F32),
            pltpu.VMEM((2 * CHUNK, CONV_DIM), F32),
            pltpu.VMEM((2 * CHUNK, D_MODEL), F32),
            pltpu.VMEM((CHUNK, CONV_DIM), F32),
        ],
        compiler_params=pltpu.CompilerParams(
            dimension_semantics=("arbitrary", "arbitrary"),
            vmem_limit_bytes=VMEM_LIMIT),
        name="seq_prompt",
    )(proj, proj, proj, proj, dt, h0t, ct0, pt0, e64, *params)


def _meta_state(proj, dt, e64, params, *, row_block):
    const = lambda b, c: (0, 0)
    zeros = functools.partial(jnp.zeros, dtype=F32)
    in_specs = [
        pl.BlockSpec((N_META, INNER), lambda b, c: (row_block, CB_ZS)),
        pl.BlockSpec((N_META, INNER), lambda b, c: (row_block, CB_U)),
        pl.BlockSpec((N_META, INNER), lambda b, c: (row_block, CB_XS)),
        pl.BlockSpec((N_META, 2 * GROUPS * STATE), lambda b, c: (row_block, CB_BC)),
        pl.BlockSpec((N_META, LANES), lambda b, c: (row_block, 0)),
        pl.BlockSpec((STATE, INNER), const),
        pl.BlockSpec((8, CONV_DIM), const),
        pl.BlockSpec((POOL_MAX, D_MODEL), const),
        pl.BlockSpec((LANES, INNER), const),
    ] + _param_specs(const)
    outs = pl.pallas_call(
        functools.partial(_seq_natural_kernel, with_y=False),
        grid=(1, 1),
        in_specs=in_specs,
        out_specs=[
            pl.BlockSpec((N_META, INNER), const),
            pl.BlockSpec((N_META, D_MODEL), const),
            pl.BlockSpec((STATE, INNER), const),
            pl.BlockSpec((8, CONV_DIM), const),
            pl.BlockSpec((POOL_MAX, D_MODEL), const),
        ],
        out_shape=[
            jax.ShapeDtypeStruct((N_META, INNER), BF16),
            jax.ShapeDtypeStruct((N_META, D_MODEL), BF16),
            jax.ShapeDtypeStruct((STATE, INNER), F32),
            jax.ShapeDtypeStruct((8, CONV_DIM), F32),
            jax.ShapeDtypeStruct((POOL_MAX, D_MODEL), F32),
        ],
        scratch_shapes=_seq_scratch(N_META),
        compiler_params=pltpu.CompilerParams(
            dimension_semantics=("arbitrary", "arbitrary"),
            vmem_limit_bytes=VMEM_LIMIT),
        name="meta_state",
    )(proj, proj, proj, proj, dt, zeros((STATE, INNER)), zeros((8, CONV_DIM)),
      zeros((POOL_MAX, D_MODEL)), e64, *params)
    return outs[2], outs[3], outs[4]


def _seq_sample_kernel(zs_ref, u_ref, xs_ref, bc_ref, dt_ref, chist_ref, phist_ref, h0_ref, e64_ref,
                       cw_ref, cbias_ref, dtb_ref, alog_ref, dskip_ref, nw_ref,
                       y_ref, pooled_ref, hnew_ref, cnew_ref, pnew_ref,
                       xext_ref, uext_ref, act_ref, dv_ref, s_ref):
    R = xs_ref.shape[0]
    T = R // SEQ_BLOCK
    SB = SEQ_BLOCK
    HIST = CONV_K - 1
    PH = POOL_MAX - 1
    half = R

    for k in range(HIST):
        xext_ref[k * SB:(k + 1) * SB, :] = chist_ref[k]
    xext_ref[HIST * SB:HIST * SB + R, 0:INNER] = xs_ref[...]
    xext_ref[HIST * SB:HIST * SB + R, INNER:CONV_DIM] = bc_ref[...]
    for cs in range(CONV_DIM // GROUP_W):
        cols = slice(cs * GROUP_W, (cs + 1) * GROUP_W)
        acc = cbias_ref[:, cols] + xext_ref[0:R, cols] * cw_ref[0:1, cols]
        for k in range(1, CONV_K):
            acc = acc + xext_ref[k * SB:k * SB + R, cols] * cw_ref[k:k + 1, cols]
        act_ref[:, cols] = _silu(acc)
    for k in range(HIST):
        cnew_ref[k] = xext_ref[R + k * SB:R + (k + 1) * SB, :]

    seq_of = lambda x: jnp.bitwise_and(x, SB - 1)
    step_of = lambda x: _div_pow2(x, SB)
    r_i = _iota2((R, R), 0)
    c_i = _iota2((R, R), 1)
    same = seq_of(r_i) == seq_of(c_i)
    tri = jnp.where(same & (step_of(c_i) <= step_of(r_i)), 1.0, 0.0).astype(BF16)
    ones_bd = jnp.where(same, 1.0, 0.0).astype(BF16)
    dtv, da = _dt_terms(dt_ref, dtb_ref, alog_ref)
    da_parts = _split3(da)
    acum = _dot(tri, da_parts[0]) + _dot(tri, da_parts[1]) + _dot(tri, da_parts[2])
    a_end = (_dot(ones_bd, da_parts[0]) + _dot(ones_bd, da_parts[1])
             + _dot(ones_bd, da_parts[2]))

    acum_t2 = jnp.concatenate([acum, acum], axis=0).T
    aend_t = _pad_rows(a_end, LANES).T

    r_s = _iota2((LANES, SB * STATE), 0)
    c_s = _iota2((LANES, SB * STATE), 1)
    sel = jnp.where(r_s == _div_pow2(c_s, STATE), 1.0, 0.0).astype(BF16)
    dv_ref[...] = jnp.exp(_dot_exact(_split3(aend_t), sel))

    lane2 = _iota2((R, 2 * R), 1)
    r_2 = _iota2((R, 2 * R), 0)
    c_2 = jnp.bitwise_and(lane2, R - 1)
    mask2 = (seq_of(r_2) == seq_of(c_2)) & (step_of(c_2) <= step_of(r_2))
    first_half = lane2 < half
    lane = _iota2((R, LANES), 1)
    r_b = _iota2((R, SB * STATE), 0)
    c_b = _iota2((R, SB * STATE), 1)
    bd_sel = seq_of(r_b) == _div_pow2(c_b, STATE)
    stacked = _split3(jnp.concatenate([acum, dtv, a_end], axis=0))

    for g in range(GROUPS):
        cols = slice(g * GROUP_W, (g + 1) * GROUP_W)
        ex = _dot_exact(stacked, e64_ref[:, cols])
        acx = ex[0:R, :]
        dtx = ex[R:2 * R, :]
        aendx = ex[2 * R:3 * R, :]
        xs_g = act_ref[:, cols]
        b_g = act_ref[:, INNER + g * STATE:INNER + (g + 1) * STATE]
        c_f = act_ref[:, INNER + GROUPS * STATE + g * STATE:INNER + GROUPS * STATE + (g + 1) * STATE]
        c_g = c_f.astype(BF16)
        xdt = xs_g * dtx
        xdtw = xdt * jnp.exp(aendx - acx)

        c_big = jnp.where(bd_sel, jnp.concatenate([c_f] * SB, axis=1), 0.0).astype(BF16)
        h_cat = jnp.concatenate(
            [h0_ref[i, g * GROUP_W:(g + 1) * GROUP_W, :].astype(BF16) for i in range(SB)], axis=1)
        yoff = _dot_nt(c_big, h_cat)

        xt = _pad_rows(xdtw, LANES).T
        b_big = jnp.where(bd_sel, jnp.concatenate([b_g] * SB, axis=1), 0.0)
        b_big = _pad_rows(b_big, LANES)
        s_ref[...] = _dot(xt.astype(BF16), b_big.astype(BF16))
        for i in range(SB):
            for hh in range(GROUP_W // HEADDIM):
                h = g * (GROUP_W // HEADDIM) + hh
                rows = slice(h * HEADDIM, (h + 1) * HEADDIM)
                hnew_ref[i, rows, :] = (
                    h0_ref[i, rows, :] * dv_ref[h:h + 1, i * STATE:(i + 1) * STATE]
                    + s_ref[hh * HEADDIM:(hh + 1) * HEADDIM, i * STATE:(i + 1) * STATE])

        cb2 = _dot_nt(c_g, jnp.concatenate([b_g, b_g], axis=0).astype(BF16))
        yds = []
        for q in range(GROUP_W // LANES):
            ha = g * 8 + 2 * q
            colc = jnp.where(first_half, acum[:, ha:ha + 1], acum[:, ha + 1:ha + 2])
            rowc = jnp.where(first_half[0:1, :], acum_t2[ha:ha + 1, :], acum_t2[ha + 1:ha + 2, :])
            dec = jnp.where(mask2, jnp.exp(colc - rowc), 0.0)
            m_p = (cb2 * dec).astype(BF16)
            xp = xdt[:, q * LANES:(q + 1) * LANES]
            r_m = jnp.concatenate([jnp.where(lane < HEADDIM, xp, 0.0),
                                   jnp.where(lane >= HEADDIM, xp, 0.0)], axis=0).astype(BF16)
            yds.append(_dot(m_p, r_m))
        y_g = jnp.concatenate(yds, axis=1) + yoff * jnp.exp(acx) + xs_g * dskip_ref[:, cols]
        _gate_norm_store(y_g, zs_ref[:, cols], nw_ref[:, cols], y_ref, cols)

    for k in range(PH):
        uext_ref[k * SB:(k + 1) * SB, :] = phist_ref[k]
    uext_ref[PH * SB:PH * SB + R, :] = u_ref[...]
    for k in range(PH):
        pnew_ref[k] = uext_ref[(T + k) * SB:(T + k + 1) * SB, :]
    for gi, w in enumerate(POOL_WINDOWS):
        cols = slice(gi * POOL_GW, (gi + 1) * POOL_GW)
        u_cur = u_ref[:, cols]
        s = u_cur
        for k in range(1, w):
            s = s + uext_ref[(PH - k) * SB:(PH - k) * SB + R, cols]
        pooled_ref[:, cols] = (s * (1.0 / w) - u_cur).astype(BF16)


def _seq_sample(proj, dt, conv_hist, pool_hist, h0, e64, params, *, nseq, t_new):
    rb = SEQ_BLOCK * t_new
    assert 2 * rb == LANES
    const = lambda s: (0, 0)
    in_specs = [
        pl.BlockSpec((rb, INNER), lambda s: (s, CB_ZS)),
        pl.BlockSpec((rb, INNER), lambda s: (s, CB_U)),
        pl.BlockSpec((rb, INNER), lambda s: (s, CB_XS)),
        pl.BlockSpec((rb, 2 * GROUPS * STATE), lambda s: (s, CB_BC)),
        pl.BlockSpec((rb, LANES), lambda s: (s, 0)),
        pl.BlockSpec((CONV_K - 1, SEQ_BLOCK, CONV_DIM), lambda s: (0, s, 0)),
        pl.BlockSpec((POOL_MAX - 1, SEQ_BLOCK, D_MODEL), lambda s: (0, s, 0)),
        pl.BlockSpec((SEQ_BLOCK, INNER, STATE), lambda s: (s, 0, 0)),
        pl.BlockSpec((LANES, INNER), const),
    ] + _param_specs(const)
    return pl.pallas_call(
        _seq_sample_kernel,
        grid=(nseq // SEQ_BLOCK,),
        in_specs=in_specs,
        out_specs=[
            pl.BlockSpec((rb, INNER), lambda s: (s, 0)),
            pl.BlockSpec((rb, D_MODEL), lambda s: (s, 0)),
            pl.BlockSpec((SEQ_BLOCK, INNER, STATE), lambda s: (s, 0, 0)),
            pl.BlockSpec((CONV_K - 1, SEQ_BLOCK, CONV_DIM), lambda s: (0, s, 0)),
            pl.BlockSpec((POOL_MAX - 1, SEQ_BLOCK, D_MODEL), lambda s: (0, s, 0)),
        ],
        out_shape=[
            jax.ShapeDtypeStruct((nseq * t_new, INNER), BF16),
            jax.ShapeDtypeStruct((nseq * t_new, D_MODEL), BF16),
            jax.ShapeDtypeStruct((nseq, INNER, STATE), F32),
            jax.ShapeDtypeStruct((CONV_K - 1, nseq, CONV_DIM), F32),
            jax.ShapeDtypeStruct((POOL_MAX - 1, nseq, D_MODEL), F32),
        ],
        scratch_shapes=[
            pltpu.VMEM((SEQ_BLOCK * (CONV_K - 1 + t_new), CONV_DIM), F32),
            pltpu.VMEM((SEQ_BLOCK * (POOL_MAX - 1 + t_new), D_MODEL), F32),
            pltpu.VMEM((rb, CONV_DIM), F32),
            pltpu.VMEM((LANES, SEQ_BLOCK * STATE), F32),
            pltpu.VMEM((GROUP_W, SEQ_BLOCK * STATE), F32),
        ],
        compiler_params=pltpu.CompilerParams(
            dimension_semantics=("arbitrary",),
            vmem_limit_bytes=VMEM_LIMIT),
        name="seq_sample",
    )(proj, proj, proj, proj, dt, conv_hist, pool_hist, h0, e64, *params)


def _merge_kernel(y_ref, pooled_ref, zp_ref, g1_ref, g2_ref, wmix_ref, bmix_ref, pscale_ref,
                  wps_ref, wpp_ref, o_ref, pout_ref):
    for g in range(len(POOL_WINDOWS)):
        cols = slice(g * POOL_GW, (g + 1) * POOL_GW)
        mixed = _dot(pooled_ref[:, cols], wmix_ref[g]) + bmix_ref[:, cols]
        pout_ref[:, cols] = (mixed * pscale_ref[:, cols] * _silu(zp_ref[:, cols])).astype(BF16)
    bs = _dot(y_ref[...], wps_ref[...])
    bp = _dot(pout_ref[...], wpp_ref[...])
    o_ref[...] = (_sigmoid(g1_ref[...]) * bs + _sigmoid(g2_ref[...]) * bp).astype(BF16)


def _merge(y, pooled, proj, wmix, bmix, pscale, wps, wpp, *, m, tm=256):
    const2 = lambda i: (0, 0)
    resident = dict(pipeline_mode=pl.Buffered(1))
    return pl.pallas_call(
        _merge_kernel,
        grid=(m // tm,),
        in_specs=[
            pl.BlockSpec((tm, INNER), lambda i: (i, 0)),
            pl.BlockSpec((tm, D_MODEL), lambda i: (i, 0)),
            pl.BlockSpec((tm, D_MODEL), lambda i: (i, CB_ZP)),
            pl.BlockSpec((tm, D_MODEL), lambda i: (i, CB_G1)),
            pl.BlockSpec((tm, D_MODEL), lambda i: (i, CB_G2)),
            pl.BlockSpec((len(POOL_WINDOWS), POOL_GW, POOL_GW), lambda i: (0, 0, 0), **resident),
            pl.BlockSpec((1, D_MODEL), const2),
            pl.BlockSpec((1, D_MODEL), const2),
            pl.BlockSpec((INNER, D_MODEL), const2, **resident),
            pl.BlockSpec((D_MODEL, D_MODEL), const2, **resident),
        ],
        out_specs=pl.BlockSpec((tm, D_MODEL), lambda i: (i, 0)),
        out_shape=jax.ShapeDtypeStruct((m, D_MODEL), BF16),
        scratch_shapes=[pltpu.VMEM((tm, D_MODEL), BF16)],
        compiler_params=pltpu.CompilerParams(
            dimension_semantics=("arbitrary",),
            vmem_limit_bytes=VMEM_LIMIT),
        name="merge",
    )(y, pooled, proj, proj, proj, wmix, bmix, pscale, wps, wpp)


def _out_kernel(m_ref, h_ref, wout_ref, fw_ref, o_ref, *, tile_major):
    if tile_major:
        unperm = _tile_major_perm(transpose=True)
        merged = jnp.concatenate(
            [_dot(unperm, m_ref[c * CHUNK:(c + 1) * CHUNK, :]).astype(BF16)
             for c in range(m_ref.shape[0] // CHUNK)], axis=0)
    else:
        merged = m_ref[...]
    hn = h_ref[...] + _dot(merged, wout_ref[...])
    ms = jnp.mean(hn * hn, axis=-1, keepdims=True)
    o_ref[...] = (hn * lax.rsqrt(ms + EPS)) * fw_ref[...]


def _out(merged, h, wout, fw, *, m, tm=512, tile_major=False):
    return pl.pallas_call(
        functools.partial(_out_kernel, tile_major=tile_major),
        grid=(m // tm,),
        in_specs=[
            pl.BlockSpec((tm, D_MODEL), lambda i: (i, 0)),
            pl.BlockSpec((tm, D_MODEL), lambda i: (i, 0)),
            pl.BlockSpec((D_MODEL, D_MODEL), lambda i: (0, 0)),
            pl.BlockSpec((1, D_MODEL), lambda i: (0, 0)),
        ],
        out_specs=pl.BlockSpec((tm, D_MODEL), lambda i: (i, 0)),
        out_shape=jax.ShapeDtypeStruct((m, D_MODEL), F32),
        compiler_params=pltpu.CompilerParams(
            dimension_semantics=("arbitrary",),
            vmem_limit_bytes=VMEM_LIMIT),
        name="out",
    )(merged, h, wout, fw)


def kernel(x_prompt, x_sample, state_conv, state_ssm, state_pool, meta_tokens, norm_w, w_in,
           conv_w, conv_b, dt_bias, a_log, d_skip, ssd_norm_w, w_proj_ssd, pool_mix_w,
           pool_mix_b, pool_scale, w_proj_pool, w_out, final_norm_w):
    batch, seq, _ = x_prompt.shape
    nseq, t_new, _ = x_sample.shape
    assert norm_w.shape[0] == 1, "single layer"
    assert w_in.shape == (1, D_MODEL, IN_ZP + 4 * D_MODEL)

    w_main, w_dt = _prep_weights(jnp.swapaxes(w_in, 1, 2)[0])
    wps = w_proj_ssd[0].astype(BF16)
    wpp = w_proj_pool[0].astype(BF16)
    wout = w_out[0].astype(BF16)
    wmix = pool_mix_w[0].astype(BF16)
    lane_pad = lambda v: jnp.pad(v.reshape(1, HEADS), ((0, 0), (0, LANES - HEADS)))
    params = (conv_w[0], conv_b[0].reshape(1, CONV_DIM), lane_pad(dt_bias[0]), lane_pad(a_log[0]),
              jnp.repeat(d_skip[0], HEADDIM).reshape(1, INNER), ssd_norm_w[0].reshape(1, INNER))
    e64 = (jnp.arange(LANES)[:, None] == (jnp.arange(INNER)[None, :] // HEADDIM)).astype(BF16)
    nw = norm_w[0].reshape(1, D_MODEL)
    fw = final_norm_w.reshape(1, D_MODEL)
    bmix = pool_mix_b[0].reshape(1, D_MODEL)
    pscale = pool_scale[0].reshape(1, D_MODEL)

    xp = x_prompt.reshape(batch * seq, D_MODEL)
    m_s = nseq * t_new
    nblk = nseq // SEQ_BLOCK
    xs_tm = x_sample.reshape(nblk, SEQ_BLOCK, t_new, D_MODEL).transpose(0, 2, 1, 3)
    x_sm = jnp.concatenate([xs_tm.reshape(m_s, D_MODEL), meta_tokens], axis=0)

    proj_p, dt_p = _proj(xp, nw, w_main, w_dt, tm=1024, tile_major=True)
    proj_s, dt_s = _proj(x_sm, nw, w_main, w_dt, tm=m_s + N_META)

    h0t, ct0, pt0 = _meta_state(proj_s, dt_s, e64, params, row_block=m_s // N_META)

    y_p, pooled_p, ssm_p, conv_p, pool_p = _seq_prompt(
        proj_p, dt_p, h0t, ct0, pt0, e64, params, batch=batch, seq=seq)
    y_s, pooled_s, ssm_s, conv_s, pool_s = _seq_sample(
        proj_s, dt_s, jnp.swapaxes(state_conv[0], 0, 1), jnp.swapaxes(state_pool[0], 0, 1),
        state_ssm[0].reshape(nseq, INNER, STATE), e64, params, nseq=nseq, t_new=t_new)

    merged_p = _merge(y_p, pooled_p, proj_p, wmix, bmix, pscale, wps, wpp, m=batch * seq)
    merged_s = _merge(y_s, pooled_s, proj_s, wmix, bmix, pscale, wps, wpp, m=m_s)
    out_p = _out(merged_p, xp, wout, fw, m=batch * seq, tile_major=True)
    out_s = _out(merged_s, x_sm, wout, fw, m=m_s)

    return (out_p.reshape(batch, seq, D_MODEL),
            out_s.reshape(nblk, t_new, SEQ_BLOCK, D_MODEL).transpose(0, 2, 1, 3).reshape(
                nseq, t_new, D_MODEL),
            conv_p[:, 8 - (CONV_K - 1):][None].astype(state_conv.dtype),
            ssm_p.reshape(1, batch, HEADS, HEADDIM, STATE).astype(state_ssm.dtype),
            pool_p[:, 1:][None].astype(state_pool.dtype),
            jnp.swapaxes(conv_s, 0, 1)[None].astype(state_conv.dtype),
            ssm_s.reshape(1, nseq, HEADS, HEADDIM, STATE).astype(state_ssm.dtype),
            jnp.swapaxes(pool_s, 0, 1)[None].astype(state_pool.dtype))
```

```python
import functools

import jax
import jax.numpy as jnp
from jax import lax
from jax.experimental import pallas as pl
from jax.experimental.pallas import tpu as pltpu

F32 = jnp.float32
BF16 = jnp.bfloat16

D_MODEL = 2048
N_META = 16
EPS = 1e-6
HEADS = 32
HEADDIM = 64
GROUPS = 4
STATE = 128
CONV_K = 4
INNER = HEADS * HEADDIM
GROUP_W = INNER // GROUPS
CONV_DIM = INNER + 2 * GROUPS * STATE
POOL_WINDOWS = (2, 4, 8, 16)
POOL_MAX = 16
POOL_GW = D_MODEL // len(POOL_WINDOWS)

IN_XBC = INNER
IN_DT = IN_XBC + CONV_DIM
IN_ZP = IN_DT + HEADS
N_MAIN = 5 * D_MODEL + CONV_DIM
CB_ZS, CB_ZP, CB_U, CB_G1, CB_G2, CB_XS = 0, 1, 2, 3, 4, 5
CB_BC = 12

LANES = 128
CHUNK = 128
SEQ_BLOCK = 8
VMEM_LIMIT = 60 * 1024 * 1024
PREP_TN = 1024
PU_BASE = 2 * POOL_MAX
TPC = 16
TS = CHUNK // TPC


def _silu(x):
    hx = 0.5 * x
    return hx + hx * jnp.tanh(hx)


def _sigmoid(x):
    return 0.5 + 0.5 * jnp.tanh(0.5 * x)


def _dot(a, b):
    return jnp.dot(a, b, preferred_element_type=F32)


def _dot_nt(a, b):
    return lax.dot_general(a, b, (((1,), (1,)), ((), ())), preferred_element_type=F32)


def _split3(a):
    a1 = a.astype(BF16)
    r1 = a - a1.astype(F32)
    a2 = r1.astype(BF16)
    a3 = (r1 - a2.astype(F32)).astype(BF16)
    return a1, a2, a3


def _dot_exact(parts, b):
    out = _dot(parts[0], b)
    for p in parts[1:]:
        out = out + _dot(p, b)
    return out


def _dot_exact_l(a01, x):
    parts = _split3(x)
    out = _dot(a01, parts[0])
    for p in parts[1:]:
        out = out + _dot(a01, p)
    return out


def _iota2(shape, dim):
    return lax.broadcasted_iota(jnp.int32, shape, dim)


def _div_pow2(x, d):
    assert d & (d - 1) == 0
    return jnp.right_shift(x, d.bit_length() - 1)


def _pad_rows(x, rows):
    if x.shape[0] == rows:
        return x
    return jnp.concatenate([x, jnp.zeros((rows - x.shape[0], x.shape[1]), x.dtype)], axis=0)


def _prep_proj_kernel(x_ref, nw_ref, a_ref, d_ref, o_ref, dt_ref, wm_ref, wdt_ref, xn_ref, *, sub_x, sub_w):
    @pl.when(pl.program_id(0) == 0)
    def _():
        def body(r, carry):
            rows = pl.ds(pl.multiple_of(r * sub_x, sub_x), sub_x)
            x = x_ref[rows, :]
            ms = jnp.mean(x * x, axis=-1, keepdims=True)
            xn_ref[rows, :] = ((x * lax.rsqrt(ms + EPS)) * nw_ref[...]).astype(BF16)
            return carry

        lax.fori_loop(0, x_ref.shape[0] // sub_x, body, 0)
        row = _iota2((LANES, D_MODEL), 0)
        wdt_ref[...] = jnp.where(row < HEADS, d_ref[...], 0.0).astype(BF16)
        dt_ref[...] = _dot_nt(xn_ref[...], wdt_ref[...])

    def cast(r, carry):
        rows = pl.ds(pl.multiple_of(r * sub_w, sub_w), sub_w)
        wm_ref[rows, :] = a_ref[rows, :].astype(BF16)
        return carry

    lax.fori_loop(0, PREP_TN // sub_w, cast, 0)
    o_ref[...] = _dot_nt(xn_ref[...], wm_ref[...])


def _prep_proj(x, norm_w, w_t):
    m = x.shape[0]

    def a_map(c):
        src = jnp.where(c < 2, c * PREP_TN,
                        jnp.where(c < 10, IN_ZP + (c - 2) * PREP_TN, IN_XBC + (c - 10) * PREP_TN))
        return (pl.multiple_of(src, 32), 0)

    return pl.pallas_call(
        functools.partial(_prep_proj_kernel, sub_x=16 if m % 128 else 128, sub_w=256),
        grid=(N_MAIN // PREP_TN,),
        in_specs=[
            pl.BlockSpec((m, D_MODEL), lambda c: (0, 0), pipeline_mode=pl.Buffered(1)),
            pl.BlockSpec((1, D_MODEL), lambda c: (0, 0)),
            pl.BlockSpec((pl.Element(PREP_TN), pl.Element(D_MODEL)), a_map),
            pl.BlockSpec((pl.Element(LANES), pl.Element(D_MODEL)), lambda c: (IN_DT, 0)),
        ],
        out_specs=[
            pl.BlockSpec((m, PREP_TN), lambda c: (0, c)),
            pl.BlockSpec((m, LANES), lambda c: (0, 0)),
            pl.BlockSpec((PREP_TN, D_MODEL), lambda c: (c, 0)),
            pl.BlockSpec((LANES, D_MODEL), lambda c: (0, 0)),
        ],
        out_shape=[
            jax.ShapeDtypeStruct((m, N_MAIN), F32),
            jax.ShapeDtypeStruct((m, LANES), F32),
            jax.ShapeDtypeStruct((N_MAIN, D_MODEL), BF16),
            jax.ShapeDtypeStruct((LANES, D_MODEL), BF16),
        ],
        scratch_shapes=[pltpu.VMEM((m, D_MODEL), BF16)],
        compiler_params=pltpu.CompilerParams(
            dimension_semantics=("arbitrary",), vmem_limit_bytes=VMEM_LIMIT),
        name="prep_proj",
    )(x, norm_w, w_t, w_t)


def _time_of(row):
    return jnp.bitwise_and(row, TS - 1) * TPC + jnp.right_shift(row, TS.bit_length() - 1)


def _tile_major_perm(transpose=False):
    r = _iota2((CHUNK, CHUNK), 0)
    c = _iota2((CHUNK, CHUNK), 1)
    hit = (r == _time_of(c)) if transpose else (c == _time_of(r))
    return jnp.where(hit, 1.0, 0.0).astype(BF16)


def _proj_kernel(x_ref, nw_ref, w_ref, wdt_ref, o_ref, dt_ref, xn_ref, *, sub, tile_major):
    tm = x_ref.shape[0]

    @pl.when(pl.program_id(1) == 0)
    def _():
        def body(r, carry):
            rows = pl.ds(pl.multiple_of(r * sub, sub), sub)
            x = x_ref[rows, :]
            ms = jnp.mean(x * x, axis=-1, keepdims=True)
            xn = (x * lax.rsqrt(ms + EPS)) * nw_ref[...]
            xn_ref[rows, :] = xn.astype(BF16)
            return carry

        lax.fori_loop(0, tm // sub, body, 0)
        if tile_major:
            perm = _tile_major_perm()

            def permute(r, carry):
                rows = pl.ds(pl.multiple_of(r * CHUNK, CHUNK), CHUNK)
                xn_ref[rows, :] = _dot(perm, xn_ref[rows, :]).astype(BF16)
                return carry

            lax.fori_loop(0, tm // CHUNK, permute, 0)
        dt_ref[...] = _dot_nt(xn_ref[...], wdt_ref[...])

    o_ref[...] = _dot_nt(xn_ref[...], w_ref[...])


def _proj(x, norm_w, w_main, w_dt, *, tm, tn=1024, tile_major=False):
    m = x.shape[0]
    assert m % tm == 0 and N_MAIN % tn == 0
    sub = 16 if tm % 128 else 128
    return pl.pallas_call(
        functools.partial(_proj_kernel, sub=sub, tile_major=tile_major),
        grid=(m // tm, N_MAIN // tn),
        in_specs=[
            pl.BlockSpec((tm, D_MODEL), lambda i, j: (i, 0)),
            pl.BlockSpec((1, D_MODEL), lambda i, j: (0, 0)),
            pl.BlockSpec((tn, D_MODEL), lambda i, j: (j, 0)),
            pl.BlockSpec((LANES, D_MODEL), lambda i, j: (0, 0)),
        ],
        out_specs=[
            pl.BlockSpec((tm, tn), lambda i, j: (i, j)),
            pl.BlockSpec((tm, LANES), lambda i, j: (i, 0)),
        ],
        out_shape=[
            jax.ShapeDtypeStruct((m, N_MAIN), F32),
            jax.ShapeDtypeStruct((m, LANES), F32),
        ],
        scratch_shapes=[pltpu.VMEM((tm, D_MODEL), BF16)],
        compiler_params=pltpu.CompilerParams(
            dimension_semantics=("arbitrary", "arbitrary"),
            vmem_limit_bytes=VMEM_LIMIT),
        name="proj",
    )(x, norm_w, w_main, w_dt)


def _conv_silu(ext_ref, first, rows, cw_ref, cbias_ref, act_ref, row0):
    for cs in range(CONV_DIM // GROUP_W):
        cols = slice(cs * GROUP_W, (cs + 1) * GROUP_W)
        acc = cbias_ref[:, cols] + ext_ref[first:first + rows, cols] * cw_ref[0:1, cols]
        for k in range(1, CONV_K):
            acc = acc + ext_ref[first + k:first + k + rows, cols] * cw_ref[k:k + 1, cols]
        act_ref[row0:row0 + rows, cols] = _silu(acc)


def _dt_terms(dt_ref, dtb_ref, alog_ref):
    x = dt_ref[...] + dtb_ref[...]
    dtv = jnp.maximum(x, 0.0) + jnp.log1p(jnp.exp(-jnp.abs(x)))
    da = dtv * (-jnp.exp(alog_ref[...]))
    return dtv, da


def _gate_norm_store(y_g, zs, nw, y_ref, cols):
    y_g = y_g * _silu(zs)
    ms = jnp.mean(y_g * y_g, axis=-1, keepdims=True)
    y_ref[:, cols] = (y_g * lax.rsqrt(ms + EPS) * nw).astype(BF16)


def _pool_prompt(uext_ref, l1_ref, l2_ref, l3_ref, pooled_ref, rows):
    base, end = PU_BASE, PU_BASE + rows
    for gi, w in enumerate(POOL_WINDOWS):
        cols = slice(gi * POOL_GW, (gi + 1) * POOL_GW)
        levels = w.bit_length() - 1
        lo = base - 8 * (levels - 1)
        if levels == 1:
            s = uext_ref[base:end, cols] + uext_ref[base - 1:end - 1, cols]
        else:
            l1_ref[lo:end, :] = uext_ref[lo:end, cols] + uext_ref[lo - 1:end - 1, cols]
            if levels == 2:
                s = l1_ref[base:end, :] + l1_ref[base - 2:end - 2, :]
            else:
                lo2 = lo + 8
                l2_ref[lo2:end, :] = l1_ref[lo2:end, :] + l1_ref[lo2 - 2:end - 2, :]
                if levels == 3:
                    s = l2_ref[base:end, :] + l2_ref[base - 4:end - 4, :]
                else:
                    lo3 = lo2 + 8
                    l3_ref[lo3:end, :] = l2_ref[lo3:end, :] + l2_ref[lo3 - 4:end - 4, :]
                    s = l3_ref[base:end, :] + l3_ref[base - 8:end - 8, :]
        pooled_ref[:, cols] = (s * (1.0 / w) - uext_ref[base:end, cols]).astype(BF16)


def _seq_natural_kernel(zs_ref, u_ref, xs_ref, bc_ref, dt_ref, h0_ref, ct0_ref, pt0_ref, e64_ref,
                        cw_ref, cbias_ref, dtb_ref, alog_ref, dskip_ref, nw_ref,
                        y_ref, pooled_ref, hfin_ref, ctail_ref, ptail_ref,
                        ht_ref, xext_ref, uext_ref, act_ref, l1_ref, l2_ref, l3_ref, *, with_y):
    L = xs_ref.shape[0]
    c = pl.program_id(1)
    nc = pl.num_programs(1)

    @pl.when(c == 0)
    def _():
        ht_ref[...] = h0_ref[...]
        xext_ref[0:8, :] = ct0_ref[...]
        uext_ref[0:POOL_MAX, :] = jnp.zeros((POOL_MAX, D_MODEL), F32)
        uext_ref[POOL_MAX:PU_BASE, :] = pt0_ref[...]

    xext_ref[8:8 + L, 0:INNER] = xs_ref[...]
    xext_ref[8:8 + L, INNER:CONV_DIM] = bc_ref[...]
    _conv_silu(xext_ref, 8 - (CONV_K - 1), L, cw_ref, cbias_ref, act_ref, 0)
    xext_ref[0:8, :] = xext_ref[L:L + 8, :]

    r_i = _iota2((L, L), 0)
    c_i = _iota2((L, L), 1)
    causal = c_i <= r_i
    tri = jnp.where(causal, 1.0, 0.0).astype(BF16)
    dtv, da = _dt_terms(dt_ref, dtb_ref, alog_ref)
    acum = _dot_exact_l(tri, da)
    stacked = _split3(jnp.concatenate([acum, dtv], axis=0))
    if with_y:
        acum_t = acum.T
        lane = _iota2((L, LANES), 1)

    for g in range(GROUPS):
        cols = slice(g * GROUP_W, (g + 1) * GROUP_W)
        ex = _dot_exact(stacked, e64_ref[:, cols])
        acx = ex[0:L, :]
        dtx = ex[L:2 * L, :]
        xs_g = act_ref[:, cols]
        b_g = act_ref[:, INNER + g * STATE:INNER + (g + 1) * STATE]
        xdt = xs_g * dtx
        aend = acx[L - 1:L, :]
        xdtw = _pad_rows(xdt * jnp.exp(aend - acx), LANES).astype(BF16)
        b_t = _pad_rows(b_g, LANES).T
        h_old = ht_ref[:, cols]
        ht_ref[:, cols] = h_old * jnp.exp(aend) + _dot(b_t.astype(BF16), xdtw)

        if with_y:
            c_g = act_ref[:, INNER + GROUPS * STATE + g * STATE:
                          INNER + GROUPS * STATE + (g + 1) * STATE].astype(BF16)
            cb = _dot_nt(c_g, b_g.astype(BF16))
            yoff = _dot(c_g, h_old.astype(BF16)) * jnp.exp(acx)
            yds = []
            for q in range(GROUP_W // LANES):
                ms_ = []
                for h in (g * 8 + 2 * q, g * 8 + 2 * q + 1):
                    seg = acum[:, h:h + 1] - acum_t[h:h + 1, :]
                    dec = jnp.where(causal, jnp.exp(seg), 0.0)
                    ms_.append((cb * dec).astype(BF16))
                m_cat = jnp.concatenate(ms_, axis=1)
                xp = xdt[:, q * LANES:(q + 1) * LANES]
                r_m = jnp.concatenate([jnp.where(lane < HEADDIM, xp, 0.0),
                                       jnp.where(lane >= HEADDIM, xp, 0.0)], axis=0).astype(BF16)
                yds.append(_dot(m_cat, r_m))
            y_g = jnp.concatenate(yds, axis=1) + yoff + xs_g * dskip_ref[:, cols]
            _gate_norm_store(y_g, zs_ref[:, cols], nw_ref[:, cols], y_ref, cols)

    if with_y:
        uext_ref[PU_BASE:PU_BASE + L, :] = u_ref[...]
        _pool_prompt(uext_ref, l1_ref, l2_ref, l3_ref, pooled_ref, L)
        uext_ref[POOL_MAX:PU_BASE, :] = uext_ref[L + POOL_MAX:L + PU_BASE, :]

        @pl.when(c == nc - 1)
        def _():
            for k in range(INNER // LANES):
                hfin_ref[k * LANES:(k + 1) * LANES, :] = ht_ref[:, k * LANES:(k + 1) * LANES].T
            ctail_ref[...] = xext_ref[0:8, :]
            ptail_ref[...] = uext_ref[POOL_MAX:PU_BASE, :]
    else:
        hfin_ref[...] = ht_ref[...]
        ctail_ref[...] = xext_ref[0:8, :]
        ptail_ref[...] = u_ref[...]
        y_ref[...] = jnp.zeros(y_ref.shape, y_ref.dtype)
        pooled_ref[...] = jnp.zeros(pooled_ref.shape, pooled_ref.dtype)


def _shifted_tile(buf_ref, new_ref, j, cols_buf, cols_new):
    rows = slice(j * TS, (j + 1) * TS)
    old = buf_ref[CHUNK + j * TS:CHUNK + (j + 1) * TS, cols_buf]
    new = new_ref[rows, cols_new]
    is_last = _iota2(new.shape, 0) == TS - 1
    buf_ref[rows, cols_buf] = pltpu.roll(jnp.where(is_last, old, new), 1, 0)


def _seq_prompt_kernel(zs_ref, u_ref, xs_ref, bc_ref, dt_ref, h0_ref, ct0_ref, pt0_ref, e64_ref,
                       cw_ref, cbias_ref, dtb_ref, alog_ref, dskip_ref, nw_ref,
                       y_ref, pooled_ref, hfin_ref, ctail_ref, ptail_ref,
                       ht_ref, xbuf_ref, ubuf_ref, act_ref):
    L = CHUNK
    c = pl.program_id(1)
    nc = pl.num_programs(1)
    hist = CONV_K - 1
    all_cols = slice(None)

    @pl.when(c == 0)
    def _():
        ht_ref[...] = h0_ref[...]
        xbuf_ref[L:2 * L, :] = jnp.zeros((L, CONV_DIM), F32)
        ubuf_ref[L:2 * L, :] = jnp.zeros((L, D_MODEL), F32)
        for j in range(TPC - hist, TPC):
            r = L + j * TS + TS - 1
            xbuf_ref[r:r + 1, :] = ct0_ref[j - TS:j - TS + 1, :]
        for j in range(1, TPC):
            r = L + j * TS + TS - 1
            ubuf_ref[r:r + 1, :] = pt0_ref[j:j + 1, :]

    for j in range(TPC - hist, TPC):
        _shifted_tile(xbuf_ref, xs_ref, j, slice(0, INNER), all_cols)
        _shifted_tile(xbuf_ref, bc_ref, j, slice(INNER, CONV_DIM), all_cols)
    xbuf_ref[L:2 * L, 0:INNER] = xs_ref[...]
    xbuf_ref[L:2 * L, INNER:CONV_DIM] = bc_ref[...]
    for cs in range(CONV_DIM // GROUP_W):
        cols = slice(cs * GROUP_W, (cs + 1) * GROUP_W)
        acc = cbias_ref[:, cols] + xbuf_ref[L - hist * TS:2 * L - hist * TS, cols] * cw_ref[0:1, cols]
        for k in range(1, CONV_K):
            back = (hist - k) * TS
            acc = acc + xbuf_ref[L - back:2 * L - back, cols] * cw_ref[k:k + 1, cols]
        act_ref[:, cols] = _silu(acc)

    t_r = _time_of(_iota2((L, L), 0))
    t_c = _time_of(_iota2((L, L), 1))
    causal = t_c <= t_r
    tri = jnp.where(causal, 1.0, 0.0).astype(BF16)
    dtv, da = _dt_terms(dt_ref, dtb_ref, alog_ref)
    acum = _dot_exact_l(tri, da)
    stacked = _split3(jnp.concatenate([acum, dtv], axis=0))
    acum_t = acum.T
    lane = _iota2((L, LANES), 1)

    for g in range(GROUPS):
        cols = slice(g * GROUP_W, (g + 1) * GROUP_W)
        ex = _dot_exact(stacked, e64_ref[:, cols])
        acx = ex[0:L, :]
        dtx = ex[L:2 * L, :]
        xs_g = act_ref[:, cols]
        b_g = act_ref[:, INNER + g * STATE:INNER + (g + 1) * STATE]
        xdt = xs_g * dtx
        aend = acx[L - 1:L, :]
        xdtw = (xdt * jnp.exp(aend - acx)).astype(BF16)
        h_old = ht_ref[:, cols]
        ht_ref[:, cols] = h_old * jnp.exp(aend) + _dot(b_g.T.astype(BF16), xdtw)

        c_g = act_ref[:, INNER + GROUPS * STATE + g * STATE:
                      INNER + GROUPS * STATE + (g + 1) * STATE].astype(BF16)
        cb = _dot_nt(c_g, b_g.astype(BF16))
        yoff = _dot(c_g, h_old.astype(BF16)) * jnp.exp(acx)
        yds = []
        for q in range(GROUP_W // LANES):
            ms_ = []
            for h in (g * 8 + 2 * q, g * 8 + 2 * q + 1):
                seg = acum[:, h:h + 1] - acum_t[h:h + 1, :]
                dec = jnp.where(causal, jnp.exp(seg), 0.0)
                ms_.append((cb * dec).astype(BF16))
            m_cat = jnp.concatenate(ms_, axis=1)
            xp = xdt[:, q * LANES:(q + 1) * LANES]
            r_m = jnp.concatenate([jnp.where(lane < HEADDIM, xp, 0.0),
                                   jnp.where(lane >= HEADDIM, xp, 0.0)], axis=0).astype(BF16)
            yds.append(_dot(m_cat, r_m))
        y_g = jnp.concatenate(yds, axis=1) + yoff + xs_g * dskip_ref[:, cols]
        _gate_norm_store(y_g, zs_ref[:, cols], nw_ref[:, cols], y_ref, cols)

    for gi, w in enumerate(POOL_WINDOWS):
        cols = slice(gi * POOL_GW, (gi + 1) * POOL_GW)
        for j in range(TPC - (w - 1), TPC):
            _shifted_tile(ubuf_ref, u_ref, j, cols, cols)
    ubuf_ref[L:2 * L, :] = u_ref[...]
    for gi, w in enumerate(POOL_WINDOWS):
        cols = slice(gi * POOL_GW, (gi + 1) * POOL_GW)
        lo = 2 * L - L - (w - 1) * TS
        s = ubuf_ref[lo:2 * L, cols]
        step = TS
        while step < w * TS:
            n = s.shape[0]
            s = s[step:n, :] + s[0:n - step, :]
            step *= 2
        pooled_ref[:, cols] = (s * (1.0 / w) - u_ref[:, cols]).astype(BF16)

    @pl.when(c == nc - 1)
    def _():
        for k in range(INNER // LANES):
            hfin_ref[k * LANES:(k + 1) * LANES, :] = ht_ref[:, k * LANES:(k + 1) * LANES].T
        ctail_ref[...] = jnp.zeros(ctail_ref.shape, F32)
        ptail_ref[...] = jnp.zeros(ptail_ref.shape, F32)
        for j in range(TPC - hist, TPC):
            r = L + j * TS + TS - 1
            ctail_ref[j - TS:j - TS + 1, :] = xbuf_ref[r:r + 1, :]
        for j in range(1, TPC):
            r = L + j * TS + TS - 1
            ptail_ref[j:j + 1, :] = ubuf_ref[r:r + 1, :]


def _param_specs(index_map):
    return [
        pl.BlockSpec((CONV_K, CONV_DIM), index_map),
        pl.BlockSpec((1, CONV_DIM), index_map),
        pl.BlockSpec((1, LANES), index_map),
        pl.BlockSpec((1, LANES), index_map),
        pl.BlockSpec((1, INNER), index_map),
        pl.BlockSpec((1, INNER), index_map),
    ]


def _seq_scratch(rows):
    return [
        pltpu.VMEM((STATE, INNER), F32),
        pltpu.VMEM((8 + rows, CONV_DIM), F32),
        pltpu.VMEM((PU_BASE + rows, D_MODEL), F32),
        pltpu.VMEM((rows, CONV_DIM), F32),
        pltpu.VMEM((PU_BASE + rows, POOL_GW), F32),
        pltpu.VMEM((PU_BASE + rows, POOL_GW), F32),
        pltpu.VMEM((PU_BASE + rows, POOL_GW), F32),
    ]


def _seq_prompt(proj, dt, h0t, ct0, pt0, e64, params, *, batch, seq):
    nc = seq // CHUNK
    row = lambda b, c: b * nc + c
    const = lambda b, c: (0, 0)
    in_specs = [
        pl.BlockSpec((CHUNK, INNER), lambda b, c: (row(b, c), CB_ZS)),
        pl.BlockSpec((CHUNK, INNER), lambda b, c: (row(b, c), CB_U)),
        pl.BlockSpec((CHUNK, INNER), lambda b, c: (row(b, c), CB_XS)),
        pl.BlockSpec((CHUNK, 2 * GROUPS * STATE), lambda b, c: (row(b, c), CB_BC)),
        pl.BlockSpec((CHUNK, LANES), lambda b, c: (row(b, c), 0)),
        pl.BlockSpec((STATE, INNER), const),
        pl.BlockSpec((8, CONV_DIM), const),
        pl.BlockSpec((POOL_MAX, D_MODEL), const),
        pl.BlockSpec((LANES, INNER), const),
    ] + _param_specs(const)
    return pl.pallas_call(
        _seq_prompt_kernel,
        grid=(batch, nc),
        in_specs=in_specs,
        out_specs=[
            pl.BlockSpec((CHUNK, INNER), lambda b, c: (row(b, c), 0)),
            pl.BlockSpec((CHUNK, D_MODEL), lambda b, c: (row(b, c), 0)),
            pl.BlockSpec((None, INNER, STATE), lambda b, c: (b, 0, 0)),
            pl.BlockSpec((None, 8, CONV_DIM), lambda b, c: (b, 0, 0)),
            pl.BlockSpec((None, POOL_MAX, D_MODEL), lambda b, c: (b, 0, 0)),
        ],
        out_shape=[
            jax.ShapeDtypeStruct((batch * seq, INNER), BF16),
            jax.ShapeDtypeStruct((batch * seq, D_MODEL), BF16),
            jax.ShapeDtypeStruct((batch, INNER, STATE), F32),
            jax.ShapeDtypeStruct((batch, 8, CONV_DIM), F32),
            jax.ShapeDtypeStruct((batch, POOL_MAX, D_MODEL), F32),
        ],
        scratch_shapes=[
            pltpu.VMEM((STATE, INNER), F32),
            pltpu.VMEM((2 * CHUNK, CONV_DIM), F32),
            pltpu.VMEM((2 * CHUNK, D_MODEL), F32),
            pltpu.VMEM((CHUNK, CONV_DIM), F32),
        ],
        compiler_params=pltpu.CompilerParams(
            dimension_semantics=("arbitrary", "arbitrary"),
            vmem_limit_bytes=VMEM_LIMIT),
        name="seq_prompt",
    )(proj, proj, proj, proj, dt, h0t, ct0, pt0, e64, *params)


def _meta_state(proj, dt, e64, params, *, row_block):
    const = lambda b, c: (0, 0)
    zeros = functools.partial(jnp.zeros, dtype=F32)
    in_specs = [
        pl.BlockSpec((N_META, INNER), lambda b, c: (row_block, CB_ZS)),
        pl.BlockSpec((N_META, INNER), lambda b, c: (row_block, CB_U)),
        pl.BlockSpec((N_META, INNER), lambda b, c: (row_block, CB_XS)),
        pl.BlockSpec((N_META, 2 * GROUPS * STATE), lambda b, c: (row_block, CB_BC)),
        pl.BlockSpec((N_META, LANES), lambda b, c: (row_block, 0)),
        pl.BlockSpec((STATE, INNER), const),
        pl.BlockSpec((8, CONV_DIM), const),
        pl.BlockSpec((POOL_MAX, D_MODEL), const),
        pl.BlockSpec((LANES, INNER), const),
    ] + _param_specs(const)
    outs = pl.pallas_call(
        functools.partial(_seq_natural_kernel, with_y=False),
        grid=(1, 1),
        in_specs=in_specs,
        out_specs=[
            pl.BlockSpec((N_META, INNER), const),
            pl.BlockSpec((N_META, D_MODEL), const),
            pl.BlockSpec((STATE, INNER), const),
            pl.BlockSpec((8, CONV_DIM), const),
            pl.BlockSpec((POOL_MAX, D_MODEL), const),
        ],
        out_shape=[
            jax.ShapeDtypeStruct((N_META, INNER), BF16),
            jax.ShapeDtypeStruct((N_META, D_MODEL), BF16),
            jax.ShapeDtypeStruct((STATE, INNER), F32),
            jax.ShapeDtypeStruct((8, CONV_DIM), F32),
            jax.ShapeDtypeStruct((POOL_MAX, D_MODEL), F32),
        ],
        scratch_shapes=_seq_scratch(N_META),
        compiler_params=pltpu.CompilerParams(
            dimension_semantics=("arbitrary", "arbitrary"),
            vmem_limit_bytes=VMEM_LIMIT),
        name="meta_state",
    )(proj, proj, proj, proj, dt, zeros((STATE, INNER)), zeros((8, CONV_DIM)),
      zeros((POOL_MAX, D_MODEL)), e64, *params)
    return outs[2], outs[3], outs[4]


def _seq_sample_kernel(zs_ref, u_ref, xs_ref, bc_ref, dt_ref, chist_ref, phist_ref, h0_ref, e64_ref,
                       cw_ref, cbias_ref, dtb_ref, alog_ref, dskip_ref, nw_ref,
                       y_ref, pooled_ref, hnew_ref, cnew_ref, pnew_ref,
                       xext_ref, uext_ref, act_ref, dv_ref, s_ref):
    R = xs_ref.shape[0]
    T = R // SEQ_BLOCK
    SB = SEQ_BLOCK
    HIST = CONV_K - 1
    PH = POOL_MAX - 1
    half = R

    for k in range(HIST):
        xext_ref[k * SB:(k + 1) * SB, :] = chist_ref[k]
    xext_ref[HIST * SB:HIST * SB + R, 0:INNER] = xs_ref[...]
    xext_ref[HIST * SB:HIST * SB + R, INNER:CONV_DIM] = bc_ref[...]
    for cs in range(CONV_DIM // GROUP_W):
        cols = slice(cs * GROUP_W, (cs + 1) * GROUP_W)
        acc = cbias_ref[:, cols] + xext_ref[0:R, cols] * cw_ref[0:1, cols]
        for k in range(1, CONV_K):
            acc = acc + xext_ref[k * SB:k * SB + R, cols] * cw_ref[k:k + 1, cols]
        act_ref[:, cols] = _silu(acc)
    for k in range(HIST):
        cnew_ref[k] = xext_ref[R + k * SB:R + (k + 1) * SB, :]

    seq_of = lambda x: jnp.bitwise_and(x, SB - 1)
    step_of = lambda x: _div_pow2(x, SB)
    r_i = _iota2((R, R), 0)
    c_i = _iota2((R, R), 1)
    same = seq_of(r_i) == seq_of(c_i)
    tri = jnp.where(same & (step_of(c_i) <= step_of(r_i)), 1.0, 0.0).astype(BF16)
    ones_bd = jnp.where(same, 1.0, 0.0).astype(BF16)
    dtv, da = _dt_terms(dt_ref, dtb_ref, alog_ref)
    da_parts = _split3(da)
    acum = _dot(tri, da_parts[0]) + _dot(tri, da_parts[1]) + _dot(tri, da_parts[2])
    a_end = (_dot(ones_bd, da_parts[0]) + _dot(ones_bd, da_parts[1])
             + _dot(ones_bd, da_parts[2]))

    acum_t2 = jnp.concatenate([acum, acum], axis=0).T
    aend_t = _pad_rows(a_end, LANES).T

    r_s = _iota2((LANES, SB * STATE), 0)
    c_s = _iota2((LANES, SB * STATE), 1)
    sel = jnp.where(r_s == _div_pow2(c_s, STATE), 1.0, 0.0).astype(BF16)
    dv_ref[...] = jnp.exp(_dot_exact(_split3(aend_t), sel))

    lane2 = _iota2((R, 2 * R), 1)
    r_2 = _iota2((R, 2 * R), 0)
    c_2 = jnp.bitwise_and(lane2, R - 1)
    mask2 = (seq_of(r_2) == seq_of(c_2)) & (step_of(c_2) <= step_of(r_2))
    first_half = lane2 < half
    lane = _iota2((R, LANES), 1)
    r_b = _iota2((R, SB * STATE), 0)
    c_b = _iota2((R, SB * STATE), 1)
    bd_sel = seq_of(r_b) == _div_pow2(c_b, STATE)
    stacked = _split3(jnp.concatenate([acum, dtv, a_end], axis=0))

    for g in range(GROUPS):
        cols = slice(g * GROUP_W, (g + 1) * GROUP_W)
        ex = _dot_exact(stacked, e64_ref[:, cols])
        acx = ex[0:R, :]
        dtx = ex[R:2 * R, :]
        aendx = ex[2 * R:3 * R, :]
        xs_g = act_ref[:, cols]
        b_g = act_ref[:, INNER + g * STATE:INNER + (g + 1) * STATE]
        c_f = act_ref[:, INNER + GROUPS * STATE + g * STATE:INNER + GROUPS * STATE + (g + 1) * STATE]
        c_g = c_f.astype(BF16)
        xdt = xs_g * dtx
        xdtw = xdt * jnp.exp(aendx - acx)

        c_big = jnp.where(bd_sel, jnp.concatenate([c_f] * SB, axis=1), 0.0).astype(BF16)
        h_cat = jnp.concatenate(
            [h0_ref[i, g * GROUP_W:(g + 1) * GROUP_W, :].astype(BF16) for i in range(SB)], axis=1)
        yoff = _dot_nt(c_big, h_cat)

        xt = _pad_rows(xdtw, LANES).T
        b_big = jnp.where(bd_sel, jnp.concatenate([b_g] * SB, axis=1), 0.0)
        b_big = _pad_rows(b_big, LANES)
        s_ref[...] = _dot(xt.astype(BF16), b_big.astype(BF16))
        for i in range(SB):
            for hh in range(GROUP_W // HEADDIM):
                h = g * (GROUP_W // HEADDIM) + hh
                rows = slice(h * HEADDIM, (h + 1) * HEADDIM)
                hnew_ref[i, rows, :] = (
                    h0_ref[i, rows, :] * dv_ref[h:h + 1, i * STATE:(i + 1) * STATE]
                    + s_ref[hh * HEADDIM:(hh + 1) * HEADDIM, i * STATE:(i + 1) * STATE])

        cb2 = _dot_nt(c_g, jnp.concatenate([b_g, b_g], axis=0).astype(BF16))
        yds = []
        for q in range(GROUP_W // LANES):
            ha = g * 8 + 2 * q
            colc = jnp.where(first_half, acum[:, ha:ha + 1], acum[:, ha + 1:ha + 2])
            rowc = jnp.where(first_half[0:1, :], acum_t2[ha:ha + 1, :], acum_t2[ha + 1:ha + 2, :])
            dec = jnp.where(mask2, jnp.exp(colc - rowc), 0.0)
            m_p = (cb2 * dec).astype(BF16)
            xp = xdt[:, q * LANES:(q + 1) * LANES]
            r_m = jnp.concatenate([jnp.where(lane < HEADDIM, xp, 0.0),
                                   jnp.where(lane >= HEADDIM, xp, 0.0)], axis=0).astype(BF16)
            yds.append(_dot(m_p, r_m))
        y_g = jnp.concatenate(yds, axis=1) + yoff * jnp.exp(acx) + xs_g * dskip_ref[:, cols]
        _gate_norm_store(y_g, zs_ref[:, cols], nw_ref[:, cols], y_ref, cols)

    for k in range(PH):
        uext_ref[k * SB:(k + 1) * SB, :] = phist_ref[k]
    uext_ref[PH * SB:PH * SB + R, :] = u_ref[...]
    for k in range(PH):
        pnew_ref[k] = uext_ref[(T + k) * SB:(T + k + 1) * SB, :]
    for gi, w in enumerate(POOL_WINDOWS):
        cols = slice(gi * POOL_GW, (gi + 1) * POOL_GW)
        u_cur = u_ref[:, cols]
        s = u_cur
        for k in range(1, w):
            s = s + uext_ref[(PH - k) * SB:(PH - k) * SB + R, cols]
        pooled_ref[:, cols] = (s * (1.0 / w) - u_cur).astype(BF16)


def _seq_sample(proj, dt, conv_hist, pool_hist, h0, e64, params, *, nseq, t_new):
    rb = SEQ_BLOCK * t_new
    assert 2 * rb == LANES
    const = lambda s: (0, 0)
    in_specs = [
        pl.BlockSpec((rb, INNER), lambda s: (s, CB_ZS)),
        pl.BlockSpec((rb, INNER), lambda s: (s, CB_U)),
        pl.BlockSpec((rb, INNER), lambda s: (s, CB_XS)),
        pl.BlockSpec((rb, 2 * GROUPS * STATE), lambda s: (s, CB_BC)),
        pl.BlockSpec((rb, LANES), lambda s: (s, 0)),
        pl.BlockSpec((CONV_K - 1, SEQ_BLOCK, CONV_DIM), lambda s: (0, s, 0)),
        pl.BlockSpec((POOL_MAX - 1, SEQ_BLOCK, D_MODEL), lambda s: (0, s, 0)),
        pl.BlockSpec((SEQ_BLOCK, INNER, STATE), lambda s: (s, 0, 0)),
        pl.BlockSpec((LANES, INNER), const),
    ] + _param_specs(const)
    return pl.pallas_call(
        _seq_sample_kernel,
        grid=(nseq // SEQ_BLOCK,),
        in_specs=in_specs,
        out_specs=[
            pl.BlockSpec((rb, INNER), lambda s: (s, 0)),
            pl.BlockSpec((rb, D_MODEL), lambda s: (s, 0)),
            pl.BlockSpec((SEQ_BLOCK, INNER, STATE), lambda s: (s, 0, 0)),
            pl.BlockSpec((CONV_K - 1, SEQ_BLOCK, CONV_DIM), lambda s: (0, s, 0)),
            pl.BlockSpec((POOL_MAX - 1, SEQ_BLOCK, D_MODEL), lambda s: (0, s, 0)),
        ],
        out_shape=[
            jax.ShapeDtypeStruct((nseq * t_new, INNER), BF16),
            jax.ShapeDtypeStruct((nseq * t_new, D_MODEL), BF16),
            jax.ShapeDtypeStruct((nseq, INNER, STATE), F32),
            jax.ShapeDtypeStruct((CONV_K - 1, nseq, CONV_DIM), F32),
            jax.ShapeDtypeStruct((POOL_MAX - 1, nseq, D_MODEL), F32),
        ],
        scratch_shapes=[
            pltpu.VMEM((SEQ_BLOCK * (CONV_K - 1 + t_new), CONV_DIM), F32),
            pltpu.VMEM((SEQ_BLOCK * (POOL_MAX - 1 + t_new), D_MODEL), F32),
            pltpu.VMEM((rb, CONV_DIM), F32),
            pltpu.VMEM((LANES, SEQ_BLOCK * STATE), F32),
            pltpu.VMEM((GROUP_W, SEQ_BLOCK * STATE), F32),
        ],
        compiler_params=pltpu.CompilerParams(
            dimension_semantics=("arbitrary",),
            vmem_limit_bytes=VMEM_LIMIT),
        name="seq_sample",
    )(proj, proj, proj, proj, dt, conv_hist, pool_hist, h0, e64, *params)


def _merge_kernel(y_ref, pooled_ref, zp_ref, g1_ref, g2_ref, wmix_ref, bmix_ref, pscale_ref,
                  wps_ref, wpp_ref, o_ref, pout_ref):
    for g in range(len(POOL_WINDOWS)):
        cols = slice(g * POOL_GW, (g + 1) * POOL_GW)
        mixed = _dot(pooled_ref[:, cols], wmix_ref[g]) + bmix_ref[:, cols]
        pout_ref[:, cols] = (mixed * pscale_ref[:, cols] * _silu(zp_ref[:, cols])).astype(BF16)
    bs = _dot(y_ref[...], wps_ref[...])
    bp = _dot(pout_ref[...], wpp_ref[...])
    o_ref[...] = (_sigmoid(g1_ref[...]) * bs + _sigmoid(g2_ref[...]) * bp).astype(BF16)


def _merge(y, pooled, proj, wmix, bmix, pscale, wps, wpp, *, m, tm=256):
    const2 = lambda i: (0, 0)
    resident = dict(pipeline_mode=pl.Buffered(1))
    return pl.pallas_call(
        _merge_kernel,
        grid=(m // tm,),
        in_specs=[
            pl.BlockSpec((tm, INNER), lambda i: (i, 0)),
            pl.BlockSpec((tm, D_MODEL), lambda i: (i, 0)),
            pl.BlockSpec((tm, D_MODEL), lambda i: (i, CB_ZP)),
            pl.BlockSpec((tm, D_MODEL), lambda i: (i, CB_G1)),
            pl.BlockSpec((tm, D_MODEL), lambda i: (i, CB_G2)),
            pl.BlockSpec((len(POOL_WINDOWS), POOL_GW, POOL_GW), lambda i: (0, 0, 0), **resident),
            pl.BlockSpec((1, D_MODEL), const2),
            pl.BlockSpec((1, D_MODEL), const2),
            pl.BlockSpec((INNER, D_MODEL), const2, **resident),
            pl.BlockSpec((D_MODEL, D_MODEL), const2, **resident),
        ],
        out_specs=pl.BlockSpec((tm, D_MODEL), lambda i: (i, 0)),
        out_shape=jax.ShapeDtypeStruct((m, D_MODEL), BF16),
        scratch_shapes=[pltpu.VMEM((tm, D_MODEL), BF16)],
        compiler_params=pltpu.CompilerParams(
            dimension_semantics=("arbitrary",),
            vmem_limit_bytes=VMEM_LIMIT),
        name="merge",
    )(y, pooled, proj, proj, proj, wmix, bmix, pscale, wps, wpp)


def _out_kernel(m_ref, h_ref, wout_ref, fw_ref, o_ref, *, tile_major):
    if tile_major:
        unperm = _tile_major_perm(transpose=True)
        merged = jnp.concatenate(
            [_dot(unperm, m_ref[c * CHUNK:(c + 1) * CHUNK, :]).astype(BF16)
             for c in range(m_ref.shape[0] // CHUNK)], axis=0)
    else:
        merged = m_ref[...]
    hn = h_ref[...] + _dot(merged, wout_ref[...])
    ms = jnp.mean(hn * hn, axis=-1, keepdims=True)
    o_ref[...] = (hn * lax.rsqrt(ms + EPS)) * fw_ref[...]


def _out(merged, h, wout, fw, *, m, tm=512, tile_major=False):
    return pl.pallas_call(
        functools.partial(_out_kernel, tile_major=tile_major),
        grid=(m // tm,),
        in_specs=[
            pl.BlockSpec((tm, D_MODEL), lambda i: (i, 0)),
            pl.BlockSpec((tm, D_MODEL), lambda i: (i, 0)),
            pl.BlockSpec((D_MODEL, D_MODEL), lambda i: (0, 0)),
            pl.BlockSpec((1, D_MODEL), lambda i: (0, 0)),
        ],
        out_specs=pl.BlockSpec((tm, D_MODEL), lambda i: (i, 0)),
        out_shape=jax.ShapeDtypeStruct((m, D_MODEL), F32),
        compiler_params=pltpu.CompilerParams(
            dimension_semantics=("arbitrary",),
            vmem_limit_bytes=VMEM_LIMIT),
        name="out",
    )(merged, h, wout, fw)


def kernel(x_prompt, x_sample, state_conv, state_ssm, state_pool, meta_tokens, norm_w, w_in,
           conv_w, conv_b, dt_bias, a_log, d_skip, ssd_norm_w, w_proj_ssd, pool_mix_w,
           pool_mix_b, pool_scale, w_proj_pool, w_out, final_norm_w):
    batch, seq, _ = x_prompt.shape
    nseq, t_new, _ = x_sample.shape
    assert norm_w.shape[0] == 1, "single layer"
    assert w_in.shape == (1, D_MODEL, IN_ZP + 4 * D_MODEL)

    wps = w_proj_ssd[0].astype(BF16)
    wpp = w_proj_pool[0].astype(BF16)
    wout = w_out[0].astype(BF16)
    wmix = pool_mix_w[0].astype(BF16)
    lane_pad = lambda v: jnp.pad(v.reshape(1, HEADS), ((0, 0), (0, LANES - HEADS)))
    params = (conv_w[0], conv_b[0].reshape(1, CONV_DIM), lane_pad(dt_bias[0]), lane_pad(a_log[0]),
              jnp.repeat(d_skip[0], HEADDIM).reshape(1, INNER), ssd_norm_w[0].reshape(1, INNER))
    e64 = (jnp.arange(LANES)[:, None] == (jnp.arange(INNER)[None, :] // HEADDIM)).astype(BF16)
    nw = norm_w[0].reshape(1, D_MODEL)
    fw = final_norm_w.reshape(1, D_MODEL)
    bmix = pool_mix_b[0].reshape(1, D_MODEL)
    pscale = pool_scale[0].reshape(1, D_MODEL)

    xp = x_prompt.reshape(batch * seq, D_MODEL)
    m_s = nseq * t_new
    nblk = nseq // SEQ_BLOCK
    xs_tm = x_sample.reshape(nblk, SEQ_BLOCK, t_new, D_MODEL).transpose(0, 2, 1, 3)
    x_sm = jnp.concatenate([xs_tm.reshape(m_s, D_MODEL), meta_tokens], axis=0)

    proj_s, dt_s, w_main, w_dt = _prep_proj(x_sm, nw, jnp.swapaxes(w_in, 1, 2)[0])
    proj_p, dt_p = _proj(xp, nw, w_main, w_dt, tm=1024, tile_major=True)

    h0t, ct0, pt0 = _meta_state(proj_s, dt_s, e64, params, row_block=m_s // N_META)

    y_p, pooled_p, ssm_p, conv_p, pool_p = _seq_prompt(
        proj_p, dt_p, h0t, ct0, pt0, e64, params, batch=batch, seq=seq)
    y_s, pooled_s, ssm_s, conv_s, pool_s = _seq_sample(
        proj_s, dt_s, jnp.swapaxes(state_conv[0], 0, 1), jnp.swapaxes(state_pool[0], 0, 1),
        state_ssm[0].reshape(nseq, INNER, STATE), e64, params, nseq=nseq, t_new=t_new)

    merged_p = _merge(y_p, pooled_p, proj_p, wmix, bmix, pscale, wps, wpp, m=batch * seq)
    merged_s = _merge(y_s, pooled_s, proj_s, wmix, bmix, pscale, wps, wpp, m=m_s)
    out_p = _out(merged_p, xp, wout, fw, m=batch * seq, tile_major=True)
    out_s = _out(merged_s, x_sm, wout, fw, m=m_s)

    return (out_p.reshape(batch, seq, D_MODEL),
            out_s.reshape(nblk, t_new, SEQ_BLOCK, D_MODEL).transpose(0, 2, 1, 3).reshape(
                nseq, t_new, D_MODEL),
            conv_p[:, 8 - (CONV_K - 1):][None].astype(state_conv.dtype),
            ssm_p.reshape(1, batch, HEADS, HEADDIM, STATE).astype(state_ssm.dtype),
            pool_p[:, 1:][None].astype(state_pool.dtype),
            jnp.swapaxes(conv_s, 0, 1)[None].astype(state_conv.dtype),
            ssm_s.reshape(1, nseq, HEADS, HEADDIM, STATE).astype(state_ssm.dtype),
            jnp.swapaxes(pool_s, 0, 1)[None].astype(state_pool.dtype))
```

```python
import functools

import jax
import jax.numpy as jnp
from jax import lax
from jax.experimental import pallas as pl
from jax.experimental.pallas import tpu as pltpu

F32 = jnp.float32
BF16 = jnp.bfloat16

D_MODEL = 2048
N_META = 16
EPS = 1e-6
HEADS = 32
HEADDIM = 64
GROUPS = 4
STATE = 128
CONV_K = 4
INNER = HEADS * HEADDIM
GROUP_W = INNER // GROUPS
CONV_DIM = INNER + 2 * GROUPS * STATE
POOL_WINDOWS = (2, 4, 8, 16)
POOL_MAX = 16
POOL_GW = D_MODEL // len(POOL_WINDOWS)

IN_XBC = INNER
IN_DT = IN_XBC + CONV_DIM
IN_ZP = IN_DT + HEADS
N_MAIN = 5 * D_MODEL + CONV_DIM
CB_ZS, CB_ZP, CB_U, CB_G1, CB_G2, CB_XS = 0, 1, 2, 3, 4, 5
CB_BC = 12

LANES = 128
CHUNK = 128
SEQ_BLOCK = 8
VMEM_LIMIT = 60 * 1024 * 1024
PREP_TN = 1024
PU_BASE = 2 * POOL_MAX
TPC = 16
TS = CHUNK // TPC


def _silu(x):
    hx = 0.5 * x
    return hx + hx * jnp.tanh(hx)


def _sigmoid(x):
    return 0.5 + 0.5 * jnp.tanh(0.5 * x)


def _dot(a, b):
    return jnp.dot(a, b, preferred_element_type=F32)


def _dot_nt(a, b):
    return lax.dot_general(a, b, (((1,), (1,)), ((), ())), preferred_element_type=F32)


def _split3(a):
    a1 = a.astype(BF16)
    r1 = a - a1.astype(F32)
    a2 = r1.astype(BF16)
    a3 = (r1 - a2.astype(F32)).astype(BF16)
    return a1, a2, a3


def _dot_exact(parts, b):
    out = _dot(parts[0], b)
    for p in parts[1:]:
        out = out + _dot(p, b)
    return out


def _dot_exact_l(a01, x):
    parts = _split3(x)
    out = _dot(a01, parts[0])
    for p in parts[1:]:
        out = out + _dot(a01, p)
    return out


def _iota2(shape, dim):
    return lax.broadcasted_iota(jnp.int32, shape, dim)


def _div_pow2(x, d):
    assert d & (d - 1) == 0
    return jnp.right_shift(x, d.bit_length() - 1)


def _pad_rows(x, rows):
    if x.shape[0] == rows:
        return x
    return jnp.concatenate([x, jnp.zeros((rows - x.shape[0], x.shape[1]), x.dtype)], axis=0)


def _prep_proj_kernel(x_ref, nw_ref, a_ref, d_ref, o_ref, dt_ref, wm_ref, wdt_ref, xn_ref, *, sub_x, sub_w):
    @pl.when(pl.program_id(0) == 0)
    def _():
        def body(r, carry):
            rows = pl.ds(pl.multiple_of(r * sub_x, sub_x), sub_x)
            x = x_ref[rows, :]
            ms = jnp.mean(x * x, axis=-1, keepdims=True)
            xn_ref[rows, :] = ((x * lax.rsqrt(ms + EPS)) * nw_ref[...]).astype(BF16)
            return carry

        lax.fori_loop(0, x_ref.shape[0] // sub_x, body, 0)
        row = _iota2((LANES, D_MODEL), 0)
        wdt_ref[...] = jnp.where(row < HEADS, d_ref[...], 0.0).astype(BF16)
        dt_ref[...] = _dot_nt(xn_ref[...], wdt_ref[...])

    def cast(r, carry):
        rows = pl.ds(pl.multiple_of(r * sub_w, sub_w), sub_w)
        wm_ref[rows, :] = a_ref[rows, :].astype(BF16)
        return carry

    lax.fori_loop(0, PREP_TN // sub_w, cast, 0)
    o_ref[...] = _dot_nt(xn_ref[...], wm_ref[...])


def _prep_proj(x, norm_w, w_t):
    m = x.shape[0]

    def a_map(c):
        src = jnp.where(c < 2, c * PREP_TN,
                        jnp.where(c < 10, IN_ZP + (c - 2) * PREP_TN, IN_XBC + (c - 10) * PREP_TN))
        return (pl.multiple_of(src, 32), 0)

    return pl.pallas_call(
        functools.partial(_prep_proj_kernel, sub_x=16 if m % 128 else 128, sub_w=256),
        grid=(N_MAIN // PREP_TN,),
        in_specs=[
            pl.BlockSpec((m, D_MODEL), lambda c: (0, 0), pipeline_mode=pl.Buffered(1)),
            pl.BlockSpec((1, D_MODEL), lambda c: (0, 0)),
            pl.BlockSpec((pl.Element(PREP_TN), pl.Element(D_MODEL)), a_map),
            pl.BlockSpec((pl.Element(LANES), pl.Element(D_MODEL)), lambda c: (IN_DT, 0)),
        ],
        out_specs=[
            pl.BlockSpec((m, PREP_TN), lambda c: (0, c)),
            pl.BlockSpec((m, LANES), lambda c: (0, 0)),
            pl.BlockSpec((PREP_TN, D_MODEL), lambda c: (c, 0)),
            pl.BlockSpec((LANES, D_MODEL), lambda c: (0, 0)),
        ],
        out_shape=[
            jax.ShapeDtypeStruct((m, N_MAIN), F32),
            jax.ShapeDtypeStruct((m, LANES), F32),
            jax.ShapeDtypeStruct((N_MAIN, D_MODEL), BF16),
            jax.ShapeDtypeStruct((LANES, D_MODEL), BF16),
        ],
        scratch_shapes=[pltpu.VMEM((m, D_MODEL), BF16)],
        compiler_params=pltpu.CompilerParams(
            dimension_semantics=("arbitrary",), vmem_limit_bytes=VMEM_LIMIT),
        name="prep_proj",
    )(x, norm_w, w_t, w_t)


def _time_of(row):
    return jnp.bitwise_and(row, TS - 1) * TPC + jnp.right_shift(row, TS.bit_length() - 1)


def _tile_major_perm(transpose=False):
    r = _iota2((CHUNK, CHUNK), 0)
    c = _iota2((CHUNK, CHUNK), 1)
    hit = (r == _time_of(c)) if transpose else (c == _time_of(r))
    return jnp.where(hit, 1.0, 0.0).astype(BF16)


def _norm_kernel(x_ref, nw_ref, wdt_ref, xn_ref, dt_ref):
    x = x_ref[...]
    ms = jnp.mean(x * x, axis=-1, keepdims=True)
    xn = ((x * lax.rsqrt(ms + EPS)) * nw_ref[...]).astype(BF16)
    perm = _tile_major_perm()
    for c in range(x_ref.shape[0] // CHUNK):
        rows = slice(c * CHUNK, (c + 1) * CHUNK)
        xn_ref[rows, :] = _dot(perm, xn[rows, :]).astype(BF16)
    dt_ref[...] = _dot_nt(xn_ref[...], wdt_ref[...])


def _norm(x, norm_w, w_dt, *, tm=256):
    m = x.shape[0]
    return pl.pallas_call(
        _norm_kernel,
        grid=(m // tm,),
        in_specs=[
            pl.BlockSpec((tm, D_MODEL), lambda i: (i, 0)),
            pl.BlockSpec((1, D_MODEL), lambda i: (0, 0)),
            pl.BlockSpec((LANES, D_MODEL), lambda i: (0, 0)),
        ],
        out_specs=[
            pl.BlockSpec((tm, D_MODEL), lambda i: (i, 0)),
            pl.BlockSpec((tm, LANES), lambda i: (i, 0)),
        ],
        out_shape=[
            jax.ShapeDtypeStruct((m, D_MODEL), BF16),
            jax.ShapeDtypeStruct((m, LANES), F32),
        ],
        compiler_params=pltpu.CompilerParams(
            dimension_semantics=("arbitrary",), vmem_limit_bytes=VMEM_LIMIT),
        name="norm",
    )(x, norm_w, w_dt)


def _proj_kernel(xn_ref, w_ref, o_ref):
    o_ref[...] = _dot_nt(xn_ref[...], w_ref[...])


def _proj(xn, w_main, *, tm=2048, tn=1024):
    m = xn.shape[0]
    assert m % tm == 0 and N_MAIN % tn == 0
    return pl.pallas_call(
        _proj_kernel,
        grid=(m // tm, N_MAIN // tn),
        in_specs=[
            pl.BlockSpec((tm, D_MODEL), lambda i, j: (i, 0)),
            pl.BlockSpec((tn, D_MODEL), lambda i, j: (j, 0)),
        ],
        out_specs=pl.BlockSpec((tm, tn), lambda i, j: (i, j)),
        out_shape=jax.ShapeDtypeStruct((m, N_MAIN), F32),
        compiler_params=pltpu.CompilerParams(
            dimension_semantics=("arbitrary", "arbitrary"),
            vmem_limit_bytes=VMEM_LIMIT),
        name="proj",
    )(xn, w_main)


def _conv_silu(ext_ref, first, rows, cw_ref, cbias_ref, act_ref, row0):
    for cs in range(CONV_DIM // GROUP_W):
        cols = slice(cs * GROUP_W, (cs + 1) * GROUP_W)
        acc = cbias_ref[:, cols] + ext_ref[first:first + rows, cols] * cw_ref[0:1, cols]
        for k in range(1, CONV_K):
            acc = acc + ext_ref[first + k:first + k + rows, cols] * cw_ref[k:k + 1, cols]
        act_ref[row0:row0 + rows, cols] = _silu(acc)


def _dt_terms(dt_ref, dtb_ref, alog_ref):
    x = dt_ref[...] + dtb_ref[...]
    dtv = jnp.maximum(x, 0.0) + jnp.log1p(jnp.exp(-jnp.abs(x)))
    da = dtv * (-jnp.exp(alog_ref[...]))
    return dtv, da


def _gate_norm_store(y_g, zs, nw, y_ref, cols):
    y_g = y_g * _silu(zs)
    ms = jnp.mean(y_g * y_g, axis=-1, keepdims=True)
    y_ref[:, cols] = (y_g * lax.rsqrt(ms + EPS) * nw).astype(BF16)


def _pool_prompt(uext_ref, l1_ref, l2_ref, l3_ref, pooled_ref, rows):
    base, end = PU_BASE, PU_BASE + rows
    for gi, w in enumerate(POOL_WINDOWS):
        cols = slice(gi * POOL_GW, (gi + 1) * POOL_GW)
        levels = w.bit_length() - 1
        lo = base - 8 * (levels - 1)
        if levels == 1:
            s = uext_ref[base:end, cols] + uext_ref[base - 1:end - 1, cols]
        else:
            l1_ref[lo:end, :] = uext_ref[lo:end, cols] + uext_ref[lo - 1:end - 1, cols]
            if levels == 2:
                s = l1_ref[base:end, :] + l1_ref[base - 2:end - 2, :]
            else:
                lo2 = lo + 8
                l2_ref[lo2:end, :] = l1_ref[lo2:end, :] + l1_ref[lo2 - 2:end - 2, :]
                if levels == 3:
                    s = l2_ref[base:end, :] + l2_ref[base - 4:end - 4, :]
                else:
                    lo3 = lo2 + 8
                    l3_ref[lo3:end, :] = l2_ref[lo3:end, :] + l2_ref[lo3 - 4:end - 4, :]
                    s = l3_ref[base:end, :] + l3_ref[base - 8:end - 8, :]
        pooled_ref[:, cols] = (s * (1.0 / w) - uext_ref[base:end, cols]).astype(BF16)


def _seq_natural_kernel(zs_ref, u_ref, xs_ref, bc_ref, dt_ref, h0_ref, ct0_ref, pt0_ref, e64_ref,
                        cw_ref, cbias_ref, dtb_ref, alog_ref, dskip_ref, nw_ref,
                        y_ref, pooled_ref, hfin_ref, ctail_ref, ptail_ref,
                        ht_ref, xext_ref, uext_ref, act_ref, l1_ref, l2_ref, l3_ref, *, with_y):
    L = xs_ref.shape[0]
    c = pl.program_id(1)
    nc = pl.num_programs(1)

    @pl.when(c == 0)
    def _():
        ht_ref[...] = h0_ref[...]
        xext_ref[0:8, :] = ct0_ref[...]
        uext_ref[0:POOL_MAX, :] = jnp.zeros((POOL_MAX, D_MODEL), F32)
        uext_ref[POOL_MAX:PU_BASE, :] = pt0_ref[...]

    xext_ref[8:8 + L, 0:INNER] = xs_ref[...]
    xext_ref[8:8 + L, INNER:CONV_DIM] = bc_ref[...]
    _conv_silu(xext_ref, 8 - (CONV_K - 1), L, cw_ref, cbias_ref, act_ref, 0)
    xext_ref[0:8, :] = xext_ref[L:L + 8, :]

    r_i = _iota2((L, L), 0)
    c_i = _iota2((L, L), 1)
    causal = c_i <= r_i
    tri = jnp.where(causal, 1.0, 0.0).astype(BF16)
    dtv, da = _dt_terms(dt_ref, dtb_ref, alog_ref)
    acum = _dot_exact_l(tri, da)
    stacked = _split3(jnp.concatenate([acum, dtv], axis=0))
    if with_y:
        acum_t = acum.T
        lane = _iota2((L, LANES), 1)

    for g in range(GROUPS):
        cols = slice(g * GROUP_W, (g + 1) * GROUP_W)
        ex = _dot_exact(stacked, e64_ref[:, cols])
        acx = ex[0:L, :]
        dtx = ex[L:2 * L, :]
        xs_g = act_ref[:, cols]
        b_g = act_ref[:, INNER + g * STATE:INNER + (g + 1) * STATE]
        xdt = xs_g * dtx
        aend = acx[L - 1:L, :]
        xdtw = _pad_rows(xdt * jnp.exp(aend - acx), LANES).astype(BF16)
        b_t = _pad_rows(b_g, LANES).T
        h_old = ht_ref[:, cols]
        ht_ref[:, cols] = h_old * jnp.exp(aend) + _dot(b_t.astype(BF16), xdtw)

        if with_y:
            c_g = act_ref[:, INNER + GROUPS * STATE + g * STATE:
                          INNER + GROUPS * STATE + (g + 1) * STATE].astype(BF16)
            cb = _dot_nt(c_g, b_g.astype(BF16))
            yoff = _dot(c_g, h_old.astype(BF16)) * jnp.exp(acx)
            yds = []
            for q in range(GROUP_W // LANES):
                ms_ = []
                for h in (g * 8 + 2 * q, g * 8 + 2 * q + 1):
                    seg = acum[:, h:h + 1] - acum_t[h:h + 1, :]
                    dec = jnp.where(causal, jnp.exp(seg), 0.0)
                    ms_.append((cb * dec).astype(BF16))
                m_cat = jnp.concatenate(ms_, axis=1)
                xp = xdt[:, q * LANES:(q + 1) * LANES]
                r_m = jnp.concatenate([jnp.where(lane < HEADDIM, xp, 0.0),
                                       jnp.where(lane >= HEADDIM, xp, 0.0)], axis=0).astype(BF16)
                yds.append(_dot(m_cat, r_m))
            y_g = jnp.concatenate(yds, axis=1) + yoff + xs_g * dskip_ref[:, cols]
            _gate_norm_store(y_g, zs_ref[:, cols], nw_ref[:, cols], y_ref, cols)

    if with_y:
        uext_ref[PU_BASE:PU_BASE + L, :] = u_ref[...]
        _pool_prompt(uext_ref, l1_ref, l2_ref, l3_ref, pooled_ref, L)
        uext_ref[POOL_MAX:PU_BASE, :] = uext_ref[L + POOL_MAX:L + PU_BASE, :]

        @pl.when(c == nc - 1)
        def _():
            for k in range(INNER // LANES):
                hfin_ref[k * LANES:(k + 1) * LANES, :] = ht_ref[:, k * LANES:(k + 1) * LANES].T
            ctail_ref[...] = xext_ref[0:8, :]
            ptail_ref[...] = uext_ref[POOL_MAX:PU_BASE, :]
    else:
        hfin_ref[...] = ht_ref[...]
        ctail_ref[...] = xext_ref[0:8, :]
        ptail_ref[...] = u_ref[...]
        y_ref[...] = jnp.zeros(y_ref.shape, y_ref.dtype)
        pooled_ref[...] = jnp.zeros(pooled_ref.shape, pooled_ref.dtype)


def _shifted_tile(buf_ref, new_ref, j, cols_buf, cols_new):
    rows = slice(j * TS, (j + 1) * TS)
    old = buf_ref[CHUNK + j * TS:CHUNK + (j + 1) * TS, cols_buf]
    new = new_ref[rows, cols_new]
    is_last = _iota2(new.shape, 0) == TS - 1
    buf_ref[rows, cols_buf] = pltpu.roll(jnp.where(is_last, old, new), 1, 0)


def _seq_prompt_kernel(zs_ref, u_ref, xs_ref, bc_ref, dt_ref, h0_ref, ct0_ref, pt0_ref, e64_ref,
                       cw_ref, cbias_ref, dtb_ref, alog_ref, dskip_ref, nw_ref,
                       y_ref, pooled_ref, hfin_ref, ctail_ref, ptail_ref,
                       ht_ref, xbuf_ref, ubuf_ref, act_ref):
    L = CHUNK
    c = pl.program_id(1)
    nc = pl.num_programs(1)
    hist = CONV_K - 1
    all_cols = slice(None)

    @pl.when(c == 0)
    def _():
        ht_ref[...] = h0_ref[...]
        xbuf_ref[L:2 * L, :] = jnp.zeros((L, CONV_DIM), F32)
        ubuf_ref[L:2 * L, :] = jnp.zeros((L, D_MODEL), F32)
        for j in range(TPC - hist, TPC):
            r = L + j * TS + TS - 1
            xbuf_ref[r:r + 1, :] = ct0_ref[j - TS:j - TS + 1, :]
        for j in range(1, TPC):
            r = L + j * TS + TS - 1
            ubuf_ref[r:r + 1, :] = pt0_ref[j:j + 1, :]

    for j in range(TPC - hist, TPC):
        _shifted_tile(xbuf_ref, xs_ref, j, slice(0, INNER), all_cols)
        _shifted_tile(xbuf_ref, bc_ref, j, slice(INNER, CONV_DIM), all_cols)
    xbuf_ref[L:2 * L, 0:INNER] = xs_ref[...]
    xbuf_ref[L:2 * L, INNER:CONV_DIM] = bc_ref[...]
    for cs in range(CONV_DIM // GROUP_W):
        cols = slice(cs * GROUP_W, (cs + 1) * GROUP_W)
        acc = cbias_ref[:, cols] + xbuf_ref[L - hist * TS:2 * L - hist * TS, cols] * cw_ref[0:1, cols]
        for k in range(1, CONV_K):
            back = (hist - k) * TS
            acc = acc + xbuf_ref[L - back:2 * L - back, cols] * cw_ref[k:k + 1, cols]
        act_ref[:, cols] = _silu(acc)

    t_r = _time_of(_iota2((L, L), 0))
    t_c = _time_of(_iota2((L, L), 1))
    causal = t_c <= t_r
    tri = jnp.where(causal, 1.0, 0.0).astype(BF16)
    dtv, da = _dt_terms(dt_ref, dtb_ref, alog_ref)
    acum = _dot_exact_l(tri, da)
    stacked = _split3(jnp.concatenate([acum, dtv], axis=0))
    acum_t = acum.T
    lane = _iota2((L, LANES), 1)

    for g in range(GROUPS):
        cols = slice(g * GROUP_W, (g + 1) * GROUP_W)
        ex = _dot_exact(stacked, e64_ref[:, cols])
        acx = ex[0:L, :]
        dtx = ex[L:2 * L, :]
        xs_g = act_ref[:, cols]
        b_g = act_ref[:, INNER + g * STATE:INNER + (g + 1) * STATE]
        xdt = xs_g * dtx
        aend = acx[L - 1:L, :]
        xdtw = (xdt * jnp.exp(aend - acx)).astype(BF16)
        h_old = ht_ref[:, cols]
        ht_ref[:, cols] = h_old * jnp.exp(aend) + _dot(b_g.T.astype(BF16), xdtw)

        c_g = act_ref[:, INNER + GROUPS * STATE + g * STATE:
                      INNER + GROUPS * STATE + (g + 1) * STATE].astype(BF16)
        cb = _dot_nt(c_g, b_g.astype(BF16))
        yoff = _dot(c_g, h_old.astype(BF16)) * jnp.exp(acx)
        yds = []
        for q in range(GROUP_W // LANES):
            ms_ = []
            for h in (g * 8 + 2 * q, g * 8 + 2 * q + 1):
                seg = acum[:, h:h + 1] - acum_t[h:h + 1, :]
                dec = jnp.where(causal, jnp.exp(seg), 0.0)
                ms_.append((cb * dec).astype(BF16))
            m_cat = jnp.concatenate(ms_, axis=1)
            xp = xdt[:, q * LANES:(q + 1) * LANES]
            r_m = jnp.concatenate([jnp.where(lane < HEADDIM, xp, 0.0),
                                   jnp.where(lane >= HEADDIM, xp, 0.0)], axis=0).astype(BF16)
            yds.append(_dot(m_cat, r_m))
        y_g = jnp.concatenate(yds, axis=1) + yoff + xs_g * dskip_ref[:, cols]
        _gate_norm_store(y_g, zs_ref[:, cols], nw_ref[:, cols], y_ref, cols)

    for gi, w in enumerate(POOL_WINDOWS):
        cols = slice(gi * POOL_GW, (gi + 1) * POOL_GW)
        for j in range(TPC - (w - 1), TPC):
            _shifted_tile(ubuf_ref, u_ref, j, cols, cols)
    ubuf_ref[L:2 * L, :] = u_ref[...]
    for gi, w in enumerate(POOL_WINDOWS):
        cols = slice(gi * POOL_GW, (gi + 1) * POOL_GW)
        lo = 2 * L - L - (w - 1) * TS
        s = ubuf_ref[lo:2 * L, cols]
        step = TS
        while step < w * TS:
            n = s.shape[0]
            s = s[step:n, :] + s[0:n - step, :]
            step *= 2
        pooled_ref[:, cols] = (s * (1.0 / w) - u_ref[:, cols]).astype(BF16)

    @pl.when(c == nc - 1)
    def _():
        for k in range(INNER // LANES):
            hfin_ref[k * LANES:(k + 1) * LANES, :] = ht_ref[:, k * LANES:(k + 1) * LANES].T
        ctail_ref[...] = jnp.zeros(ctail_ref.shape, F32)
        ptail_ref[...] = jnp.zeros(ptail_ref.shape, F32)
        for j in range(TPC - hist, TPC):
            r = L + j * TS + TS - 1
            ctail_ref[j - TS:j - TS + 1, :] = xbuf_ref[r:r + 1, :]
        for j in range(1, TPC):
            r = L + j * TS + TS - 1
            ptail_ref[j:j + 1, :] = ubuf_ref[r:r + 1, :]


def _param_specs(index_map):
    return [
        pl.BlockSpec((CONV_K, CONV_DIM), index_map),
        pl.BlockSpec((1, CONV_DIM), index_map),
        pl.BlockSpec((1, LANES), index_map),
        pl.BlockSpec((1, LANES), index_map),
        pl.BlockSpec((1, INNER), index_map),
        pl.BlockSpec((1, INNER), index_map),
    ]


def _seq_scratch(rows):
    return [
        pltpu.VMEM((STATE, INNER), F32),
        pltpu.VMEM((8 + rows, CONV_DIM), F32),
        pltpu.VMEM((PU_BASE + rows, D_MODEL), F32),
        pltpu.VMEM((rows, CONV_DIM), F32),
        pltpu.VMEM((PU_BASE + rows, POOL_GW), F32),
        pltpu.VMEM((PU_BASE + rows, POOL_GW), F32),
        pltpu.VMEM((PU_BASE + rows, POOL_GW), F32),
    ]


def _seq_prompt(proj, dt, h0t, ct0, pt0, e64, params, *, batch, seq):
    nc = seq // CHUNK
    row = lambda b, c: b * nc + c
    const = lambda b, c: (0, 0)
    in_specs = [
        pl.BlockSpec((CHUNK, INNER), lambda b, c: (row(b, c), CB_ZS)),
        pl.BlockSpec((CHUNK, INNER), lambda b, c: (row(b, c), CB_U)),
        pl.BlockSpec((CHUNK, INNER), lambda b, c: (row(b, c), CB_XS)),
        pl.BlockSpec((CHUNK, 2 * GROUPS * STATE), lambda b, c: (row(b, c), CB_BC)),
        pl.BlockSpec((CHUNK, LANES), lambda b, c: (row(b, c), 0)),
        pl.BlockSpec((STATE, INNER), const),
        pl.BlockSpec((8, CONV_DIM), const),
        pl.BlockSpec((POOL_MAX, D_MODEL), const),
        pl.BlockSpec((LANES, INNER), const),
    ] + _param_specs(const)
    return pl.pallas_call(
        _seq_prompt_kernel,
        grid=(batch, nc),
        in_specs=in_specs,
        out_specs=[
            pl.BlockSpec((CHUNK, INNER), lambda b, c: (row(b, c), 0)),
            pl.BlockSpec((CHUNK, D_MODEL), lambda b, c: (row(b, c), 0)),
            pl.BlockSpec((None, INNER, STATE), lambda b, c: (b, 0, 0)),
            pl.BlockSpec((None, 8, CONV_DIM), lambda b, c: (b, 0, 0)),
            pl.BlockSpec((None, POOL_MAX, D_MODEL), lambda b, c: (b, 0, 0)),
        ],
        out_shape=[
            jax.ShapeDtypeStruct((batch * seq, INNER), BF16),
            jax.ShapeDtypeStruct((batch * seq, D_MODEL), BF16),
            jax.ShapeDtypeStruct((batch, INNER, STATE), F32),
            jax.ShapeDtypeStruct((batch, 8, CONV_DIM), F32),
            jax.ShapeDtypeStruct((batch, POOL_MAX, D_MODEL), F32),
        ],
        scratch_shapes=[
            pltpu.VMEM((STATE, INNER), F32),
            pltpu.VMEM((2 * CHUNK, CONV_DIM), F32),
            pltpu.VMEM((2 * CHUNK, D_MODEL), F32),
            pltpu.VMEM((CHUNK, CONV_DIM), F32),
        ],
        compiler_params=pltpu.CompilerParams(
            dimension_semantics=("arbitrary", "arbitrary"),
            vmem_limit_bytes=VMEM_LIMIT),
        name="seq_prompt",
    )(proj, proj, proj, proj, dt, h0t, ct0, pt0, e64, *params)


def _meta_state(proj, dt, e64, params, *, row_block):
    const = lambda b, c: (0, 0)
    zeros = functools.partial(jnp.zeros, dtype=F32)
    in_specs = [
        pl.BlockSpec((N_META, INNER), lambda b, c: (row_block, CB_ZS)),
        pl.BlockSpec((N_META, INNER), lambda b, c: (row_block, CB_U)),
        pl.BlockSpec((N_META, INNER), lambda b, c: (row_block, CB_XS)),
        pl.BlockSpec((N_META, 2 * GROUPS * STATE), lambda b, c: (row_block, CB_BC)),
        pl.BlockSpec((N_META, LANES), lambda b, c: (row_block, 0)),
        pl.BlockSpec((STATE, INNER), const),
        pl.BlockSpec((8, CONV_DIM), const),
        pl.BlockSpec((POOL_MAX, D_MODEL), const),
        pl.BlockSpec((LANES, INNER), const),
    ] + _param_specs(const)
    outs = pl.pallas_call(
        functools.partial(_seq_natural_kernel, with_y=False),
        grid=(1, 1),
        in_specs=in_specs,
        out_specs=[
            pl.BlockSpec((N_META, INNER), const),
            pl.BlockSpec((N_META, D_MODEL), const),
            pl.BlockSpec((STATE, INNER), const),
            pl.BlockSpec((8, CONV_DIM), const),
            pl.BlockSpec((POOL_MAX, D_MODEL), const),
        ],
        out_shape=[
            jax.ShapeDtypeStruct((N_META, INNER), BF16),
            jax.ShapeDtypeStruct((N_META, D_MODEL), BF16),
            jax.ShapeDtypeStruct((STATE, INNER), F32),
            jax.ShapeDtypeStruct((8, CONV_DIM), F32),
            jax.ShapeDtypeStruct((POOL_MAX, D_MODEL), F32),
        ],
        scratch_shapes=_seq_scratch(N_META),
        compiler_params=pltpu.CompilerParams(
            dimension_semantics=("arbitrary", "arbitrary"),
            vmem_limit_bytes=VMEM_LIMIT),
        name="meta_state",
    )(proj, proj, proj, proj, dt, zeros((STATE, INNER)), zeros((8, CONV_DIM)),
      zeros((POOL_MAX, D_MODEL)), e64, *params)
    return outs[2], outs[3], outs[4]


def _seq_sample_kernel(zs_ref, u_ref, xs_ref, bc_ref, dt_ref, chist_ref, phist_ref, h0_ref, e64_ref,
                       cw_ref, cbias_ref, dtb_ref, alog_ref, dskip_ref, nw_ref,
                       y_ref, pooled_ref, hnew_ref, cnew_ref, pnew_ref,
                       xext_ref, uext_ref, act_ref, dv_ref, s_ref):
    R = xs_ref.shape[0]
    T = R // SEQ_BLOCK
    SB = SEQ_BLOCK
    HIST = CONV_K - 1
    PH = POOL_MAX - 1
    half = R

    for k in range(HIST):
        xext_ref[k * SB:(k + 1) * SB, :] = chist_ref[k]
    xext_ref[HIST * SB:HIST * SB + R, 0:INNER] = xs_ref[...]
    xext_ref[HIST * SB:HIST * SB + R, INNER:CONV_DIM] = bc_ref[...]
    for cs in range(CONV_DIM // GROUP_W):
        cols = slice(cs * GROUP_W, (cs + 1) * GROUP_W)
        acc = cbias_ref[:, cols] + xext_ref[0:R, cols] * cw_ref[0:1, cols]
        for k in range(1, CONV_K):
            acc = acc + xext_ref[k * SB:k * SB + R, cols] * cw_ref[k:k + 1, cols]
        act_ref[:, cols] = _silu(acc)
    for k in range(HIST):
        cnew_ref[k] = xext_ref[R + k * SB:R + (k + 1) * SB, :]

    seq_of = lambda x: jnp.bitwise_and(x, SB - 1)
    step_of = lambda x: _div_pow2(x, SB)
    r_i = _iota2((R, R), 0)
    c_i = _iota2((R, R), 1)
    same = seq_of(r_i) == seq_of(c_i)
    tri = jnp.where(same & (step_of(c_i) <= step_of(r_i)), 1.0, 0.0).astype(BF16)
    ones_bd = jnp.where(same, 1.0, 0.0).astype(BF16)
    dtv, da = _dt_terms(dt_ref, dtb_ref, alog_ref)
    da_parts = _split3(da)
    acum = _dot(tri, da_parts[0]) + _dot(tri, da_parts[1]) + _dot(tri, da_parts[2])
    a_end = (_dot(ones_bd, da_parts[0]) + _dot(ones_bd, da_parts[1])
             + _dot(ones_bd, da_parts[2]))

    acum_t2 = jnp.concatenate([acum, acum], axis=0).T
    aend_t = _pad_rows(a_end, LANES).T

    r_s = _iota2((LANES, SB * STATE), 0)
    c_s = _iota2((LANES, SB * STATE), 1)
    sel = jnp.where(r_s == _div_pow2(c_s, STATE), 1.0, 0.0).astype(BF16)
    dv_ref[...] = jnp.exp(_dot_exact(_split3(aend_t), sel))

    lane2 = _iota2((R, 2 * R), 1)
    r_2 = _iota2((R, 2 * R), 0)
    c_2 = jnp.bitwise_and(lane2, R - 1)
    mask2 = (seq_of(r_2) == seq_of(c_2)) & (step_of(c_2) <= step_of(r_2))
    first_half = lane2 < half
    lane = _iota2((R, LANES), 1)
    r_b = _iota2((R, SB * STATE), 0)
    c_b = _iota2((R, SB * STATE), 1)
    bd_sel = seq_of(r_b) == _div_pow2(c_b, STATE)
    stacked = _split3(jnp.concatenate([acum, dtv, a_end], axis=0))

    for g in range(GROUPS):
        cols = slice(g * GROUP_W, (g + 1) * GROUP_W)
        ex = _dot_exact(stacked, e64_ref[:, cols])
        acx = ex[0:R, :]
        dtx = ex[R:2 * R, :]
        aendx = ex[2 * R:3 * R, :]
        xs_g = act_ref[:, cols]
        b_g = act_ref[:, INNER + g * STATE:INNER + (g + 1) * STATE]
        c_f = act_ref[:, INNER + GROUPS * STATE + g * STATE:INNER + GROUPS * STATE + (g + 1) * STATE]
        c_g = c_f.astype(BF16)
        xdt = xs_g * dtx
        xdtw = xdt * jnp.exp(aendx - acx)

        c_big = jnp.where(bd_sel, jnp.concatenate([c_f] * SB, axis=1), 0.0).astype(BF16)
        h_cat = jnp.concatenate(
            [h0_ref[i, g * GROUP_W:(g + 1) * GROUP_W, :].astype(BF16) for i in range(SB)], axis=1)
        yoff = _dot_nt(c_big, h_cat)

        xt = _pad_rows(xdtw, LANES).T
        b_big = jnp.where(bd_sel, jnp.concatenate([b_g] * SB, axis=1), 0.0)
        b_big = _pad_rows(b_big, LANES)
        s_ref[...] = _dot(xt.astype(BF16), b_big.astype(BF16))
        for i in range(SB):
            for hh in range(GROUP_W // HEADDIM):
                h = g * (GROUP_W // HEADDIM) + hh
                rows = slice(h * HEADDIM, (h + 1) * HEADDIM)
                hnew_ref[i, rows, :] = (
                    h0_ref[i, rows, :] * dv_ref[h:h + 1, i * STATE:(i + 1) * STATE]
                    + s_ref[hh * HEADDIM:(hh + 1) * HEADDIM, i * STATE:(i + 1) * STATE])

        cb2 = _dot_nt(c_g, jnp.concatenate([b_g, b_g], axis=0).astype(BF16))
        yds = []
        for q in range(GROUP_W // LANES):
            ha = g * 8 + 2 * q
            colc = jnp.where(first_half, acum[:, ha:ha + 1], acum[:, ha + 1:ha + 2])
            rowc = jnp.where(first_half[0:1, :], acum_t2[ha:ha + 1, :], acum_t2[ha + 1:ha + 2, :])
            dec = jnp.where(mask2, jnp.exp(colc - rowc), 0.0)
            m_p = (cb2 * dec).astype(BF16)
            xp = xdt[:, q * LANES:(q + 1) * LANES]
            r_m = jnp.concatenate([jnp.where(lane < HEADDIM, xp, 0.0),
                                   jnp.where(lane >= HEADDIM, xp, 0.0)], axis=0).astype(BF16)
            yds.append(_dot(m_p, r_m))
        y_g = jnp.concatenate(yds, axis=1) + yoff * jnp.exp(acx) + xs_g * dskip_ref[:, cols]
        _gate_norm_store(y_g, zs_ref[:, cols], nw_ref[:, cols], y_ref, cols)

    for k in range(PH):
        uext_ref[k * SB:(k + 1) * SB, :] = phist_ref[k]
    uext_ref[PH * SB:PH * SB + R, :] = u_ref[...]
    for k in range(PH):
        pnew_ref[k] = uext_ref[(T + k) * SB:(T + k + 1) * SB, :]
    for gi, w in enumerate(POOL_WINDOWS):
        cols = slice(gi * POOL_GW, (gi + 1) * POOL_GW)
        u_cur = u_ref[:, cols]
        s = u_cur
        for k in range(1, w):
            s = s + uext_ref[(PH - k) * SB:(PH - k) * SB + R, cols]
        pooled_ref[:, cols] = (s * (1.0 / w) - u_cur).astype(BF16)


def _seq_sample(proj, dt, conv_hist, pool_hist, h0, e64, params, *, nseq, t_new):
    rb = SEQ_BLOCK * t_new
    assert 2 * rb == LANES
    const = lambda s: (0, 0)
    in_specs = [
        pl.BlockSpec((rb, INNER), lambda s: (s, CB_ZS)),
        pl.BlockSpec((rb, INNER), lambda s: (s, CB_U)),
        pl.BlockSpec((rb, INNER), lambda s: (s, CB_XS)),
        pl.BlockSpec((rb, 2 * GROUPS * STATE), lambda s: (s, CB_BC)),
        pl.BlockSpec((rb, LANES), lambda s: (s, 0)),
        pl.BlockSpec((CONV_K - 1, SEQ_BLOCK, CONV_DIM), lambda s: (0, s, 0)),
        pl.BlockSpec((POOL_MAX - 1, SEQ_BLOCK, D_MODEL), lambda s: (0, s, 0)),
        pl.BlockSpec((SEQ_BLOCK, INNER, STATE), lambda s: (s, 0, 0)),
        pl.BlockSpec((LANES, INNER), const),
    ] + _param_specs(const)
    return pl.pallas_call(
        _seq_sample_kernel,
        grid=(nseq // SEQ_BLOCK,),
        in_specs=in_specs,
        out_specs=[
            pl.BlockSpec((rb, INNER), lambda s: (s, 0)),
            pl.BlockSpec((rb, D_MODEL), lambda s: (s, 0)),
            pl.BlockSpec((SEQ_BLOCK, INNER, STATE), lambda s: (s, 0, 0)),
            pl.BlockSpec((CONV_K - 1, SEQ_BLOCK, CONV_DIM), lambda s: (0, s, 0)),
            pl.BlockSpec((POOL_MAX - 1, SEQ_BLOCK, D_MODEL), lambda s: (0, s, 0)),
        ],
        out_shape=[
            jax.ShapeDtypeStruct((nseq * t_new, INNER), BF16),
            jax.ShapeDtypeStruct((nseq * t_new, D_MODEL), BF16),
            jax.ShapeDtypeStruct((nseq, INNER, STATE), F32),
            jax.ShapeDtypeStruct((CONV_K - 1, nseq, CONV_DIM), F32),
            jax.ShapeDtypeStruct((POOL_MAX - 1, nseq, D_MODEL), F32),
        ],
        scratch_shapes=[
            pltpu.VMEM((SEQ_BLOCK * (CONV_K - 1 + t_new), CONV_DIM), F32),
            pltpu.VMEM((SEQ_BLOCK * (POOL_MAX - 1 + t_new), D_MODEL), F32),
            pltpu.VMEM((rb, CONV_DIM), F32),
            pltpu.VMEM((LANES, SEQ_BLOCK * STATE), F32),
            pltpu.VMEM((GROUP_W, SEQ_BLOCK * STATE), F32),
        ],
        compiler_params=pltpu.CompilerParams(
            dimension_semantics=("arbitrary",),
            vmem_limit_bytes=VMEM_LIMIT),
        name="seq_sample",
    )(proj, proj, proj, proj, dt, conv_hist, pool_hist, h0, e64, *params)


def _merge_kernel(y_ref, pooled_ref, zp_ref, g1_ref, g2_ref, wmix_ref, bmix_ref, pscale_ref,
                  wps_ref, wpp_ref, o_ref, pout_ref):
    for g in range(len(POOL_WINDOWS)):
        cols = slice(g * POOL_GW, (g + 1) * POOL_GW)
        mixed = _dot(pooled_ref[:, cols], wmix_ref[g]) + bmix_ref[:, cols]
        pout_ref[:, cols] = (mixed * pscale_ref[:, cols] * _silu(zp_ref[:, cols])).astype(BF16)
    bs = _dot(y_ref[...], wps_ref[...])
    bp = _dot(pout_ref[...], wpp_ref[...])
    o_ref[...] = (_sigmoid(g1_ref[...]) * bs + _sigmoid(g2_ref[...]) * bp).astype(BF16)


def _merge(y, pooled, proj, wmix, bmix, pscale, wps, wpp, *, m, tm=256):
    const2 = lambda i: (0, 0)
    resident = dict(pipeline_mode=pl.Buffered(1))
    return pl.pallas_call(
        _merge_kernel,
        grid=(m // tm,),
        in_specs=[
            pl.BlockSpec((tm, INNER), lambda i: (i, 0)),
            pl.BlockSpec((tm, D_MODEL), lambda i: (i, 0)),
            pl.BlockSpec((tm, D_MODEL), lambda i: (i, CB_ZP)),
            pl.BlockSpec((tm, D_MODEL), lambda i: (i, CB_G1)),
            pl.BlockSpec((tm, D_MODEL), lambda i: (i, CB_G2)),
            pl.BlockSpec((len(POOL_WINDOWS), POOL_GW, POOL_GW), lambda i: (0, 0, 0), **resident),
            pl.BlockSpec((1, D_MODEL), const2),
            pl.BlockSpec((1, D_MODEL), const2),
            pl.BlockSpec((INNER, D_MODEL), const2, **resident),
            pl.BlockSpec((D_MODEL, D_MODEL), const2, **resident),
        ],
        out_specs=pl.BlockSpec((tm, D_MODEL), lambda i: (i, 0)),
        out_shape=jax.ShapeDtypeStruct((m, D_MODEL), BF16),
        scratch_shapes=[pltpu.VMEM((tm, D_MODEL), BF16)],
        compiler_params=pltpu.CompilerParams(
            dimension_semantics=("arbitrary",),
            vmem_limit_bytes=VMEM_LIMIT),
        name="merge",
    )(y, pooled, proj, proj, proj, wmix, bmix, pscale, wps, wpp)


def _out_kernel(m_ref, h_ref, wout_ref, fw_ref, o_ref, *, tile_major):
    if tile_major:
        unperm = _tile_major_perm(transpose=True)
        merged = jnp.concatenate(
            [_dot(unperm, m_ref[c * CHUNK:(c + 1) * CHUNK, :]).astype(BF16)
             for c in range(m_ref.shape[0] // CHUNK)], axis=0)
    else:
        merged = m_ref[...]
    hn = h_ref[...] + _dot(merged, wout_ref[...])
    ms = jnp.mean(hn * hn, axis=-1, keepdims=True)
    o_ref[...] = (hn * lax.rsqrt(ms + EPS)) * fw_ref[...]


def _out(merged, h, wout, fw, *, m, tm=512, tile_major=False):
    return pl.pallas_call(
        functools.partial(_out_kernel, tile_major=tile_major),
        grid=(m // tm,),
        in_specs=[
            pl.BlockSpec((tm, D_MODEL), lambda i: (i, 0)),
            pl.BlockSpec((tm, D_MODEL), lambda i: (i, 0)),
            pl.BlockSpec((D_MODEL, D_MODEL), lambda i: (0, 0)),
            pl.BlockSpec((1, D_MODEL), lambda i: (0, 0)),
        ],
        out_specs=pl.BlockSpec((tm, D_MODEL), lambda i: (i, 0)),
        out_shape=jax.ShapeDtypeStruct((m, D_MODEL), F32),
        compiler_params=pltpu.CompilerParams(
            dimension_semantics=("arbitrary",),
            vmem_limit_bytes=VMEM_LIMIT),
        name="out",
    )(merged, h, wout, fw)


def kernel(x_prompt, x_sample, state_conv, state_ssm, state_pool, meta_tokens, norm_w, w_in,
           conv_w, conv_b, dt_bias, a_log, d_skip, ssd_norm_w, w_proj_ssd, pool_mix_w,
           pool_mix_b, pool_scale, w_proj_pool, w_out, final_norm_w):
    batch, seq, _ = x_prompt.shape
    nseq, t_new, _ = x_sample.shape
    assert norm_w.shape[0] == 1, "single layer"
    assert w_in.shape == (1, D_MODEL, IN_ZP + 4 * D_MODEL)

    wps = w_proj_ssd[0].astype(BF16)
    wpp = w_proj_pool[0].astype(BF16)
    wout = w_out[0].astype(BF16)
    wmix = pool_mix_w[0].astype(BF16)
    lane_pad = lambda v: jnp.pad(v.reshape(1, HEADS), ((0, 0), (0, LANES - HEADS)))
    params = (conv_w[0], conv_b[0].reshape(1, CONV_DIM), lane_pad(dt_bias[0]), lane_pad(a_log[0]),
              jnp.repeat(d_skip[0], HEADDIM).reshape(1, INNER), ssd_norm_w[0].reshape(1, INNER))
    e64 = (jnp.arange(LANES)[:, None] == (jnp.arange(INNER)[None, :] // HEADDIM)).astype(BF16)
    nw = norm_w[0].reshape(1, D_MODEL)
    fw = final_norm_w.reshape(1, D_MODEL)
    bmix = pool_mix_b[0].reshape(1, D_MODEL)
    pscale = pool_scale[0].reshape(1, D_MODEL)

    xp = x_prompt.reshape(batch * seq, D_MODEL)
    m_s = nseq * t_new
    nblk = nseq // SEQ_BLOCK
    xs_tm = x_sample.reshape(nblk, SEQ_BLOCK, t_new, D_MODEL).transpose(0, 2, 1, 3)
    x_sm = jnp.concatenate([xs_tm.reshape(m_s, D_MODEL), meta_tokens], axis=0)

    proj_s, dt_s, w_main, w_dt = _prep_proj(x_sm, nw, jnp.swapaxes(w_in, 1, 2)[0])
    xn_p, dt_p = _norm(xp, nw, w_dt)
    proj_p = _proj(xn_p, w_main)

    h0t, ct0, pt0 = _meta_state(proj_s, dt_s, e64, params, row_block=m_s // N_META)

    y_p, pooled_p, ssm_p, conv_p, pool_p = _seq_prompt(
        proj_p, dt_p, h0t, ct0, pt0, e64, params, batch=batch, seq=seq)
    y_s, pooled_s, ssm_s, conv_s, pool_s = _seq_sample(
        proj_s, dt_s, jnp.swapaxes(state_conv[0], 0, 1), jnp.swapaxes(state_pool[0], 0, 1),
        state_ssm[0].reshape(nseq, INNER, STATE), e64, params, nseq=nseq, t_new=t_new)

    merged_p = _merge(y_p, pooled_p, proj_p, wmix, bmix, pscale, wps, wpp, m=batch * seq)
    merged_s = _merge(y_s, pooled_s, proj_s, wmix, bmix, pscale, wps, wpp, m=m_s)
    out_p = _out(merged_p, xp, wout, fw, m=batch * seq, tile_major=True)
    out_s = _out(merged_s, x_sm, wout, fw, m=m_s)

    return (out_p.reshape(batch, seq, D_MODEL),
            out_s.reshape(nblk, t_new, SEQ_BLOCK, D_MODEL).transpose(0, 2, 1, 3).reshape(
                nseq, t_new, D_MODEL),
            conv_p[:, 8 - (CONV_K - 1):][None].astype(state_conv.dtype),
            ssm_p.reshape(1, batch, HEADS, HEADDIM, STATE).astype(state_ssm.dtype),
            pool_p[:, 1:][None].astype(state_pool.dtype),
            jnp.swapaxes(conv_s, 0, 1)[None].astype(state_conv.dtype),
            ssm_s.reshape(1, nseq, HEADS, HEADDIM, STATE).astype(state_ssm.dtype),
            jnp.swapaxes(pool_s, 0, 1)[None].astype(state_pool.dtype))
```

```python
import functools

import jax
import jax.numpy as jnp
from jax import lax
from jax.experimental import pallas as pl
from jax.experimental.pallas import tpu as pltpu

F32 = jnp.float32
BF16 = jnp.bfloat16

D_MODEL = 2048
N_META = 16
EPS = 1e-6
HEADS = 32
HEADDIM = 64
GROUPS = 4
STATE = 128
CONV_K = 4
INNER = HEADS * HEADDIM
GROUP_W = INNER // GROUPS
CONV_DIM = INNER + 2 * GROUPS * STATE
POOL_WINDOWS = (2, 4, 8, 16)
POOL_MAX = 16
POOL_GW = D_MODEL // len(POOL_WINDOWS)

IN_XBC = INNER
IN_DT = IN_XBC + CONV_DIM
IN_ZP = IN_DT + HEADS
N_MAIN = 5 * D_MODEL + CONV_DIM
CB_ZS, CB_ZP, CB_U, CB_G1, CB_G2, CB_XS = 0, 1, 2, 3, 4, 5
CB_BC = 12

LANES = 128
CHUNK = 128
SEQ_BLOCK = 8
VMEM_LIMIT = 60 * 1024 * 1024
PREP_TN = 1024
PU_BASE = 2 * POOL_MAX
TPC = 16
TS = CHUNK // TPC


def _silu(x):
    hx = 0.5 * x
    return hx + hx * jnp.tanh(hx)


def _sigmoid(x):
    return 0.5 + 0.5 * jnp.tanh(0.5 * x)


def _dot(a, b):
    return jnp.dot(a, b, preferred_element_type=F32)


def _dot_nt(a, b):
    return lax.dot_general(a, b, (((1,), (1,)), ((), ())), preferred_element_type=F32)


def _split3(a):
    a1 = a.astype(BF16)
    r1 = a - a1.astype(F32)
    a2 = r1.astype(BF16)
    a3 = (r1 - a2.astype(F32)).astype(BF16)
    return a1, a2, a3


def _dot_exact(parts, b):
    out = _dot(parts[0], b)
    for p in parts[1:]:
        out = out + _dot(p, b)
    return out


def _dot_exact_l(a01, x):
    parts = _split3(x)
    out = _dot(a01, parts[0])
    for p in parts[1:]:
        out = out + _dot(a01, p)
    return out


def _iota2(shape, dim):
    return lax.broadcasted_iota(jnp.int32, shape, dim)


def _div_pow2(x, d):
    assert d & (d - 1) == 0
    return jnp.right_shift(x, d.bit_length() - 1)


def _pad_rows(x, rows):
    if x.shape[0] == rows:
        return x
    return jnp.concatenate([x, jnp.zeros((rows - x.shape[0], x.shape[1]), x.dtype)], axis=0)


def _prep_proj_kernel(x_ref, nw_ref, a_ref, d_ref, o_ref, dt_ref, wm_ref, wdt_ref, xn_ref, *, sub_x, sub_w):
    @pl.when(pl.program_id(0) == 0)
    def _():
        def body(r, carry):
            rows = pl.ds(pl.multiple_of(r * sub_x, sub_x), sub_x)
            x = x_ref[rows, :]
            ms = jnp.mean(x * x, axis=-1, keepdims=True)
            xn_ref[rows, :] = ((x * lax.rsqrt(ms + EPS)) * nw_ref[...]).astype(BF16)
            return carry

        lax.fori_loop(0, x_ref.shape[0] // sub_x, body, 0)
        row = _iota2((LANES, D_MODEL), 0)
        wdt_ref[...] = jnp.where(row < HEADS, d_ref[...], 0.0).astype(BF16)
        dt_ref[...] = _dot_nt(xn_ref[...], wdt_ref[...])

    def cast(r, carry):
        rows = pl.ds(pl.multiple_of(r * sub_w, sub_w), sub_w)
        wm_ref[rows, :] = a_ref[rows, :].astype(BF16)
        return carry

    lax.fori_loop(0, PREP_TN // sub_w, cast, 0)
    o_ref[...] = _dot_nt(xn_ref[...], wm_ref[...])


def _prep_proj(x, norm_w, w_t):
    m = x.shape[0]

    def a_map(c):
        src = jnp.where(c < 2, c * PREP_TN,
                        jnp.where(c < 10, IN_ZP + (c - 2) * PREP_TN, IN_XBC + (c - 10) * PREP_TN))
        return (pl.multiple_of(src, 32), 0)

    return pl.pallas_call(
        functools.partial(_prep_proj_kernel, sub_x=16 if m % 128 else 128, sub_w=256),
        grid=(N_MAIN // PREP_TN,),
        in_specs=[
            pl.BlockSpec((m, D_MODEL), lambda c: (0, 0), pipeline_mode=pl.Buffered(1)),
            pl.BlockSpec((1, D_MODEL), lambda c: (0, 0)),
            pl.BlockSpec((pl.Element(PREP_TN), pl.Element(D_MODEL)), a_map),
            pl.BlockSpec((pl.Element(LANES), pl.Element(D_MODEL)), lambda c: (IN_DT, 0)),
        ],
        out_specs=[
            pl.BlockSpec((m, PREP_TN), lambda c: (0, c)),
            pl.BlockSpec((m, LANES), lambda c: (0, 0)),
            pl.BlockSpec((PREP_TN, D_MODEL), lambda c: (c, 0)),
            pl.BlockSpec((LANES, D_MODEL), lambda c: (0, 0)),
        ],
        out_shape=[
            jax.ShapeDtypeStruct((m, N_MAIN), F32),
            jax.ShapeDtypeStruct((m, LANES), F32),
            jax.ShapeDtypeStruct((N_MAIN, D_MODEL), BF16),
            jax.ShapeDtypeStruct((LANES, D_MODEL), BF16),
        ],
        scratch_shapes=[pltpu.VMEM((m, D_MODEL), BF16)],
        compiler_params=pltpu.CompilerParams(
            dimension_semantics=("arbitrary",), vmem_limit_bytes=VMEM_LIMIT),
        name="prep_proj",
    )(x, norm_w, w_t, w_t)


def _time_of(row):
    return jnp.bitwise_and(row, TS - 1) * TPC + jnp.right_shift(row, TS.bit_length() - 1)


def _tile_major_perm(transpose=False):
    r = _iota2((CHUNK, CHUNK), 0)
    c = _iota2((CHUNK, CHUNK), 1)
    hit = (r == _time_of(c)) if transpose else (c == _time_of(r))
    return jnp.where(hit, 1.0, 0.0).astype(BF16)


def _norm_kernel(x_ref, nw_ref, wdt_ref, xn_ref, dt_ref):
    x = x_ref[...]
    ms = jnp.mean(x * x, axis=-1, keepdims=True)
    xn = ((x * lax.rsqrt(ms + EPS)) * nw_ref[...]).astype(BF16)
    perm = _tile_major_perm()
    for c in range(x_ref.shape[0] // CHUNK):
        rows = slice(c * CHUNK, (c + 1) * CHUNK)
        xn_ref[rows, :] = _dot(perm, xn[rows, :]).astype(BF16)
    dt_ref[...] = _dot_nt(xn_ref[...], wdt_ref[...])


def _norm(x, norm_w, w_dt, *, tm=512):
    m = x.shape[0]
    return pl.pallas_call(
        _norm_kernel,
        grid=(m // tm,),
        in_specs=[
            pl.BlockSpec((tm, D_MODEL), lambda i: (i, 0)),
            pl.BlockSpec((1, D_MODEL), lambda i: (0, 0)),
            pl.BlockSpec((LANES, D_MODEL), lambda i: (0, 0)),
        ],
        out_specs=[
            pl.BlockSpec((tm, D_MODEL), lambda i: (i, 0)),
            pl.BlockSpec((tm, LANES), lambda i: (i, 0)),
        ],
        out_shape=[
            jax.ShapeDtypeStruct((m, D_MODEL), BF16),
            jax.ShapeDtypeStruct((m, LANES), F32),
        ],
        compiler_params=pltpu.CompilerParams(
            dimension_semantics=("arbitrary",), vmem_limit_bytes=VMEM_LIMIT),
        name="norm",
    )(x, norm_w, w_dt)


def _proj_kernel(xn_ref, w_ref, o_ref):
    o_ref[...] = _dot_nt(xn_ref[...], w_ref[...])


def _proj(xn, w_main, *, tm=2048, tn=1024):
    m = xn.shape[0]
    assert m % tm == 0 and N_MAIN % tn == 0
    return pl.pallas_call(
        _proj_kernel,
        grid=(m // tm, N_MAIN // tn),
        in_specs=[
            pl.BlockSpec((tm, D_MODEL), lambda i, j: (i, 0)),
            pl.BlockSpec((tn, D_MODEL), lambda i, j: (j, 0)),
        ],
        out_specs=pl.BlockSpec((tm, tn), lambda i, j: (i, j)),
        out_shape=jax.ShapeDtypeStruct((m, N_MAIN), F32),
        compiler_params=pltpu.CompilerParams(
            dimension_semantics=("arbitrary", "arbitrary"),
            vmem_limit_bytes=VMEM_LIMIT),
        name="proj",
    )(xn, w_main)


def _conv_silu(ext_ref, first, rows, cw_ref, cbias_ref, act_ref, row0):
    for cs in range(CONV_DIM // GROUP_W):
        cols = slice(cs * GROUP_W, (cs + 1) * GROUP_W)
        acc = cbias_ref[:, cols] + ext_ref[first:first + rows, cols] * cw_ref[0:1, cols]
        for k in range(1, CONV_K):
            acc = acc + ext_ref[first + k:first + k + rows, cols] * cw_ref[k:k + 1, cols]
        act_ref[row0:row0 + rows, cols] = _silu(acc)


def _dt_terms(dt_ref, dtb_ref, alog_ref):
    x = dt_ref[...] + dtb_ref[...]
    dtv = jnp.maximum(x, 0.0) + jnp.log1p(jnp.exp(-jnp.abs(x)))
    da = dtv * (-jnp.exp(alog_ref[...]))
    return dtv, da


def _gate_norm_store(y_g, zs, nw, y_ref, cols):
    y_g = y_g * _silu(zs)
    ms = jnp.mean(y_g * y_g, axis=-1, keepdims=True)
    y_ref[:, cols] = (y_g * lax.rsqrt(ms + EPS) * nw).astype(BF16)


def _pool_prompt(uext_ref, l1_ref, l2_ref, l3_ref, pooled_ref, rows):
    base, end = PU_BASE, PU_BASE + rows
    for gi, w in enumerate(POOL_WINDOWS):
        cols = slice(gi * POOL_GW, (gi + 1) * POOL_GW)
        levels = w.bit_length() - 1
        lo = base - 8 * (levels - 1)
        if levels == 1:
            s = uext_ref[base:end, cols] + uext_ref[base - 1:end - 1, cols]
        else:
            l1_ref[lo:end, :] = uext_ref[lo:end, cols] + uext_ref[lo - 1:end - 1, cols]
            if levels == 2:
                s = l1_ref[base:end, :] + l1_ref[base - 2:end - 2, :]
            else:
                lo2 = lo + 8
                l2_ref[lo2:end, :] = l1_ref[lo2:end, :] + l1_ref[lo2 - 2:end - 2, :]
                if levels == 3:
                    s = l2_ref[base:end, :] + l2_ref[base - 4:end - 4, :]
                else:
                    lo3 = lo2 + 8
                    l3_ref[lo3:end, :] = l2_ref[lo3:end, :] + l2_ref[lo3 - 4:end - 4, :]
                    s = l3_ref[base:end, :] + l3_ref[base - 8:end - 8, :]
        pooled_ref[:, cols] = (s * (1.0 / w) - uext_ref[base:end, cols]).astype(BF16)


def _seq_natural_kernel(zs_ref, u_ref, xs_ref, bc_ref, dt_ref, h0_ref, ct0_ref, pt0_ref, e64_ref,
                        cw_ref, cbias_ref, dtb_ref, alog_ref, dskip_ref, nw_ref,
                        y_ref, pooled_ref, hfin_ref, ctail_ref, ptail_ref,
                        ht_ref, xext_ref, uext_ref, act_ref, l1_ref, l2_ref, l3_ref, *, with_y):
    L = xs_ref.shape[0]
    c = pl.program_id(1)
    nc = pl.num_programs(1)

    @pl.when(c == 0)
    def _():
        ht_ref[...] = h0_ref[...]
        xext_ref[0:8, :] = ct0_ref[...]
        uext_ref[0:POOL_MAX, :] = jnp.zeros((POOL_MAX, D_MODEL), F32)
        uext_ref[POOL_MAX:PU_BASE, :] = pt0_ref[...]

    xext_ref[8:8 + L, 0:INNER] = xs_ref[...]
    xext_ref[8:8 + L, INNER:CONV_DIM] = bc_ref[...]
    _conv_silu(xext_ref, 8 - (CONV_K - 1), L, cw_ref, cbias_ref, act_ref, 0)
    xext_ref[0:8, :] = xext_ref[L:L + 8, :]

    r_i = _iota2((L, L), 0)
    c_i = _iota2((L, L), 1)
    causal = c_i <= r_i
    tri = jnp.where(causal, 1.0, 0.0).astype(BF16)
    dtv, da = _dt_terms(dt_ref, dtb_ref, alog_ref)
    acum = _dot_exact_l(tri, da)
    stacked = _split3(jnp.concatenate([acum, dtv], axis=0))
    if with_y:
        acum_t = acum.T
        lane = _iota2((L, LANES), 1)

    for g in range(GROUPS):
        cols = slice(g * GROUP_W, (g + 1) * GROUP_W)
        ex = _dot_exact(stacked, e64_ref[:, cols])
        acx = ex[0:L, :]
        dtx = ex[L:2 * L, :]
        xs_g = act_ref[:, cols]
        b_g = act_ref[:, INNER + g * STATE:INNER + (g + 1) * STATE]
        xdt = xs_g * dtx
        aend = acx[L - 1:L, :]
        xdtw = _pad_rows(xdt * jnp.exp(aend - acx), LANES).astype(BF16)
        b_t = _pad_rows(b_g, LANES).T
        h_old = ht_ref[:, cols]
        ht_ref[:, cols] = h_old * jnp.exp(aend) + _dot(b_t.astype(BF16), xdtw)

        if with_y:
            c_g = act_ref[:, INNER + GROUPS * STATE + g * STATE:
                          INNER + GROUPS * STATE + (g + 1) * STATE].astype(BF16)
            cb = _dot_nt(c_g, b_g.astype(BF16))
            yoff = _dot(c_g, h_old.astype(BF16)) * jnp.exp(acx)
            yds = []
            for q in range(GROUP_W // LANES):
                ms_ = []
                for h in (g * 8 + 2 * q, g * 8 + 2 * q + 1):
                    seg = acum[:, h:h + 1] - acum_t[h:h + 1, :]
                    dec = jnp.where(causal, jnp.exp(seg), 0.0)
                    ms_.append((cb * dec).astype(BF16))
                m_cat = jnp.concatenate(ms_, axis=1)
                xp = xdt[:, q * LANES:(q + 1) * LANES]
                r_m = jnp.concatenate([jnp.where(lane < HEADDIM, xp, 0.0),
                                       jnp.where(lane >= HEADDIM, xp, 0.0)], axis=0).astype(BF16)
                yds.append(_dot(m_cat, r_m))
            y_g = jnp.concatenate(yds, axis=1) + yoff + xs_g * dskip_ref[:, cols]
            _gate_norm_store(y_g, zs_ref[:, cols], nw_ref[:, cols], y_ref, cols)

    if with_y:
        uext_ref[PU_BASE:PU_BASE + L, :] = u_ref[...]
        _pool_prompt(uext_ref, l1_ref, l2_ref, l3_ref, pooled_ref, L)
        uext_ref[POOL_MAX:PU_BASE, :] = uext_ref[L + POOL_MAX:L + PU_BASE, :]

        @pl.when(c == nc - 1)
        def _():
            for k in range(INNER // LANES):
                hfin_ref[k * LANES:(k + 1) * LANES, :] = ht_ref[:, k * LANES:(k + 1) * LANES].T
            ctail_ref[...] = xext_ref[0:8, :]
            ptail_ref[...] = uext_ref[POOL_MAX:PU_BASE, :]
    else:
        hfin_ref[...] = ht_ref[...]
        ctail_ref[...] = xext_ref[0:8, :]
        ptail_ref[...] = u_ref[...]
        y_ref[...] = jnp.zeros(y_ref.shape, y_ref.dtype)
        pooled_ref[...] = jnp.zeros(pooled_ref.shape, pooled_ref.dtype)


def _shifted_tile(buf_ref, new_ref, j, cols_buf, cols_new):
    rows = slice(j * TS, (j + 1) * TS)
    old = buf_ref[CHUNK + j * TS:CHUNK + (j + 1) * TS, cols_buf]
    new = new_ref[rows, cols_new]
    is_last = _iota2(new.shape, 0) == TS - 1
    buf_ref[rows, cols_buf] = pltpu.roll(jnp.where(is_last, old, new), 1, 0)


def _seq_prompt_kernel(zs_ref, u_ref, xs_ref, bc_ref, dt_ref, h0_ref, ct0_ref, pt0_ref, e64_ref,
                       cw_ref, cbias_ref, dtb_ref, alog_ref, dskip_ref, nw_ref,
                       y_ref, pooled_ref, hfin_ref, ctail_ref, ptail_ref,
                       ht_ref, xbuf_ref, ubuf_ref, act_ref):
    L = CHUNK
    c = pl.program_id(1)
    nc = pl.num_programs(1)
    hist = CONV_K - 1
    all_cols = slice(None)

    @pl.when(c == 0)
    def _():
        ht_ref[...] = h0_ref[...]
        xbuf_ref[L:2 * L, :] = jnp.zeros((L, CONV_DIM), F32)
        ubuf_ref[L:2 * L, :] = jnp.zeros((L, D_MODEL), F32)
        for j in range(TPC - hist, TPC):
            r = L + j * TS + TS - 1
            xbuf_ref[r:r + 1, :] = ct0_ref[j - TS:j - TS + 1, :]
        for j in range(1, TPC):
            r = L + j * TS + TS - 1
            ubuf_ref[r:r + 1, :] = pt0_ref[j:j + 1, :]

    for j in range(TPC - hist, TPC):
        _shifted_tile(xbuf_ref, xs_ref, j, slice(0, INNER), all_cols)
        _shifted_tile(xbuf_ref, bc_ref, j, slice(INNER, CONV_DIM), all_cols)
    xbuf_ref[L:2 * L, 0:INNER] = xs_ref[...]
    xbuf_ref[L:2 * L, INNER:CONV_DIM] = bc_ref[...]
    for cs in range(CONV_DIM // GROUP_W):
        cols = slice(cs * GROUP_W, (cs + 1) * GROUP_W)
        acc = cbias_ref[:, cols] + xbuf_ref[L - hist * TS:2 * L - hist * TS, cols] * cw_ref[0:1, cols]
        for k in range(1, CONV_K):
            back = (hist - k) * TS
            acc = acc + xbuf_ref[L - back:2 * L - back, cols] * cw_ref[k:k + 1, cols]
        act_ref[:, cols] = _silu(acc)

    t_r = _time_of(_iota2((L, L), 0))
    t_c = _time_of(_iota2((L, L), 1))
    causal = t_c <= t_r
    tri = jnp.where(causal, 1.0, 0.0).astype(BF16)
    dtv, da = _dt_terms(dt_ref, dtb_ref, alog_ref)
    acum = _dot_exact_l(tri, da)
    stacked = _split3(jnp.concatenate([acum, dtv], axis=0))
    acum_t = acum.T
    lane = _iota2((L, LANES), 1)

    col_of = lambda g: slice(g * GROUP_W, (g + 1) * GROUP_W)
    exs, cbs, yoffs, holds, bs_ = [], [], [], [], []
    for g in range(GROUPS):
        b_g = act_ref[:, INNER + g * STATE:INNER + (g + 1) * STATE]
        c_g = act_ref[:, INNER + GROUPS * STATE + g * STATE:
                      INNER + GROUPS * STATE + (g + 1) * STATE].astype(BF16)
        h_old = ht_ref[:, col_of(g)]
        exs.append(_dot_exact(stacked, e64_ref[:, col_of(g)]))
        cbs.append(_dot_nt(c_g, b_g.astype(BF16)))
        yoffs.append(_dot(c_g, h_old.astype(BF16)))
        holds.append(h_old)
        bs_.append(b_g)

    for g in range(GROUPS):
        cols = col_of(g)
        acx = exs[g][0:L, :]
        dtx = exs[g][L:2 * L, :]
        xs_g = act_ref[:, cols]
        xdt = xs_g * dtx
        aend = acx[L - 1:L, :]
        xdtw = (xdt * jnp.exp(aend - acx)).astype(BF16)
        ht_ref[:, cols] = holds[g] * jnp.exp(aend) + _dot(bs_[g].T.astype(BF16), xdtw)

        cb = cbs[g]
        yoff = yoffs[g] * jnp.exp(acx)
        yds = []
        for q in range(GROUP_W // LANES):
            ms_ = []
            for h in (g * 8 + 2 * q, g * 8 + 2 * q + 1):
                seg = acum[:, h:h + 1] - acum_t[h:h + 1, :]
                dec = jnp.where(causal, jnp.exp(seg), 0.0)
                ms_.append((cb * dec).astype(BF16))
            m_cat = jnp.concatenate(ms_, axis=1)
            xp = xdt[:, q * LANES:(q + 1) * LANES]
            r_m = jnp.concatenate([jnp.where(lane < HEADDIM, xp, 0.0),
                                   jnp.where(lane >= HEADDIM, xp, 0.0)], axis=0).astype(BF16)
            yds.append(_dot(m_cat, r_m))
        y_g = jnp.concatenate(yds, axis=1) + yoff + xs_g * dskip_ref[:, cols]
        _gate_norm_store(y_g, zs_ref[:, cols], nw_ref[:, cols], y_ref, cols)

    for gi, w in enumerate(POOL_WINDOWS):
        cols = slice(gi * POOL_GW, (gi + 1) * POOL_GW)
        for j in range(TPC - (w - 1), TPC):
            _shifted_tile(ubuf_ref, u_ref, j, cols, cols)
    ubuf_ref[L:2 * L, :] = u_ref[...]
    for gi, w in enumerate(POOL_WINDOWS):
        cols = slice(gi * POOL_GW, (gi + 1) * POOL_GW)
        lo = 2 * L - L - (w - 1) * TS
        s = ubuf_ref[lo:2 * L, cols]
        step = TS
        while step < w * TS:
            n = s.shape[0]
            s = s[step:n, :] + s[0:n - step, :]
            step *= 2
        pooled_ref[:, cols] = (s * (1.0 / w) - u_ref[:, cols]).astype(BF16)

    @pl.when(c == nc - 1)
    def _():
        for k in range(INNER // LANES):
            hfin_ref[k * LANES:(k + 1) * LANES, :] = ht_ref[:, k * LANES:(k + 1) * LANES].T
        ctail_ref[...] = jnp.zeros(ctail_ref.shape, F32)
        ptail_ref[...] = jnp.zeros(ptail_ref.shape, F32)
        for j in range(TPC - hist, TPC):
            r = L + j * TS + TS - 1
            ctail_ref[j - TS:j - TS + 1, :] = xbuf_ref[r:r + 1, :]
        for j in range(1, TPC):
            r = L + j * TS + TS - 1
            ptail_ref[j:j + 1, :] = ubuf_ref[r:r + 1, :]


def _param_specs(index_map):
    return [
        pl.BlockSpec((CONV_K, CONV_DIM), index_map),
        pl.BlockSpec((1, CONV_DIM), index_map),
        pl.BlockSpec((1, LANES), index_map),
        pl.BlockSpec((1, LANES), index_map),
        pl.BlockSpec((1, INNER), index_map),
        pl.BlockSpec((1, INNER), index_map),
    ]


def _seq_scratch(rows):
    return [
        pltpu.VMEM((STATE, INNER), F32),
        pltpu.VMEM((8 + rows, CONV_DIM), F32),
        pltpu.VMEM((PU_BASE + rows, D_MODEL), F32),
        pltpu.VMEM((rows, CONV_DIM), F32),
        pltpu.VMEM((PU_BASE + rows, POOL_GW), F32),
        pltpu.VMEM((PU_BASE + rows, POOL_GW), F32),
        pltpu.VMEM((PU_BASE + rows, POOL_GW), F32),
    ]


def _seq_prompt(proj, dt, h0t, ct0, pt0, e64, params, *, batch, seq):
    nc = seq // CHUNK
    row = lambda b, c: b * nc + c
    const = lambda b, c: (0, 0)
    in_specs = [
        pl.BlockSpec((CHUNK, INNER), lambda b, c: (row(b, c), CB_ZS)),
        pl.BlockSpec((CHUNK, INNER), lambda b, c: (row(b, c), CB_U)),
        pl.BlockSpec((CHUNK, INNER), lambda b, c: (row(b, c), CB_XS)),
        pl.BlockSpec((CHUNK, 2 * GROUPS * STATE), lambda b, c: (row(b, c), CB_BC)),
        pl.BlockSpec((CHUNK, LANES), lambda b, c: (row(b, c), 0)),
        pl.BlockSpec((STATE, INNER), const),
        pl.BlockSpec((8, CONV_DIM), const),
        pl.BlockSpec((POOL_MAX, D_MODEL), const),
        pl.BlockSpec((LANES, INNER), const),
    ] + _param_specs(const)
    return pl.pallas_call(
        _seq_prompt_kernel,
        grid=(batch, nc),
        in_specs=in_specs,
        out_specs=[
            pl.BlockSpec((CHUNK, INNER), lambda b, c: (row(b, c), 0)),
            pl.BlockSpec((CHUNK, D_MODEL), lambda b, c: (row(b, c), 0)),
            pl.BlockSpec((None, INNER, STATE), lambda b, c: (b, 0, 0)),
            pl.BlockSpec((None, 8, CONV_DIM), lambda b, c: (b, 0, 0)),
            pl.BlockSpec((None, POOL_MAX, D_MODEL), lambda b, c: (b, 0, 0)),
        ],
        out_shape=[
            jax.ShapeDtypeStruct((batch * seq, INNER), BF16),
            jax.ShapeDtypeStruct((batch * seq, D_MODEL), BF16),
            jax.ShapeDtypeStruct((batch, INNER, STATE), F32),
            jax.ShapeDtypeStruct((batch, 8, CONV_DIM), F32),
            jax.ShapeDtypeStruct((batch, POOL_MAX, D_MODEL), F32),
        ],
        scratch_shapes=[
            pltpu.VMEM((STATE, INNER), F32),
            pltpu.VMEM((2 * CHUNK, CONV_DIM), F32),
            pltpu.VMEM((2 * CHUNK, D_MODEL), F32),
            pltpu.VMEM((CHUNK, CONV_DIM), F32),
        ],
        compiler_params=pltpu.CompilerParams(
            dimension_semantics=("arbitrary", "arbitrary"),
            vmem_limit_bytes=VMEM_LIMIT),
        name="seq_prompt",
    )(proj, proj, proj, proj, dt, h0t, ct0, pt0, e64, *params)


def _meta_state(proj, dt, e64, params, *, row_block):
    const = lambda b, c: (0, 0)
    zeros = functools.partial(jnp.zeros, dtype=F32)
    in_specs = [
        pl.BlockSpec((N_META, INNER), lambda b, c: (row_block, CB_ZS)),
        pl.BlockSpec((N_META, INNER), lambda b, c: (row_block, CB_U)),
        pl.BlockSpec((N_META, INNER), lambda b, c: (row_block, CB_XS)),
        pl.BlockSpec((N_META, 2 * GROUPS * STATE), lambda b, c: (row_block, CB_BC)),
        pl.BlockSpec((N_META, LANES), lambda b, c: (row_block, 0)),
        pl.BlockSpec((STATE, INNER), const),
        pl.BlockSpec((8, CONV_DIM), const),
        pl.BlockSpec((POOL_MAX, D_MODEL), const),
        pl.BlockSpec((LANES, INNER), const),
    ] + _param_specs(const)
    outs = pl.pallas_call(
        functools.partial(_seq_natural_kernel, with_y=False),
        grid=(1, 1),
        in_specs=in_specs,
        out_specs=[
            pl.BlockSpec((N_META, INNER), const),
            pl.BlockSpec((N_META, D_MODEL), const),
            pl.BlockSpec((STATE, INNER), const),
            pl.BlockSpec((8, CONV_DIM), const),
            pl.BlockSpec((POOL_MAX, D_MODEL), const),
        ],
        out_shape=[
            jax.ShapeDtypeStruct((N_META, INNER), BF16),
            jax.ShapeDtypeStruct((N_META, D_MODEL), BF16),
            jax.ShapeDtypeStruct((STATE, INNER), F32),
            jax.ShapeDtypeStruct((8, CONV_DIM), F32),
            jax.ShapeDtypeStruct((POOL_MAX, D_MODEL), F32),
        ],
        scratch_shapes=_seq_scratch(N_META),
        compiler_params=pltpu.CompilerParams(
            dimension_semantics=("arbitrary", "arbitrary"),
            vmem_limit_bytes=VMEM_LIMIT),
        name="meta_state",
    )(proj, proj, proj, proj, dt, zeros((STATE, INNER)), zeros((8, CONV_DIM)),
      zeros((POOL_MAX, D_MODEL)), e64, *params)
    return outs[2], outs[3], outs[4]


def _seq_sample_kernel(zs_ref, u_ref, xs_ref, bc_ref, dt_ref, chist_ref, phist_ref, h0_ref, e64_ref,
                       cw_ref, cbias_ref, dtb_ref, alog_ref, dskip_ref, nw_ref,
                       y_ref, pooled_ref, hnew_ref, cnew_ref, pnew_ref,
                       xext_ref, uext_ref, act_ref, dv_ref, s_ref):
    R = xs_ref.shape[0]
    T = R // SEQ_BLOCK
    SB = SEQ_BLOCK
    HIST = CONV_K - 1
    PH = POOL_MAX - 1
    half = R

    for k in range(HIST):
        xext_ref[k * SB:(k + 1) * SB, :] = chist_ref[k]
    xext_ref[HIST * SB:HIST * SB + R, 0:INNER] = xs_ref[...]
    xext_ref[HIST * SB:HIST * SB + R, INNER:CONV_DIM] = bc_ref[...]
    for cs in range(CONV_DIM // GROUP_W):
        cols = slice(cs * GROUP_W, (cs + 1) * GROUP_W)
        acc = cbias_ref[:, cols] + xext_ref[0:R, cols] * cw_ref[0:1, cols]
        for k in range(1, CONV_K):
            acc = acc + xext_ref[k * SB:k * SB + R, cols] * cw_ref[k:k + 1, cols]
        act_ref[:, cols] = _silu(acc)
    for k in range(HIST):
        cnew_ref[k] = xext_ref[R + k * SB:R + (k + 1) * SB, :]

    seq_of = lambda x: jnp.bitwise_and(x, SB - 1)
    step_of = lambda x: _div_pow2(x, SB)
    r_i = _iota2((R, R), 0)
    c_i = _iota2((R, R), 1)
    same = seq_of(r_i) == seq_of(c_i)
    tri = jnp.where(same & (step_of(c_i) <= step_of(r_i)), 1.0, 0.0).astype(BF16)
    ones_bd = jnp.where(same, 1.0, 0.0).astype(BF16)
    dtv, da = _dt_terms(dt_ref, dtb_ref, alog_ref)
    da_parts = _split3(da)
    acum = _dot(tri, da_parts[0]) + _dot(tri, da_parts[1]) + _dot(tri, da_parts[2])
    a_end = (_dot(ones_bd, da_parts[0]) + _dot(ones_bd, da_parts[1])
             + _dot(ones_bd, da_parts[2]))

    acum_t2 = jnp.concatenate([acum, acum], axis=0).T
    aend_t = _pad_rows(a_end, LANES).T

    r_s = _iota2((LANES, SB * STATE), 0)
    c_s = _iota2((LANES, SB * STATE), 1)
    sel = jnp.where(r_s == _div_pow2(c_s, STATE), 1.0, 0.0).astype(BF16)
    dv_ref[...] = jnp.exp(_dot_exact(_split3(aend_t), sel))

    lane2 = _iota2((R, 2 * R), 1)
    r_2 = _iota2((R, 2 * R), 0)
    c_2 = jnp.bitwise_and(lane2, R - 1)
    mask2 = (seq_of(r_2) == seq_of(c_2)) & (step_of(c_2) <= step_of(r_2))
    first_half = lane2 < half
    lane = _iota2((R, LANES), 1)
    r_b = _iota2((R, SB * STATE), 0)
    c_b = _iota2((R, SB * STATE), 1)
    bd_sel = seq_of(r_b) == _div_pow2(c_b, STATE)
    stacked = _split3(jnp.concatenate([acum, dtv, a_end], axis=0))

    for g in range(GROUPS):
        cols = slice(g * GROUP_W, (g + 1) * GROUP_W)
        ex = _dot_exact(stacked, e64_ref[:, cols])
        acx = ex[0:R, :]
        dtx = ex[R:2 * R, :]
        aendx = ex[2 * R:3 * R, :]
        xs_g = act_ref[:, cols]
        b_g = act_ref[:, INNER + g * STATE:INNER + (g + 1) * STATE]
        c_f = act_ref[:, INNER + GROUPS * STATE + g * STATE:INNER + GROUPS * STATE + (g + 1) * STATE]
        c_g = c_f.astype(BF16)
        xdt = xs_g * dtx
        xdtw = xdt * jnp.exp(aendx - acx)

        c_big = jnp.where(bd_sel, jnp.concatenate([c_f] * SB, axis=1), 0.0).astype(BF16)
        h_cat = jnp.concatenate(
            [h0_ref[i, g * GROUP_W:(g + 1) * GROUP_W, :].astype(BF16) for i in range(SB)], axis=1)
        yoff = _dot_nt(c_big, h_cat)

        xt = _pad_rows(xdtw, LANES).T
        b_big = jnp.where(bd_sel, jnp.concatenate([b_g] * SB, axis=1), 0.0)
        b_big = _pad_rows(b_big, LANES)
        s_ref[...] = _dot(xt.astype(BF16), b_big.astype(BF16))
        for i in range(SB):
            for hh in range(GROUP_W // HEADDIM):
                h = g * (GROUP_W // HEADDIM) + hh
                rows = slice(h * HEADDIM, (h + 1) * HEADDIM)
                hnew_ref[i, rows, :] = (
                    h0_ref[i, rows, :] * dv_ref[h:h + 1, i * STATE:(i + 1) * STATE]
                    + s_ref[hh * HEADDIM:(hh + 1) * HEADDIM, i * STATE:(i + 1) * STATE])

        cb2 = _dot_nt(c_g, jnp.concatenate([b_g, b_g], axis=0).astype(BF16))
        yds = []
        for q in range(GROUP_W // LANES):
            ha = g * 8 + 2 * q
            colc = jnp.where(first_half, acum[:, ha:ha + 1], acum[:, ha + 1:ha + 2])
            rowc = jnp.where(first_half[0:1, :], acum_t2[ha:ha + 1, :], acum_t2[ha + 1:ha + 2, :])
            dec = jnp.where(mask2, jnp.exp(colc - rowc), 0.0)
            m_p = (cb2 * dec).astype(BF16)
            xp = xdt[:, q * LANES:(q + 1) * LANES]
            r_m = jnp.concatenate([jnp.where(lane < HEADDIM, xp, 0.0),
                                   jnp.where(lane >= HEADDIM, xp, 0.0)], axis=0).astype(BF16)
            yds.append(_dot(m_p, r_m))
        y_g = jnp.concatenate(yds, axis=1) + yoff * jnp.exp(acx) + xs_g * dskip_ref[:, cols]
        _gate_norm_store(y_g, zs_ref[:, cols], nw_ref[:, cols], y_ref, cols)

    for k in range(PH):
        uext_ref[k * SB:(k + 1) * SB, :] = phist_ref[k]
    uext_ref[PH * SB:PH * SB + R, :] = u_ref[...]
    for k in range(PH):
        pnew_ref[k] = uext_ref[(T + k) * SB:(T + k + 1) * SB, :]
    for gi, w in enumerate(POOL_WINDOWS):
        cols = slice(gi * POOL_GW, (gi + 1) * POOL_GW)
        u_cur = u_ref[:, cols]
        s = u_cur
        for k in range(1, w):
            s = s + uext_ref[(PH - k) * SB:(PH - k) * SB + R, cols]
        pooled_ref[:, cols] = (s * (1.0 / w) - u_cur).astype(BF16)


def _seq_sample(proj, dt, conv_hist, pool_hist, h0, e64, params, *, nseq, t_new):
    rb = SEQ_BLOCK * t_new
    assert 2 * rb == LANES
    const = lambda s: (0, 0)
    in_specs = [
        pl.BlockSpec((rb, INNER), lambda s: (s, CB_ZS)),
        pl.BlockSpec((rb, INNER), lambda s: (s, CB_U)),
        pl.BlockSpec((rb, INNER), lambda s: (s, CB_XS)),
        pl.BlockSpec((rb, 2 * GROUPS * STATE), lambda s: (s, CB_BC)),
        pl.BlockSpec((rb, LANES), lambda s: (s, 0)),
        pl.BlockSpec((CONV_K - 1, SEQ_BLOCK, CONV_DIM), lambda s: (0, s, 0)),
        pl.BlockSpec((POOL_MAX - 1, SEQ_BLOCK, D_MODEL), lambda s: (0, s, 0)),
        pl.BlockSpec((SEQ_BLOCK, INNER, STATE), lambda s: (s, 0, 0)),
        pl.BlockSpec((LANES, INNER), const),
    ] + _param_specs(const)
    return pl.pallas_call(
        _seq_sample_kernel,
        grid=(nseq // SEQ_BLOCK,),
        in_specs=in_specs,
        out_specs=[
            pl.BlockSpec((rb, INNER), lambda s: (s, 0)),
            pl.BlockSpec((rb, D_MODEL), lambda s: (s, 0)),
            pl.BlockSpec((SEQ_BLOCK, INNER, STATE), lambda s: (s, 0, 0)),
            pl.BlockSpec((CONV_K - 1, SEQ_BLOCK, CONV_DIM), lambda s: (0, s, 0)),
            pl.BlockSpec((POOL_MAX - 1, SEQ_BLOCK, D_MODEL), lambda s: (0, s, 0)),
        ],
        out_shape=[
            jax.ShapeDtypeStruct((nseq * t_new, INNER), BF16),
            jax.ShapeDtypeStruct((nseq * t_new, D_MODEL), BF16),
            jax.ShapeDtypeStruct((nseq, INNER, STATE), F32),
            jax.ShapeDtypeStruct((CONV_K - 1, nseq, CONV_DIM), F32),
            jax.ShapeDtypeStruct((POOL_MAX - 1, nseq, D_MODEL), F32),
        ],
        scratch_shapes=[
            pltpu.VMEM((SEQ_BLOCK * (CONV_K - 1 + t_new), CONV_DIM), F32),
            pltpu.VMEM((SEQ_BLOCK * (POOL_MAX - 1 + t_new), D_MODEL), F32),
            pltpu.VMEM((rb, CONV_DIM), F32),
            pltpu.VMEM((LANES, SEQ_BLOCK * STATE), F32),
            pltpu.VMEM((GROUP_W, SEQ_BLOCK * STATE), F32),
        ],
        compiler_params=pltpu.CompilerParams(
            dimension_semantics=("arbitrary",),
            vmem_limit_bytes=VMEM_LIMIT),
        name="seq_sample",
    )(proj, proj, proj, proj, dt, conv_hist, pool_hist, h0, e64, *params)


def _merge_kernel(y_ref, pooled_ref, zp_ref, g1_ref, g2_ref, wmix_ref, bmix_ref, pscale_ref,
                  wps_ref, wpp_ref, o_ref, pout_ref):
    for g in range(len(POOL_WINDOWS)):
        cols = slice(g * POOL_GW, (g + 1) * POOL_GW)
        mixed = _dot(pooled_ref[:, cols], wmix_ref[g]) + bmix_ref[:, cols]
        pout_ref[:, cols] = (mixed * pscale_ref[:, cols] * _silu(zp_ref[:, cols])).astype(BF16)
    bs = _dot(y_ref[...], wps_ref[...])
    bp = _dot(pout_ref[...], wpp_ref[...])
    o_ref[...] = (_sigmoid(g1_ref[...]) * bs + _sigmoid(g2_ref[...]) * bp).astype(BF16)


def _merge(y, pooled, proj, wmix, bmix, pscale, wps, wpp, *, m, tm=256):
    const2 = lambda i: (0, 0)
    resident = dict(pipeline_mode=pl.Buffered(1))
    return pl.pallas_call(
        _merge_kernel,
        grid=(m // tm,),
        in_specs=[
            pl.BlockSpec((tm, INNER), lambda i: (i, 0)),
            pl.BlockSpec((tm, D_MODEL), lambda i: (i, 0)),
            pl.BlockSpec((tm, D_MODEL), lambda i: (i, CB_ZP)),
            pl.BlockSpec((tm, D_MODEL), lambda i: (i, CB_G1)),
            pl.BlockSpec((tm, D_MODEL), lambda i: (i, CB_G2)),
            pl.BlockSpec((len(POOL_WINDOWS), POOL_GW, POOL_GW), lambda i: (0, 0, 0), **resident),
            pl.BlockSpec((1, D_MODEL), const2),
            pl.BlockSpec((1, D_MODEL), const2),
            pl.BlockSpec((INNER, D_MODEL), const2, **resident),
            pl.BlockSpec((D_MODEL, D_MODEL), const2, **resident),
        ],
        out_specs=pl.BlockSpec((tm, D_MODEL), lambda i: (i, 0)),
        out_shape=jax.ShapeDtypeStruct((m, D_MODEL), BF16),
        scratch_shapes=[pltpu.VMEM((tm, D_MODEL), BF16)],
        compiler_params=pltpu.CompilerParams(
            dimension_semantics=("arbitrary",),
            vmem_limit_bytes=VMEM_LIMIT),
        name="merge",
    )(y, pooled, proj, proj, proj, wmix, bmix, pscale, wps, wpp)


def _out_kernel(m_ref, h_ref, wout_ref, fw_ref, o_ref, *, tile_major):
    if tile_major:
        unperm = _tile_major_perm(transpose=True)
        merged = jnp.concatenate(
            [_dot(unperm, m_ref[c * CHUNK:(c + 1) * CHUNK, :]).astype(BF16)
             for c in range(m_ref.shape[0] // CHUNK)], axis=0)
    else:
        merged = m_ref[...]
    hn = h_ref[...] + _dot(merged, wout_ref[...])
    ms = jnp.mean(hn * hn, axis=-1, keepdims=True)
    o_ref[...] = (hn * lax.rsqrt(ms + EPS)) * fw_ref[...]


def _out(merged, h, wout, fw, *, m, tm=512, tile_major=False):
    return pl.pallas_call(
        functools.partial(_out_kernel, tile_major=tile_major),
        grid=(m // tm,),
        in_specs=[
            pl.BlockSpec((tm, D_MODEL), lambda i: (i, 0)),
            pl.BlockSpec((tm, D_MODEL), lambda i: (i, 0)),
            pl.BlockSpec((D_MODEL, D_MODEL), lambda i: (0, 0)),
            pl.BlockSpec((1, D_MODEL), lambda i: (0, 0)),
        ],
        out_specs=pl.BlockSpec((tm, D_MODEL), lambda i: (i, 0)),
        out_shape=jax.ShapeDtypeStruct((m, D_MODEL), F32),
        compiler_params=pltpu.CompilerParams(
            dimension_semantics=("arbitrary",),
            vmem_limit_bytes=VMEM_LIMIT),
        name="out",
    )(merged, h, wout, fw)


def kernel(x_prompt, x_sample, state_conv, state_ssm, state_pool, meta_tokens, norm_w, w_in,
           conv_w, conv_b, dt_bias, a_log, d_skip, ssd_norm_w, w_proj_ssd, pool_mix_w,
           pool_mix_b, pool_scale, w_proj_pool, w_out, final_norm_w):
    batch, seq, _ = x_prompt.shape
    nseq, t_new, _ = x_sample.shape
    assert norm_w.shape[0] == 1, "single layer"
    assert w_in.shape == (1, D_MODEL, IN_ZP + 4 * D_MODEL)

    wps = w_proj_ssd[0].astype(BF16)
    wpp = w_proj_pool[0].astype(BF16)
    wout = w_out[0].astype(BF16)
    wmix = pool_mix_w[0].astype(BF16)
    lane_pad = lambda v: jnp.pad(v.reshape(1, HEADS), ((0, 0), (0, LANES - HEADS)))
    params = (conv_w[0], conv_b[0].reshape(1, CONV_DIM), lane_pad(dt_bias[0]), lane_pad(a_log[0]),
              jnp.repeat(d_skip[0], HEADDIM).reshape(1, INNER), ssd_norm_w[0].reshape(1, INNER))
    e64 = (jnp.arange(LANES)[:, None] == (jnp.arange(INNER)[None, :] // HEADDIM)).astype(BF16)
    nw = norm_w[0].reshape(1, D_MODEL)
    fw = final_norm_w.reshape(1, D_MODEL)
    bmix = pool_mix_b[0].reshape(1, D_MODEL)
    pscale = pool_scale[0].reshape(1, D_MODEL)

    xp = x_prompt.reshape(batch * seq, D_MODEL)
    m_s = nseq * t_new
    nblk = nseq // SEQ_BLOCK
    xs_tm = x_sample.reshape(nblk, SEQ_BLOCK, t_new, D_MODEL).transpose(0, 2, 1, 3)
    x_sm = jnp.concatenate([xs_tm.reshape(m_s, D_MODEL), meta_tokens], axis=0)

    proj_s, dt_s, w_main, w_dt = _prep_proj(x_sm, nw, jnp.swapaxes(w_in, 1, 2)[0])
    xn_p, dt_p = _norm(xp, nw, w_dt)
    proj_p = _proj(xn_p, w_main)

    h0t, ct0, pt0 = _meta_state(proj_s, dt_s, e64, params, row_block=m_s // N_META)

    y_p, pooled_p, ssm_p, conv_p, pool_p = _seq_prompt(
        proj_p, dt_p, h0t, ct0, pt0, e64, params, batch=batch, seq=seq)
    y_s, pooled_s, ssm_s, conv_s, pool_s = _seq_sample(
        proj_s, dt_s, jnp.swapaxes(state_conv[0], 0, 1), jnp.swapaxes(state_pool[0], 0, 1),
        state_ssm[0].reshape(nseq, INNER, STATE), e64, params, nseq=nseq, t_new=t_new)

    merged_p = _merge(y_p, pooled_p, proj_p, wmix, bmix, pscale, wps, wpp, m=batch * seq)
    merged_s = _merge(y_s, pooled_s, proj_s, wmix, bmix, pscale, wps, wpp, m=m_s)
    out_p = _out(merged_p, xp, wout, fw, m=batch * seq, tile_major=True)
    out_s = _out(merged_s, x_sm, wout, fw, m=m_s)

    return (out_p.reshape(batch, seq, D_MODEL),
            out_s.reshape(nblk, t_new, SEQ_BLOCK, D_MODEL).transpose(0, 2, 1, 3).reshape(
                nseq, t_new, D_MODEL),
            conv_p[:, 8 - (CONV_K - 1):][None].astype(state_conv.dtype),
            ssm_p.reshape(1, batch, HEADS, HEADDIM, STATE).astype(state_ssm.dtype),
            pool_p[:, 1:][None].astype(state_pool.dtype),
            jnp.swapaxes(conv_s, 0, 1)[None].astype(state_conv.dtype),
            ssm_s.reshape(1, nseq, HEADS, HEADDIM, STATE).astype(state_ssm.dtype),
            jnp.swapaxes(pool_s, 0, 1)[None].astype(state_pool.dtype))
```
